```python
import jax, jax.numpy as jnp
from jax import lax
import numpy as np

D_MODEL = 1024
BATCH = 32
SEQ = 2048
DEPTH = 2

GRID_W = 64
CTX_LEN = 256
NORM_EPS = 1e-6
ROPE_BASE = 10000.0
Q_BLOCK = 128

MLA_HEADS = 6
MLA_Q_RANK = 256
MLA_KV_RANK = 128
MLA_NOPE = 64
MLA_ROPE = 32
MLA_V = 64
MLA_WIDTH = MLA_HEADS * MLA_V
MLA_SCALE = (MLA_NOPE + MLA_ROPE) ** -0.5

SWA_HEADS = 6
SWA_KV_HEADS = 2
SWA_REP = SWA_HEADS // SWA_KV_HEADS
SWA_HEAD_DIM = 64
SWA_WINDOW = 128
SWA_WIDTH = SWA_HEADS * SWA_HEAD_DIM
SWA_SCALE = SWA_HEAD_DIM ** -0.5

CONV_CH = 256
CONV_K = 3

MIX_WIDTH = MLA_WIDTH + SWA_WIDTH + CONV_CH
IN_SPLITS = (MLA_Q_RANK, MLA_KV_RANK, MLA_ROPE, SWA_WIDTH, SWA_KV_HEADS * SWA_HEAD_DIM, SWA_KV_HEADS * SWA_HEAD_DIM, CONV_CH, CONV_CH, CONV_CH)
IN_WIDTH = sum(IN_SPLITS)

N_EXPERTS = 16
EXPERT_FF = 512
EC_CAPACITY_FACTOR = 2
N_MOD = 6

kernel_name = 'hybrid_mla_swa_conv_ec_moe_dit'


def rmsnorm(x, g):
    xf = x.astype(jnp.float32)
    y = xf * lax.rsqrt(jnp.mean(xf * xf, axis=-1, keepdims=True) + NORM_EPS)
    return (y * g.astype(jnp.float32)).astype(x.dtype)


def axial_rope(n_tokens, rot_dim, dtype):
    rows = n_tokens // GRID_W
    row = jnp.repeat(jnp.arange(rows, dtype=jnp.float32), GRID_W)
    col = jnp.tile(jnp.arange(GRID_W, dtype=jnp.float32), rows)
    n_freq = rot_dim // 4
    inv = ROPE_BASE ** (-jnp.arange(n_freq, dtype=jnp.float32) / n_freq)
    ang = jnp.concatenate([row[:, None] * inv, col[:, None] * inv], axis=-1)
    return jnp.cos(ang).astype(dtype), jnp.sin(ang).astype(dtype)


def apply_rope(x, cos, sin):
    shape = (cos.shape[0],) + (1,) * (x.ndim - 3) + (cos.shape[1],)
    cos = cos.reshape(shape)
    sin = sin.reshape(shape)
    x1, x2 = jnp.split(x, 2, axis=-1)
    return jnp.concatenate([x1 * cos - x2 * sin, x2 * cos + x1 * sin], axis=-1)


def split_columns(p):
    offs = [int(o) for o in np.cumsum(IN_SPLITS)[:-1]]
    return jnp.split(p, offs, axis=-1)


def to_blocks(t):
    b, n = t.shape[0], t.shape[1]
    return jnp.moveaxis(t.reshape((b, n // Q_BLOCK, Q_BLOCK) + t.shape[2:]), 1, 0)


def from_blocks(o):
    nb, b, qb = o.shape[0], o.shape[1], o.shape[2]
    return jnp.moveaxis(o, 0, 1).reshape(b, nb * qb, -1)


def mla_q(p_q, q_norm_g, w_uq):
    b, t = p_q.shape[0], p_q.shape[1]
    q = (rmsnorm(p_q, q_norm_g) @ w_uq).reshape(b, t, MLA_HEADS, MLA_NOPE + MLA_ROPE)
    return q[..., :MLA_NOPE], q[..., MLA_NOPE:]


def mla_kv(p_kv, kv_norm_g, w_ukv):
    b, t = p_kv.shape[0], p_kv.shape[1]
    kv = (rmsnorm(p_kv, kv_norm_g) @ w_ukv).reshape(b, t, MLA_HEADS, MLA_NOPE + MLA_V)
    return kv[..., :MLA_NOPE], kv[..., MLA_NOPE:]


def mla_block(qn, qr, kn, kr, v):
    s = (jnp.einsum('bqhd,bkhd->bhqk', qn, kn) + jnp.einsum('bqhr,bkr->bhqk', qr, kr)).astype(jnp.float32) * MLA_SCALE
    p = jax.nn.softmax(s, axis=-1).astype(v.dtype)
    return jnp.einsum('bhqk,bkhd->bqhd', p, v)


def mla_latent(qn, qr, kn, kr, v):
    out = lax.map(lambda a: mla_block(a[0], a[1], kn, kr, v), (to_blocks(qn), to_blocks(qr)))
    return from_blocks(out)


def sink_softmax(scores, sink):
    s_sink = jnp.broadcast_to(sink.astype(jnp.float32)[None, :, :, None, None], scores.shape[:-1] + (1,))
    p = jax.nn.softmax(jnp.concatenate([scores, s_sink], axis=-1), axis=-1)
    return p[..., :-1]


def swa_latent(q, k, v, k_ctx, v_ctx, sink):
    s_len = q.shape[1]
    n_ctx = k_ctx.shape[1]
    nb = s_len // Q_BLOCK
    band = Q_BLOCK + 2 * SWA_WINDOW
    pad = ((0, 0), (SWA_WINDOW, SWA_WINDOW), (0, 0), (0, 0))
    kp = jnp.pad(k, pad)
    vp = jnp.pad(v, pad)
    q_off = jnp.arange(Q_BLOCK)
    k_off = jnp.arange(band)

    def block(args):
        i, q_b = args
        start = i * Q_BLOCK
        k_b = lax.dynamic_slice_in_dim(kp, start, band, axis=1)
        v_b = lax.dynamic_slice_in_dim(vp, start, band, axis=1)
        q_pos = start + q_off
        k_pos = start - SWA_WINDOW + k_off
        valid = (jnp.abs(q_pos[:, None] - k_pos[None, :]) <= SWA_WINDOW) & ((k_pos >= 0) & (k_pos < s_len))[None, :]
        s_loc = jnp.einsum('bqgrd,bkgd->bgrqk', q_b, k_b).astype(jnp.float32) * SWA_SCALE
        s_loc = jnp.where(valid, s_loc, -jnp.inf)
        s_ctx = jnp.einsum('bqgrd,bcgd->bgrqc', q_b, k_ctx).astype(jnp.float32) * SWA_SCALE
        p = sink_softmax(jnp.concatenate([s_ctx, s_loc], axis=-1), sink).astype(v.dtype)
        return (jnp.einsum('bgrqc,bcgd->bqgrd', p[..., :n_ctx], v_ctx)
                + jnp.einsum('bgrqk,bkgd->bqgrd', p[..., n_ctx:], v_b))

    out = lax.map(block, (jnp.arange(nb), to_blocks(q)))
    return from_blocks(out)


def swa_context(q, k, v, sink):
    s = jnp.einsum('bqgrd,bcgd->bgrqc', q, k).astype(jnp.float32) * SWA_SCALE
    p = sink_softmax(s, sink).astype(v.dtype)
    o = jnp.einsum('bgrqc,bcgd->bqgrd', p, v)
    return o.reshape(o.shape[0], o.shape[1], SWA_WIDTH)


def short_conv(b_gate, c_gate, u, w):
    z = c_gate * u
    y = lax.conv_general_dilated(z, w[:, None, :].astype(z.dtype), window_strides=(1,),
                                 padding=((CONV_K // 2, CONV_K // 2),),
                                 dimension_numbers=('NWC', 'WIO', 'NWC'),
                                 feature_group_count=CONV_CH)
    return b_gate * y


def ec_ffn(h, router_w, w_gate, w_up, w_down):
    n_tok, d = h.shape[1], h.shape[2]
    cap = EC_CAPACITY_FACTOR * n_tok // N_EXPERTS
    aff = jax.nn.softmax((h @ router_w).astype(jnp.float32), axis=-1)
    top_aff, idx = lax.top_k(jnp.swapaxes(aff, 1, 2), cap)
    xg = jax.vmap(lambda hb, ib: hb[ib])(h, idx)
    hid = jax.nn.silu(jnp.einsum('becd,edf->becf', xg, w_gate)) * jnp.einsum('becd,edf->becf', xg, w_up)
    y = jnp.einsum('becf,efd->becd', hid, w_down) * top_aff[..., None].astype(h.dtype)
    return jax.vmap(lambda yb, ib: jnp.zeros((n_tok, d), yb.dtype).at[ib.reshape(-1)].add(yb.reshape(-1, d)))(y, idx)


def hybrid_layer(xl, xc, c, c_ctx, ada_w, ada_b, norm1_g, w_in, q_norm_g, w_uq, kv_norm_g, w_ukv,
                 sink, conv_w, w_o, norm2_g, router_w, w_gate, w_up, w_down, rope_mla, rope_swa, update_ctx):
    b, s_len = xl.shape[0], xl.shape[1]
    n_ctx = xc.shape[1]
    sh1, sc1, g1, sh2, sc2, g2 = jnp.split((jax.nn.silu(c) @ ada_w + ada_b)[:, None, :], N_MOD, axis=-1)
    csh1, csc1, cg1, csh2, csc2, cg2 = jnp.split(jax.nn.silu(c_ctx) @ ada_w + ada_b, N_MOD, axis=-1)

    hl = rmsnorm(xl, norm1_g) * (1 + sc1) + sh1
    hc = rmsnorm(xc, norm1_g) * (1 + csc1) + csh1
    l_cq, l_ckv, l_kr, l_q, l_k, l_v, l_b, l_c, l_u = split_columns(hl @ w_in)
    c_cq, c_ckv, c_kr, c_q, c_k, c_v, c_b, c_c, c_u = split_columns(hc @ w_in)

    qn_l, qr_l = mla_q(l_cq, q_norm_g, w_uq)
    qr_l = apply_rope(qr_l, *rope_mla)
    kn_l, va_l = mla_kv(l_ckv, kv_norm_g, w_ukv)
    kr_l = apply_rope(l_kr, *rope_mla)
    kn_c, va_c = mla_kv(c_ckv, kv_norm_g, w_ukv)
    a_l = mla_latent(qn_l, qr_l, jnp.concatenate([kn_c, kn_l], axis=1),
                     jnp.concatenate([c_kr, kr_l], axis=1), jnp.concatenate([va_c, va_l], axis=1))

    sink_gr = sink.reshape(SWA_KV_HEADS, SWA_REP)
    q_l = apply_rope(l_q.reshape(b, s_len, SWA_KV_HEADS, SWA_REP, SWA_HEAD_DIM), *rope_swa)
    k_l = apply_rope(l_k.reshape(b, s_len, SWA_KV_HEADS, SWA_HEAD_DIM), *rope_swa)
    v_l = l_v.reshape(b, s_len, SWA_KV_HEADS, SWA_HEAD_DIM)
    k_c = c_k.reshape(b, n_ctx, SWA_KV_HEADS, SWA_HEAD_DIM)
    v_c = c_v.reshape(b, n_ctx, SWA_KV_HEADS, SWA_HEAD_DIM)
    b_l = swa_latent(q_l, k_l, v_l, k_c, v_c, sink_gr)

    cv_l = short_conv(l_b, l_c, l_u, conv_w)

    xl = xl + g1 * (jnp.concatenate([a_l, b_l, cv_l], axis=-1) @ w_o)
    xl = xl + g2 * ec_ffn(rmsnorm(xl, norm2_g) * (1 + sc2) + sh2, router_w, w_gate, w_up, w_down)

    if update_ctx:
        qn_c, qr_c = mla_q(c_cq, q_norm_g, w_uq)
        a_c = mla_block(qn_c, qr_c, kn_c, c_kr, va_c).reshape(b, n_ctx, MLA_WIDTH)
        b_c = swa_context(c_q.reshape(b, n_ctx, SWA_KV_HEADS, SWA_REP, SWA_HEAD_DIM), k_c, v_c, sink_gr)
        cv_c = short_conv(c_b, c_c, c_u, conv_w)
        xc = xc + cg1 * (jnp.concatenate([a_c, b_c, cv_c], axis=-1) @ w_o)
        xc = xc + cg2 * ec_ffn(rmsnorm(xc, norm2_g) * (1 + csc2) + csh2, router_w, w_gate, w_up, w_down)
    return xl, xc


def setup_inputs(seed: int = 0) -> dict:
    key = jax.random.key(seed)
    ks = jax.random.split(key, 21)
    f32 = jnp.float32
    D = D_MODEL

    def nrm(k, shape, scale):
        return jax.random.normal(k, shape, f32) * scale

    def gain(k, shape):
        return 1.0 + 0.02 * jax.random.normal(k, shape, f32)

    return {
        'x': nrm(ks[0], (BATCH, SEQ, D), 1.0),
        'c': nrm(ks[1], (BATCH, D), 1.0),
        'ctx': nrm(ks[2], (BATCH, CTX_LEN, D), 1.0),
        'c_ctx': nrm(ks[3], (D,), 1.0),
        'ada_w': nrm(ks[4], (DEPTH, D, N_MOD * D), 0.5 * D ** -0.5),
        'ada_b': nrm(ks[5], (DEPTH, N_MOD * D), 0.02),
        'norm1_g': gain(ks[6], (DEPTH, D)),
        'w_in': nrm(ks[7], (DEPTH, D, IN_WIDTH), D ** -0.5),
        'mla_q_norm_g': gain(ks[8], (DEPTH, MLA_Q_RANK)),
        'mla_w_uq': nrm(ks[9], (DEPTH, MLA_Q_RANK, MLA_HEADS * (MLA_NOPE + MLA_ROPE)), MLA_Q_RANK ** -0.5),
        'mla_kv_norm_g': gain(ks[10], (DEPTH, MLA_KV_RANK)),
        'mla_w_ukv': nrm(ks[11], (DEPTH, MLA_KV_RANK, MLA_HEADS * (MLA_NOPE + MLA_V)), MLA_KV_RANK ** -0.5),
        'swa_sink': nrm(ks[12], (DEPTH, SWA_HEADS), 0.5),
        'conv_w': nrm(ks[13], (DEPTH, CONV_K, CONV_CH), CONV_K ** -0.5),
        'w_o': nrm(ks[14], (DEPTH, MIX_WIDTH, D), MIX_WIDTH ** -0.5),
        'norm2_g': gain(ks[15], (DEPTH, D)),
        'router_w': nrm(ks[16], (DEPTH, D, N_EXPERTS), D ** -0.5),
        'exp_w_gate': nrm(ks[17], (DEPTH, N_EXPERTS, D, EXPERT_FF), D ** -0.5),
        'exp_w_up': nrm(ks[18], (DEPTH, N_EXPERTS, D, EXPERT_FF), D ** -0.5),
        'exp_w_down': nrm(ks[19], (DEPTH, N_EXPERTS, EXPERT_FF, D), EXPERT_FF ** -0.5),
        'final_norm_g': gain(ks[20], (D,)),
    }


def reference(x, c, ctx, c_ctx, ada_w, ada_b, norm1_g, w_in, mla_q_norm_g, mla_w_uq, mla_kv_norm_g,
              mla_w_ukv, swa_sink, conv_w, w_o, norm2_g, router_w, exp_w_gate, exp_w_up, exp_w_down,
              final_norm_g):
    n_lat = x.shape[1]
    rope_mla = axial_rope(n_lat, MLA_ROPE, x.dtype)
    rope_swa = axial_rope(n_lat, SWA_HEAD_DIM, x.dtype)
    xl, xc = x, ctx
    for li in range(DEPTH):
        xl, xc = hybrid_layer(xl, xc, c, c_ctx, ada_w[li], ada_b[li], norm1_g[li], w_in[li],
                              mla_q_norm_g[li], mla_w_uq[li], mla_kv_norm_g[li], mla_w_ukv[li],
                              swa_sink[li], conv_w[li], w_o[li], norm2_g[li], router_w[li],
                              exp_w_gate[li], exp_w_up[li], exp_w_down[li], rope_mla, rope_swa,
                              li < DEPTH - 1)
    return rmsnorm(xl, final_norm_g)
```

```python
import functools

import jax
import jax.numpy as jnp
import numpy as np
from jax import lax
from jax.experimental import pallas as pl
from jax.experimental.pallas import tpu as pltpu

D_MODEL = 1024
GRID_W = 64
NORM_EPS = 1e-6
ROPE_BASE = 10000.0

MLA_HEADS = 6
MLA_Q_RANK = 256
MLA_KV_RANK = 128
MLA_NOPE = 64
MLA_ROPE = 32
MLA_V = 64
MLA_SCALE = (MLA_NOPE + MLA_ROPE) ** -0.5

SWA_HEADS = 6
SWA_KV_HEADS = 2
SWA_REP = SWA_HEADS // SWA_KV_HEADS
SWA_HEAD_DIM = 64
SWA_WINDOW = 128
SWA_SCALE = SWA_HEAD_DIM ** -0.5

CONV_CH = 256
N_EXPERTS = 16
EXPERT_FF = 512
EC_CAPACITY_FACTOR = 2
N_MOD = 6

LANES = 128
SUBLANES = 8
TILE = 256
HEAD_PAD = 128
MOD_ROWS = 40
NEG_BIG = -1e30

_C_CQ = 0
_C_CKV = 256
_C_KR = 384
_C_SQ = 512
_C_SQSW = 896
_C_SK = 1280
_C_SKSW = 1408
_C_SV = 1536
_C_CB = 1664
_C_CC = 1920
_C_CU = 2176
IN_W = 2432

_T_CQ = 0
_T_SQ = 768
_T_KR = 1536
_T_CS = 1664
_T_SS = 2048
_T_CK = 2432
_T_SK = 2560
TAB_W = 2688

_VMEM_LIMIT = 56 * 1024 * 1024


def _cparams(n_grid):
    return pltpu.CompilerParams(dimension_semantics=("arbitrary",) * n_grid,
                                vmem_limit_bytes=_VMEM_LIMIT)


def _silu(v):
    return v * (1.0 / (1.0 + jnp.exp(-v)))


def _rms(v, g):
    return v * lax.rsqrt(jnp.mean(v * v, axis=-1, keepdims=True) + NORM_EPS) * g


def _dot(a, b):
    return jnp.dot(a, b, preferred_element_type=jnp.float32)


def _dot_nt(a, b):
    return lax.dot_general(a, b, (((1,), (1,)), ((), ())), preferred_element_type=jnp.float32)


def _mod_body(c_ref, w_ref, b_ref, o_ref):
    a = _silu(c_ref[...]).astype(jnp.bfloat16)
    o_ref[...] = _dot(a, w_ref[...].astype(jnp.bfloat16)) + b_ref[...]


def _modulation(cc, ada_w, ada_b):
    depth, d, n = ada_w.shape
    bn = 512
    return pl.pallas_call(
        _mod_body,
        grid=(depth, n // bn),
        in_specs=[
            pl.BlockSpec((MOD_ROWS, d), lambda l, i: (0, 0)),
            pl.BlockSpec((None, d, bn), lambda l, i: (l, 0, i)),
            pl.BlockSpec((None, 1, bn), lambda l, i: (l, 0, i)),
        ],
        out_specs=pl.BlockSpec((None, MOD_ROWS, bn), lambda l, i: (l, 0, i)),
        out_shape=jax.ShapeDtypeStruct((depth, MOD_ROWS, n), jnp.float32),
        compiler_params=_cparams(2),
        name="modulation",
    )(cc, ada_w, ada_b.reshape(depth, 1, n))


def _proj_body(x_ref, ctx_ref, mod_ref, g1_ref, win_ref, qg_ref, wuq_ref, kvg_ref, wk_ref, wv_ref,
               e_ref, tab_ref, qm_ref, km_ref, vm_ref, qs_ref, ks_ref, vs_ref, cz_ref, xs_ref):
    j = pl.program_id(0)

    @pl.when(j == 0)
    def _():
        xs_ref[...] = ctx_ref[...]

    @pl.when(j > 0)
    def _():
        xs_ref[...] = x_ref[...]

    d = D_MODEL
    sh1 = mod_ref[:, 0:d]
    sc1 = mod_ref[:, d:2 * d]
    h = (_rms(xs_ref[...], g1_ref[...]) * (1.0 + sc1) + sh1).astype(jnp.bfloat16)

    def proj(lo, hi):
        return _dot(h, win_ref[:, lo:hi])

    def tab(lo, n):
        return tab_ref[:, lo:lo + n]

    cq = _rms(proj(_C_CQ, _C_CKV), qg_ref[...]).astype(jnp.bfloat16)
    uq = _dot(cq, wuq_ref[...])
    nq = MLA_HEADS * HEAD_PAD
    qm_ref[...] = (uq[:, :nq] * tab(_T_CQ, nq) + uq[:, nq:] * tab(_T_SQ, nq)).astype(jnp.bfloat16)

    ckv = _rms(proj(_C_CKV, _C_KR), kvg_ref[...]).astype(jnp.bfloat16)
    krp = (proj(_C_KR, _C_SQ) * tab(_T_KR, LANES)).astype(jnp.bfloat16)
    km_ref[...] = (_dot(ckv, wk_ref[...]) + _dot(krp, e_ref[...])).astype(jnp.bfloat16)
    vm_ref[...] = _dot(ckv, wv_ref[...]).astype(jnp.bfloat16)

    nsq = SWA_HEADS * SWA_HEAD_DIM
    qs_ref[...] = (proj(_C_SQ, _C_SQSW) * tab(_T_CS, nsq)
                   + proj(_C_SQSW, _C_SK) * tab(_T_SS, nsq)).astype(jnp.bfloat16)
    ks_ref[...] = (proj(_C_SK, _C_SKSW) * tab(_T_CK, LANES)
                   + proj(_C_SKSW, _C_SV) * tab(_T_SK, LANES)).astype(jnp.bfloat16)
    vs_ref[...] = proj(_C_SV, _C_CB).astype(jnp.bfloat16)

    cz_ref[:, 0:CONV_CH] = proj(_C_CB, _C_CC).astype(jnp.bfloat16)
    cz_ref[:, CONV_CH:2 * CONV_CH] = (proj(_C_CC, _C_CU) * proj(_C_CU, IN_W)).astype(jnp.bfloat16)


def _projections(x, ctx, mod_l, lw, tab):
    b, s, d = x.shape
    n_ctx = ctx.shape[1]
    t = n_ctx + s
    nt = t // TILE
    const = lambda j, i: (0, 0)
    outs = [(MLA_HEADS * HEAD_PAD, "qm"), (MLA_HEADS * HEAD_PAD, "km"), (MLA_HEADS * MLA_V, "vm"),
            (SWA_HEADS * SWA_HEAD_DIM, "qs"), (LANES, "ks"), (LANES, "vs"), (2 * CONV_CH, "cz")]
    return pl.pallas_call(
        _proj_body,
        grid=(nt, b),
        in_specs=[
            pl.BlockSpec((None, TILE, d), lambda j, i: (jnp.where(j == 0, 0, i), jnp.maximum(j - 1, 0), 0)),
            pl.BlockSpec((None, TILE, d), lambda j, i: (jnp.where(j == 0, i, 0), 0, 0)),
            pl.BlockSpec((None, 1, N_MOD * d), lambda j, i: (jnp.where(j == 0, b, i), 0, 0)),
            pl.BlockSpec((1, d), const),
            pl.BlockSpec((d, IN_W), const),
            pl.BlockSpec((1, MLA_Q_RANK), const),
            pl.BlockSpec((MLA_Q_RANK, 2 * MLA_HEADS * HEAD_PAD), const),
            pl.BlockSpec((1, MLA_KV_RANK), const),
            pl.BlockSpec((MLA_KV_RANK, MLA_HEADS * HEAD_PAD), const),
            pl.BlockSpec((MLA_KV_RANK, MLA_HEADS * MLA_V), const),
            pl.BlockSpec((LANES, MLA_HEADS * HEAD_PAD), const),
            pl.BlockSpec((TILE, TAB_W), lambda j, i: (j, 0)),
        ],
        out_specs=[pl.BlockSpec((None, TILE, w), lambda j, i: (i, j, 0)) for w, _ in outs],
        out_shape=[jax.ShapeDtypeStruct((b, t, w), jnp.bfloat16) for w, _ in outs],
        scratch_shapes=[pltpu.VMEM((TILE, d), jnp.float32)],
        compiler_params=_cparams(2),
        name="projections",
    )(x, ctx, mod_l, lw["g1"], lw["win"], lw["qg"], lw["wuq"], lw["kvg"], lw["wk"], lw["wv"], lw["e"], tab)


def _mla_body(q_ref, k_ref, v_ref, o_ref, *, off, n_ctx):
    t = pl.program_id(1) + off
    lane = lax.broadcasted_iota(jnp.int32, (TILE, LANES), 1)

    def attend(nk):
        for c in range(MLA_HEADS // 2):
            res = []
            for hh in range(2):
                h = 2 * c + hh
                q = q_ref[:, h * HEAD_PAD:(h + 1) * HEAD_PAD]
                k = k_ref[0:nk, h * HEAD_PAD:(h + 1) * HEAD_PAD]
                s = _dot_nt(q, k)
                m = jnp.max(s, axis=-1, keepdims=True)
                p = jnp.exp(s - m)
                l = jnp.sum(p, axis=-1, keepdims=True)
                o = _dot(p.astype(jnp.bfloat16), v_ref[0:nk, c * LANES:(c + 1) * LANES])
                res.append(o / l)
            o_ref[:, c * LANES:(c + 1) * LANES] = jnp.where(lane < MLA_V, res[0], res[1]).astype(jnp.bfloat16)

    if off == 0:
        @pl.when(t == 0)
        def _():
            attend(n_ctx)

    @pl.when(t > 0)
    def _():
        attend(k_ref.shape[0])


def _mla_attention(qm, km, vm, n_ctx, with_ctx):
    b, t, _ = qm.shape
    off = 0 if with_ctx else 1
    nt = t // TILE - off
    return pl.pallas_call(
        functools.partial(_mla_body, off=off, n_ctx=n_ctx),
        grid=(b, nt),
        in_specs=[
            pl.BlockSpec((None, TILE, qm.shape[2]), lambda i, j: (i, j + off, 0)),
            pl.BlockSpec((None, t, km.shape[2]), lambda i, j: (i, 0, 0)),
            pl.BlockSpec((None, t, vm.shape[2]), lambda i, j: (i, 0, 0)),
        ],
        out_specs=pl.BlockSpec((None, TILE, vm.shape[2]), lambda i, j: (i, j + off, 0)),
        out_shape=jax.ShapeDtypeStruct((b, t, vm.shape[2]), jnp.bfloat16),
        compiler_params=_cparams(2),
        name="mla_attention",
    )(qm, km, vm)


def _swa_body(sink_ref, q_ref, k_ref, v_ref, o_ref, *, off, n_ctx, n_lat):
    t = pl.program_id(1) + off
    band = TILE + 2 * SWA_WINDOW
    lane = lax.broadcasted_iota(jnp.int32, (TILE, LANES), 1)
    lo_half = lane < SWA_HEAD_DIM

    def attend(kcat, vcat, valid):
        for c in range(SWA_REP):
            q2 = q_ref[:, c * LANES:(c + 1) * LANES]
            res = []
            for g in range(SWA_KV_HEADS):
                keep = lo_half if g == 0 else jnp.logical_not(lo_half)
                qg = jnp.where(keep, q2, jnp.zeros_like(q2))
                s = _dot_nt(qg, kcat)
                if valid is not None:
                    s = jnp.where(valid, s, NEG_BIG)
                sk = sink_ref[g * SWA_REP + c]
                m = jnp.maximum(jnp.max(s, axis=-1, keepdims=True), sk)
                p = jnp.exp(s - m)
                l = jnp.sum(p, axis=-1, keepdims=True) + jnp.exp(sk - m)
                res.append(_dot(p.astype(jnp.bfloat16), vcat) / l)
            o_ref[:, c * LANES:(c + 1) * LANES] = jnp.where(lo_half, res[0], res[1]).astype(jnp.bfloat16)

    if off == 0:
        @pl.when(t == 0)
        def _():
            attend(k_ref[0:n_ctx, :], v_ref[0:n_ctx, :], None)

    @pl.when(t > 0)
    def _():
        start = (t - 1) * TILE
        ks = jnp.clip(start - SWA_WINDOW, 0, n_lat - band)
        row0 = pl.multiple_of(n_ctx + ks, SWA_WINDOW)
        kcat = jnp.concatenate([k_ref[0:n_ctx, :], k_ref[pl.ds(row0, band), :]], axis=0)
        vcat = jnp.concatenate([v_ref[0:n_ctx, :], v_ref[pl.ds(row0, band), :]], axis=0)
        qpos = start + lax.broadcasted_iota(jnp.int32, (TILE, n_ctx + band), 0)
        col = lax.broadcasted_iota(jnp.int32, (TILE, n_ctx + band), 1)
        kpos = ks + col - n_ctx
        valid = (col < n_ctx) | (jnp.abs(qpos - kpos) <= SWA_WINDOW)
        attend(kcat, vcat, valid)


def _swa_attention(sink, qs, ks, vs, n_ctx, with_ctx):
    b, t, _ = qs.shape
    off = 0 if with_ctx else 1
    nt = t // TILE - off
    return pl.pallas_call(
        functools.partial(_swa_body, off=off, n_ctx=n_ctx, n_lat=t - n_ctx),
        grid=(b, nt),
        in_specs=[
            pl.BlockSpec(memory_space=pltpu.SMEM),
            pl.BlockSpec((None, TILE, qs.shape[2]), lambda i, j: (i, j + off, 0)),
            pl.BlockSpec((None, t, LANES), lambda i, j: (i, 0, 0)),
            pl.BlockSpec((None, t, LANES), lambda i, j: (i, 0, 0)),
        ],
        out_specs=pl.BlockSpec((None, TILE, qs.shape[2]), lambda i, j: (i, j + off, 0)),
        out_shape=jax.ShapeDtypeStruct((b, t, qs.shape[2]), jnp.bfloat16),
        compiler_params=_cparams(2),
        name="swa_attention",
    )(sink, qs, ks, vs)


def _mix_body(*refs, off, with_ctx, nt_all):
    if with_ctx:
        (a_ref, b_ref, cz_ref, hp_ref, hn_ref, x_ref, ctx_ref, mod_ref, modc_ref, cw_ref, wo_ref, n2_ref,
         rw_ref, x1_ref, aff_ref, xc1_ref, affc_ref) = refs
    else:
        (a_ref, b_ref, cz_ref, hp_ref, hn_ref, x_ref, mod_ref, cw_ref, wo_ref, n2_ref,
         rw_ref, x1_ref, aff_ref) = refs
    t = pl.program_id(1) + off
    d = D_MODEL
    halo = hp_ref.shape[0]

    def run(xin_ref, m_ref, xo_ref, ao_ref, first, last):
        z = cz_ref[:, CONV_CH:2 * CONV_CH].astype(jnp.float32)
        zp = hp_ref[halo - 1:halo, CONV_CH:2 * CONV_CH].astype(jnp.float32)
        zn = hn_ref[0:1, CONV_CH:2 * CONV_CH].astype(jnp.float32)
        zp = jnp.where(first, jnp.zeros_like(zp), zp)
        zn = jnp.where(last, jnp.zeros_like(zn), zn)
        row = lax.broadcasted_iota(jnp.int32, z.shape, 0)
        z_dn = jnp.where(row == 0, zp, pltpu.roll(z, 1, 0))
        z_up = jnp.where(row == TILE - 1, zn, pltpu.roll(z, TILE - 1, 0))
        y = z_dn * cw_ref[0:1, :] + z * cw_ref[1:2, :] + z_up * cw_ref[2:3, :]
        cv = (cz_ref[:, 0:CONV_CH].astype(jnp.float32) * y).astype(jnp.bfloat16)
        mix = jnp.concatenate([a_ref[...], b_ref[...], cv], axis=-1)
        g1 = m_ref[:, 2 * d:3 * d]
        sh2 = m_ref[:, 3 * d:4 * d]
        sc2 = m_ref[:, 4 * d:5 * d]
        x1 = xin_ref[...] + g1 * _dot(mix, wo_ref[...])
        xo_ref[...] = x1
        h2 = (_rms(x1, n2_ref[...]) * (1.0 + sc2) + sh2).astype(jnp.bfloat16)
        lg = _dot_nt(rw_ref[...], h2)
        ex = jnp.exp(lg - jnp.max(lg, axis=0, keepdims=True))
        ao_ref[...] = ex / jnp.sum(ex, axis=0, keepdims=True)

    if with_ctx:
        @pl.when(t == 0)
        def _():
            run(ctx_ref, modc_ref, xc1_ref, affc_ref, True, True)

    @pl.when(t > 0)
    def _():
        run(x_ref, mod_ref, x1_ref, aff_ref, t == 1, t == nt_all - 1)


def _mixer_out(a, bsw, cz, x, ctx, mod_l, lw, with_ctx):
    b, t, _ = a.shape
    s = x.shape[1]
    d = x.shape[2]
    n_ctx = t - s
    off = 0 if with_ctx else 1
    nt_all = t // TILE
    nt = nt_all - off
    halo = 16
    hb = TILE // halo
    lat = lambda i, j: (i, jnp.maximum(j + off - 1, 0), 0)
    const = lambda i, j: (0, 0)
    in_specs = [
        pl.BlockSpec((None, TILE, a.shape[2]), lambda i, j: (i, j + off, 0)),
        pl.BlockSpec((None, TILE, bsw.shape[2]), lambda i, j: (i, j + off, 0)),
        pl.BlockSpec((None, TILE, cz.shape[2]), lambda i, j: (i, j + off, 0)),
        pl.BlockSpec((None, halo, cz.shape[2]), lambda i, j: (i, jnp.maximum((j + off) * hb - 1, 0), 0)),
        pl.BlockSpec((None, halo, cz.shape[2]),
                     lambda i, j: (i, jnp.minimum((j + off + 1) * hb, t // halo - 1), 0)),
        pl.BlockSpec((None, TILE, d), lat),
    ]
    args = [a, bsw, cz, cz, cz, x]
    if with_ctx:
        in_specs.append(pl.BlockSpec((None, n_ctx, d), lambda i, j: (i, 0, 0)))
        args.append(ctx)
    in_specs.append(pl.BlockSpec((None, 1, N_MOD * d), lambda i, j: (i, 0, 0)))
    args.append(mod_l)
    if with_ctx:
        in_specs.append(pl.BlockSpec((None, 1, N_MOD * d), lambda i, j: (b, 0, 0)))
        args.append(mod_l)
    in_specs += [
        pl.BlockSpec((3, CONV_CH), const),
        pl.BlockSpec((d, d), const),
        pl.BlockSpec((1, d), const),
        pl.BlockSpec((N_EXPERTS, d), const),
    ]
    args += [lw["cw"], lw["wo"], lw["n2"], lw["rw"]]
    out_specs = [
        pl.BlockSpec((None, TILE, d), lat),
        pl.BlockSpec((None, N_EXPERTS, TILE), lambda i, j: (i, 0, jnp.maximum(j + off - 1, 0))),
    ]
    out_shape = [jax.ShapeDtypeStruct((b, s, d), jnp.float32),
                 jax.ShapeDtypeStruct((b, N_EXPERTS, s), jnp.float32)]
    if with_ctx:
        out_specs += [pl.BlockSpec((None, n_ctx, d), lambda i, j: (i, 0, 0)),
                      pl.BlockSpec((None, N_EXPERTS, n_ctx), lambda i, j: (i, 0, 0))]
        out_shape += [jax.ShapeDtypeStruct((b, n_ctx, d), jnp.float32),
                      jax.ShapeDtypeStruct((b, N_EXPERTS, n_ctx), jnp.float32)]
    return pl.pallas_call(
        functools.partial(_mix_body, off=off, with_ctx=with_ctx, nt_all=nt_all),
        grid=(b, nt),
        in_specs=in_specs,
        out_specs=out_specs,
        out_shape=out_shape,
        compiler_params=_cparams(2),
        name="mixer_out",
    )(*args)


ROUTE_ROWS = 128
ROUTE_SLOTS = 64


def _route_body(aff_ref, tri_ref, idx_ref, taff_ref, pos_ref, oi_ref, oa_ref, *, cap):
    rows, n = aff_ref.shape
    aff = aff_ref[...]
    bits = pltpu.bitcast(aff, jnp.int32)

    def count(mask):
        return jnp.sum(jnp.where(mask, 1.0, 0.0), axis=1, keepdims=True)

    def ones(mask):
        return jnp.where(mask, 1.0, 0.0).astype(jnp.bfloat16)

    def search(i, thr):
        cand = thr | (1 << (30 - i))
        return jnp.where(count(bits >= cand) >= cap, cand, thr)

    thr = lax.fori_loop(0, 31, search, jnp.zeros((rows, 1), jnp.int32))
    gt = bits > thr
    eq = bits == thr
    need = cap - count(gt)
    peq = _dot(ones(eq), tri_ref[...])
    sel = gt | (eq & (peq < need))
    pos = _dot(ones(sel), tri_ref[...])
    pos_ref[...] = jnp.where(sel, pos, -1.0)

    n_sweep = min(ROUTE_SLOTS, cap)
    tok = lax.broadcasted_iota(jnp.int32, (n_sweep, n), 1).astype(jnp.float32)
    lane = lax.broadcasted_iota(jnp.int32, (cap, rows), 1)
    oi_ref[...] = jnp.zeros_like(oi_ref)
    oa_ref[...] = jnp.zeros_like(oa_ref)

    def per_row(r, carry):
        prow = pos_ref[pl.ds(r, 1), :]
        arow = aff_ref[pl.ds(r, 1), :]
        icols = []
        acols = []
        for j0 in range(0, cap, n_sweep):
            slot = (lax.broadcasted_iota(jnp.int32, (n_sweep, 1), 0) + j0).astype(jnp.float32)
            hit = prow == slot
            icols.append(jnp.sum(jnp.where(hit, tok, 0.0), axis=1, keepdims=True))
            acols.append(jnp.sum(jnp.where(hit, arow, 0.0), axis=1, keepdims=True))
        icol = jnp.concatenate(icols, axis=0) if len(icols) > 1 else icols[0]
        acol = jnp.concatenate(acols, axis=0) if len(acols) > 1 else acols[0]
        oi_ref[...] = jnp.where(lane == r, icol, oi_ref[...])
        oa_ref[...] = jnp.where(lane == r, acol, oa_ref[...])
        return carry

    lax.fori_loop(0, rows, per_row, 0)
    idx_ref[...] = jnp.transpose(oi_ref[...]).astype(jnp.int32)
    taff_ref[...] = jnp.transpose(oa_ref[...])


def _routing(aff, tri, cap):
    b, e, n = aff.shape
    rows = b * e
    rr = min(ROUTE_ROWS, rows)
    idx, taff = pl.pallas_call(
        functools.partial(_route_body, cap=cap),
        grid=(rows // rr,),
        in_specs=[pl.BlockSpec((rr, n), lambda i: (i, 0)),
                  pl.BlockSpec((n, n), lambda i: (0, 0))],
        out_specs=[pl.BlockSpec((rr, cap), lambda i: (i, 0)),
                   pl.BlockSpec((rr, cap), lambda i: (i, 0))],
        out_shape=[jax.ShapeDtypeStruct((rows, cap), jnp.int32),
                   jax.ShapeDtypeStruct((rows, cap), jnp.float32)],
        scratch_shapes=[pltpu.VMEM((rr, n), jnp.float32),
                        pltpu.VMEM((cap, rr), jnp.float32),
                        pltpu.VMEM((cap, rr), jnp.float32)],
        compiler_params=_cparams(1),
        name="routing",
    )(aff.reshape(rows, n), tri)
    return idx.reshape(b, e, cap), taff.reshape(b, e, cap)


GATHER_UNROLL = 8


def _moe_body(idx_ref, taff_ref, x_ref, mod_ref, n2_ref, fg_ref, wg_ref, wu_ref, wd_ref, o_ref,
              h_ref, acc_ref, xg_ref, yb_ref, *, nch, cap, final):
    j = pl.program_id(1)
    d = D_MODEL
    nsl = d // LANES
    rows = TILE * nsl

    @pl.when(j < nch)
    def _():
        sh2 = mod_ref[:, 3 * d:4 * d]
        sc2 = mod_ref[:, 4 * d:5 * d]
        h2 = _rms(x_ref[...], n2_ref[...]) * (1.0 + sc2) + sh2
        base = pl.multiple_of(j * rows, rows)
        for s in range(nsl):
            h_ref[pl.ds(base + s, TILE, stride=nsl), :] = h2[:, s * LANES:(s + 1) * LANES]
        acc_ref[pl.ds(base, rows), :] = jnp.zeros((rows, LANES), jnp.float32)

    @pl.when((j >= nch) & (j < nch + N_EXPERTS))
    def _():
        e = j - nch

        def gather(g, carry):
            for u in range(GATHER_UNROLL):
                jj = g * GATHER_UNROLL + u
                src = pl.multiple_of(idx_ref[e, jj] * nsl, nsl)
                xg_ref[pl.ds(pl.multiple_of(jj * nsl, nsl), nsl), :] = h_ref[pl.ds(src, nsl), :]
            return carry

        lax.fori_loop(0, cap // GATHER_UNROLL, gather, 0)
        xs = jnp.concatenate([xg_ref[pl.ds(s, cap, stride=nsl), :] for s in range(nsl)], axis=-1)
        xs = xs.astype(jnp.bfloat16)
        gate = _dot(xs, wg_ref[...])
        up = _dot(xs, wu_ref[...])
        hid = (_silu(gate) * up).astype(jnp.bfloat16)
        y = _dot(hid, wd_ref[...])
        for s in range(nsl):
            yb_ref[pl.ds(s, cap, stride=nsl), :] = y[:, s * LANES:(s + 1) * LANES]

        def scatter(g, carry):
            vals = []
            for u in range(GATHER_UNROLL):
                jj = g * GATHER_UNROLL + u
                dst = pl.multiple_of(idx_ref[e, jj] * nsl, nsl)
                yrow = yb_ref[pl.ds(pl.multiple_of(jj * nsl, nsl), nsl), :]
                vals.append((dst, acc_ref[pl.ds(dst, nsl), :] + taff_ref[e, jj] * yrow))
            for dst, v in vals:
                acc_ref[pl.ds(dst, nsl), :] = v
            return carry

        lax.fori_loop(0, cap // GATHER_UNROLL, scatter, 0)

    @pl.when(j >= nch + N_EXPERTS)
    def _():
        c = j - nch - N_EXPERTS
        base = pl.multiple_of(c * rows, rows)
        m = jnp.concatenate([acc_ref[pl.ds(base + s, TILE, stride=nsl), :] for s in range(nsl)], axis=-1)
        g2 = mod_ref[:, 5 * d:6 * d]
        x2 = x_ref[...] + g2 * m
        if final:
            x2 = _rms(x2, fg_ref[...])
        o_ref[...] = x2


def _experts(x1, idx, taff, mod_l, mod_row0, lw, final_g, final):
    b, n_tok, d = x1.shape
    cap = idx.shape[2]
    nch = n_tok // TILE
    steps = 2 * nch + N_EXPERTS
    nsl = d // LANES

    def chunk_in(i, j):
        c = jnp.where(j < nch, j, jnp.where(j < nch + N_EXPERTS, nch - 1, j - nch - N_EXPERTS))
        return (i, c, 0)

    def chunk_out(i, j):
        return (i, jnp.clip(j - nch - N_EXPERTS, 0, nch - 1), 0)

    def expert(i, j):
        return (jnp.clip(j - nch, 0, N_EXPERTS - 1), 0, 0)

    const = lambda i, j: (0, 0)
    if mod_row0 is None:
        mod_map = lambda i, j: (i, 0, 0)
    else:
        mod_map = lambda i, j: (mod_row0, 0, 0)
    return pl.pallas_call(
        functools.partial(_moe_body, nch=nch, cap=cap, final=final),
        grid=(b, steps),
        in_specs=[
            pl.BlockSpec((None, N_EXPERTS, cap), lambda i, j: (i, 0, 0), memory_space=pltpu.SMEM),
            pl.BlockSpec((None, N_EXPERTS, cap), lambda i, j: (i, 0, 0), memory_space=pltpu.SMEM),
            pl.BlockSpec((None, TILE, d), chunk_in),
            pl.BlockSpec((None, 1, N_MOD * d), mod_map),
            pl.BlockSpec((1, d), const),
            pl.BlockSpec((1, d), const),
            pl.BlockSpec((None, d, EXPERT_FF), expert),
            pl.BlockSpec((None, d, EXPERT_FF), expert),
            pl.BlockSpec((None, EXPERT_FF, d), expert),
        ],
        out_specs=pl.BlockSpec((None, TILE, d), chunk_out),
        out_shape=jax.ShapeDtypeStruct((b, n_tok, d), jnp.float32),
        scratch_shapes=[pltpu.VMEM((n_tok * nsl, LANES), jnp.float32),
                        pltpu.VMEM((n_tok * nsl, LANES), jnp.float32),
                        pltpu.VMEM((cap * nsl, LANES), jnp.float32),
                        pltpu.VMEM((cap * nsl, LANES), jnp.float32)],
        compiler_params=_cparams(2),
        name="experts",
    )(idx, taff, x1, mod_l, lw["n2"], final_g, lw["wg"], lw["wu"], lw["wd"])


def _swap_halves(w):
    half = w.shape[-1] // 2
    return jnp.concatenate([-w[..., half:], w[..., :half]], axis=-1)


def _swa_perm():
    cols = []
    for c in range(SWA_REP):
        for g in range(SWA_KV_HEADS):
            h = g * SWA_REP + c
            cols.extend(range(h * SWA_HEAD_DIM, (h + 1) * SWA_HEAD_DIM))
    return np.asarray(cols, np.int32)


def _layer_weights(w_in, q_g, w_uq, kv_g, w_ukv, conv_w, w_o, n1, n2, router_w, wg, wu, wd):
    bf = jnp.bfloat16
    d = w_in.shape[0]
    offs = np.cumsum([0, MLA_Q_RANK, MLA_KV_RANK, MLA_ROPE, SWA_HEADS * SWA_HEAD_DIM,
                      SWA_KV_HEADS * SWA_HEAD_DIM, SWA_KV_HEADS * SWA_HEAD_DIM, CONV_CH, CONV_CH, CONV_CH])
    part = [w_in[:, offs[i]:offs[i + 1]] for i in range(9)]
    w_cq, w_ckv, w_kr, w_sq, w_sk, w_sv, w_cb, w_cc, w_cu = part
    perm = _swa_perm()

    def per_head_swap(w, nh):
        wh = w.reshape(d, nh, -1)
        return _swap_halves(wh).reshape(d, -1)

    kr_block = jnp.concatenate([w_kr, _swap_halves(w_kr), jnp.zeros((d, LANES - 2 * MLA_ROPE), w_in.dtype)], axis=1)
    win = jnp.concatenate([
        w_cq, w_ckv, kr_block,
        w_sq[:, perm], per_head_swap(w_sq, SWA_HEADS)[:, perm],
        w_sk, per_head_swap(w_sk, SWA_KV_HEADS), w_sv, w_cb, w_cc, w_cu], axis=1).astype(bf)

    qd = MLA_NOPE + MLA_ROPE
    uq = w_uq.reshape(MLA_Q_RANK, MLA_HEADS, qd)
    zpad = jnp.zeros((MLA_Q_RANK, MLA_HEADS, HEAD_PAD - qd), w_uq.dtype)
    main = jnp.concatenate([uq, zpad], axis=-1)
    swp = jnp.concatenate([jnp.zeros_like(uq[..., :MLA_NOPE]), _swap_halves(uq[..., MLA_NOPE:]), zpad], axis=-1)
    wuq = jnp.concatenate([main.reshape(MLA_Q_RANK, -1), swp.reshape(MLA_Q_RANK, -1)], axis=1).astype(bf)

    ukv = w_ukv.reshape(MLA_KV_RANK, MLA_HEADS, MLA_NOPE + MLA_V)
    wk = jnp.concatenate([ukv[..., :MLA_NOPE],
                          jnp.zeros((MLA_KV_RANK, MLA_HEADS, HEAD_PAD - MLA_NOPE), w_ukv.dtype)], axis=-1)
    wk = wk.reshape(MLA_KV_RANK, -1).astype(bf)
    wv = ukv[..., MLA_NOPE:].reshape(MLA_KV_RANK, -1).astype(bf)

    e = np.zeros((LANES, MLA_HEADS * HEAD_PAD), np.float32)
    for h in range(MLA_HEADS):
        for l in range(MLA_ROPE):
            e[l, h * HEAD_PAD + MLA_NOPE + l] = 1.0
            e[MLA_ROPE + l, h * HEAD_PAD + MLA_NOPE + l] = 1.0

    n_mla = MLA_HEADS * MLA_V
    n_swa = SWA_HEADS * SWA_HEAD_DIM
    wo = jnp.concatenate([w_o[:n_mla], w_o[n_mla:n_mla + n_swa][perm], w_o[n_mla + n_swa:]], axis=0).astype(bf)
    return dict(
        g1=n1.reshape(1, -1), win=win, qg=q_g.reshape(1, -1), wuq=wuq, kvg=kv_g.reshape(1, -1),
        wk=wk, wv=wv, e=jnp.asarray(e, bf), cw=conv_w, wo=wo, n2=n2.reshape(1, -1),
        rw=router_w.T.astype(bf), wg=wg.astype(bf), wu=wu.astype(bf), wd=wd.astype(bf))


def _axial_tables(n_tokens, rot_dim):
    rows = n_tokens // GRID_W
    row = jnp.repeat(jnp.arange(rows, dtype=jnp.float32), GRID_W)
    col = jnp.tile(jnp.arange(GRID_W, dtype=jnp.float32), rows)
    n_freq = rot_dim // 4
    inv = ROPE_BASE ** (-jnp.arange(n_freq, dtype=jnp.float32) / n_freq)
    ang = jnp.concatenate([row[:, None] * inv, col[:, None] * inv], axis=-1)
    return jnp.cos(ang), jnp.sin(ang)


def _row_table(n_ctx, n_lat):
    def with_ctx(cos, sin):
        one = jnp.ones((n_ctx, cos.shape[1]), jnp.float32)
        return (jnp.concatenate([one, cos], axis=0), jnp.concatenate([0.0 * one, sin], axis=0))

    cm, sm = with_ctx(*_axial_tables(n_lat, MLA_ROPE))
    cs, ss = with_ctx(*_axial_tables(n_lat, SWA_HEAD_DIM))
    t = n_ctx + n_lat
    cm2 = jnp.concatenate([cm, cm], axis=1)
    sm2 = jnp.concatenate([sm, sm], axis=1)
    cs2 = jnp.concatenate([cs, cs], axis=1)
    ss2 = jnp.concatenate([ss, ss], axis=1)
    pad = jnp.zeros((t, HEAD_PAD - MLA_NOPE - MLA_ROPE), jnp.float32)
    cq = jnp.concatenate([jnp.ones((t, MLA_NOPE), jnp.float32), cm2, pad], axis=1) * MLA_SCALE
    sq = jnp.concatenate([jnp.zeros((t, MLA_NOPE), jnp.float32), sm2, pad], axis=1) * MLA_SCALE
    kr = jnp.concatenate([cm2, sm2, jnp.zeros((t, LANES - 2 * MLA_ROPE), jnp.float32)], axis=1)
    tab = jnp.concatenate([
        jnp.tile(cq, (1, MLA_HEADS)), jnp.tile(sq, (1, MLA_HEADS)), kr,
        jnp.tile(cs2, (1, SWA_HEADS)) * SWA_SCALE, jnp.tile(ss2, (1, SWA_HEADS)) * SWA_SCALE,
        jnp.tile(cs2, (1, SWA_KV_HEADS)), jnp.tile(ss2, (1, SWA_KV_HEADS))], axis=1)
    return tab


def _strict_upper(n):
    r = lax.broadcasted_iota(jnp.int32, (n, n), 0)
    c = lax.broadcasted_iota(jnp.int32, (n, n), 1)
    return (r < c).astype(jnp.bfloat16)


def kernel(x, c, ctx, c_ctx, ada_w, ada_b, norm1_g, w_in, mla_q_norm_g, mla_w_uq, mla_kv_norm_g, mla_w_ukv,
           swa_sink, conv_w, w_o, norm2_g, router_w, exp_w_gate, exp_w_up, exp_w_down, final_norm_g):
    b, s, d = x.shape
    n_ctx = ctx.shape[1]
    depth = ada_w.shape[0]
    assert d == D_MODEL and s % TILE == 0 and n_ctx == TILE and b + 1 <= MOD_ROWS
    cap_lat = EC_CAPACITY_FACTOR * s // N_EXPERTS
    cap_ctx = EC_CAPACITY_FACTOR * n_ctx // N_EXPERTS

    cc = jnp.concatenate([c, c_ctx[None, :], jnp.zeros((MOD_ROWS - b - 1, d), c.dtype)], axis=0)
    mod = _modulation(cc, ada_w, ada_b).reshape(depth, MOD_ROWS, 1, N_MOD * d)
    tab = _row_table(n_ctx, s)
    tri_lat = _strict_upper(s)
    tri_ctx = _strict_upper(n_ctx)
    sink_slots = jnp.zeros((8,), jnp.float32)
    fg = final_norm_g.reshape(1, d)

    xl, xc = x, ctx
    for li in range(depth):
        last = li == depth - 1
        lw = _layer_weights(w_in[li], mla_q_norm_g[li], mla_w_uq[li], mla_kv_norm_g[li], mla_w_ukv[li],
                            conv_w[li], w_o[li], norm1_g[li], norm2_g[li], router_w[li],
                            exp_w_gate[li], exp_w_up[li], exp_w_down[li])
        mod_l = mod[li]
        sink = sink_slots.at[:SWA_HEADS].set(swa_sink[li])
        qm, km, vm, qs, ks, vs, cz = _projections(xl, xc, mod_l, lw, tab)
        a = _mla_attention(qm, km, vm, n_ctx, with_ctx=not last)
        bsw = _swa_attention(sink, qs, ks, vs, n_ctx, with_ctx=not last)
        outs = _mixer_out(a, bsw, cz, xl, xc, mod_l, lw, with_ctx=not last)
        x1, aff = outs[0], outs[1]
        idx, taff = _routing(aff, tri_lat, cap_lat)
        xl = _experts(x1, idx, taff, mod_l, None, lw, fg, final=last)
        if not last:
            xc1, affc = outs[2], outs[3]
            idxc, taffc = _routing(affc, tri_ctx, cap_ctx)
            xc = _experts(xc1, idxc, taffc, mod_l, b, lw, fg, final=False)
    return xl
```

```python
import functools

import jax
import jax.numpy as jnp
import numpy as np
from jax import lax
from jax.experimental import pallas as pl
from jax.experimental.pallas import tpu as pltpu

D_MODEL = 1024
GRID_W = 64
NORM_EPS = 1e-6
ROPE_BASE = 10000.0

MLA_HEADS = 6
MLA_Q_RANK = 256
MLA_KV_RANK = 128
MLA_NOPE = 64
MLA_ROPE = 32
MLA_V = 64
MLA_SCALE = (MLA_NOPE + MLA_ROPE) ** -0.5

SWA_HEADS = 6
SWA_KV_HEADS = 2
SWA_REP = SWA_HEADS // SWA_KV_HEADS
SWA_HEAD_DIM = 64
SWA_WINDOW = 128
SWA_SCALE = SWA_HEAD_DIM ** -0.5

CONV_CH = 256
N_EXPERTS = 16
EXPERT_FF = 512
EC_CAPACITY_FACTOR = 2
N_MOD = 6

LANES = 128
SUBLANES = 8
TILE = 256
HEAD_PAD = 128
MOD_ROWS = 40
NEG_BIG = -1e30

_C_CQ = 0
_C_CKV = 256
_C_KR = 384
_C_SQ = 512
_C_SQSW = 896
_C_SK = 1280
_C_SKSW = 1408
_C_SV = 1536
_C_CB = 1664
_C_CC = 1920
_C_CU = 2176
IN_W = 2432

_T_CQ = 0
_T_SQ = 768
_T_KR = 1536
_T_CS = 1664
_T_SS = 2048
_T_CK = 2432
_T_SK = 2560
TAB_W = 2688

_VMEM_LIMIT = 56 * 1024 * 1024


def _cparams(n_grid):
    return pltpu.CompilerParams(dimension_semantics=("arbitrary",) * n_grid,
                                vmem_limit_bytes=_VMEM_LIMIT)


def _silu(v):
    return v * (1.0 / (1.0 + jnp.exp(-v)))


def _rms(v, g):
    return v * lax.rsqrt(jnp.mean(v * v, axis=-1, keepdims=True) + NORM_EPS) * g


def _dot(a, b):
    return jnp.dot(a, b, preferred_element_type=jnp.float32)


def _dot_nt(a, b):
    return lax.dot_general(a, b, (((1,), (1,)), ((), ())), preferred_element_type=jnp.float32)


def _mod_body(c_ref, w_ref, b_ref, o_ref):
    a = _silu(c_ref[...]).astype(jnp.bfloat16)
    o_ref[...] = _dot(a, w_ref[...].astype(jnp.bfloat16)) + b_ref[...]


def _modulation(cc, ada_w, ada_b):
    depth, d, n = ada_w.shape
    bn = 512
    return pl.pallas_call(
        _mod_body,
        grid=(depth, n // bn),
        in_specs=[
            pl.BlockSpec((MOD_ROWS, d), lambda l, i: (0, 0)),
            pl.BlockSpec((None, d, bn), lambda l, i: (l, 0, i)),
            pl.BlockSpec((None, 1, bn), lambda l, i: (l, 0, i)),
        ],
        out_specs=pl.BlockSpec((None, MOD_ROWS, bn), lambda l, i: (l, 0, i)),
        out_shape=jax.ShapeDtypeStruct((depth, MOD_ROWS, n), jnp.float32),
        compiler_params=_cparams(2),
        name="modulation",
    )(cc, ada_w, ada_b.reshape(depth, 1, n))


def _proj_body(x_ref, ctx_ref, mod_ref, g1_ref, win_ref, qg_ref, wuq_ref, kvg_ref, wk_ref, wv_ref,
               e_ref, tab_ref, qm_ref, km_ref, vm_ref, qs_ref, ks_ref, vs_ref, cz_ref, xs_ref):
    j = pl.program_id(0)

    @pl.when(j == 0)
    def _():
        xs_ref[...] = ctx_ref[...]

    @pl.when(j > 0)
    def _():
        xs_ref[...] = x_ref[...]

    d = D_MODEL
    sh1 = mod_ref[:, 0:d]
    sc1 = mod_ref[:, d:2 * d]
    h = (_rms(xs_ref[...], g1_ref[...]) * (1.0 + sc1) + sh1).astype(jnp.bfloat16)

    def proj(lo, hi):
        return _dot(h, win_ref[:, lo:hi])

    def tab(lo, n):
        return tab_ref[:, lo:lo + n]

    cq = _rms(proj(_C_CQ, _C_CKV), qg_ref[...]).astype(jnp.bfloat16)
    uq = _dot(cq, wuq_ref[...])
    nq = MLA_HEADS * HEAD_PAD
    qm_ref[...] = (uq[:, :nq] * tab(_T_CQ, nq) + uq[:, nq:] * tab(_T_SQ, nq)).astype(jnp.bfloat16)

    ckv = _rms(proj(_C_CKV, _C_KR), kvg_ref[...]).astype(jnp.bfloat16)
    krp = (proj(_C_KR, _C_SQ) * tab(_T_KR, LANES)).astype(jnp.bfloat16)
    km_ref[...] = (_dot(ckv, wk_ref[...]) + _dot(krp, e_ref[...])).astype(jnp.bfloat16)
    vm_ref[...] = _dot(ckv, wv_ref[...]).astype(jnp.bfloat16)

    nsq = SWA_HEADS * SWA_HEAD_DIM
    qs_ref[...] = (proj(_C_SQ, _C_SQSW) * tab(_T_CS, nsq)
                   + proj(_C_SQSW, _C_SK) * tab(_T_SS, nsq)).astype(jnp.bfloat16)
    ks_ref[...] = (proj(_C_SK, _C_SKSW) * tab(_T_CK, LANES)
                   + proj(_C_SKSW, _C_SV) * tab(_T_SK, LANES)).astype(jnp.bfloat16)
    vs_ref[...] = proj(_C_SV, _C_CB).astype(jnp.bfloat16)

    cz_ref[:, 0:CONV_CH] = proj(_C_CB, _C_CC).astype(jnp.bfloat16)
    cz_ref[:, CONV_CH:2 * CONV_CH] = (proj(_C_CC, _C_CU) * proj(_C_CU, IN_W)).astype(jnp.bfloat16)


def _projections(x, ctx, mod_l, lw, tab):
    b, s, d = x.shape
    n_ctx = ctx.shape[1]
    t = n_ctx + s
    nt = t // TILE
    const = lambda j, i: (0, 0)
    outs = [(MLA_HEADS * HEAD_PAD, "qm"), (MLA_HEADS * HEAD_PAD, "km"), (MLA_HEADS * MLA_V, "vm"),
            (SWA_HEADS * SWA_HEAD_DIM, "qs"), (LANES, "ks"), (LANES, "vs"), (2 * CONV_CH, "cz")]
    return pl.pallas_call(
        _proj_body,
        grid=(nt, b),
        in_specs=[
            pl.BlockSpec((None, TILE, d), lambda j, i: (jnp.where(j == 0, 0, i), jnp.maximum(j - 1, 0), 0)),
            pl.BlockSpec((None, TILE, d), lambda j, i: (jnp.where(j == 0, i, 0), 0, 0)),
            pl.BlockSpec((None, 1, N_MOD * d), lambda j, i: (jnp.where(j == 0, b, i), 0, 0)),
            pl.BlockSpec((1, d), const),
            pl.BlockSpec((d, IN_W), const),
            pl.BlockSpec((1, MLA_Q_RANK), const),
            pl.BlockSpec((MLA_Q_RANK, 2 * MLA_HEADS * HEAD_PAD), const),
            pl.BlockSpec((1, MLA_KV_RANK), const),
            pl.BlockSpec((MLA_KV_RANK, MLA_HEADS * HEAD_PAD), const),
            pl.BlockSpec((MLA_KV_RANK, MLA_HEADS * MLA_V), const),
            pl.BlockSpec((LANES, MLA_HEADS * HEAD_PAD), const),
            pl.BlockSpec((TILE, TAB_W), lambda j, i: (j, 0)),
        ],
        out_specs=[pl.BlockSpec((None, TILE, w), lambda j, i: (i, j, 0)) for w, _ in outs],
        out_shape=[jax.ShapeDtypeStruct((b, t, w), jnp.bfloat16) for w, _ in outs],
        scratch_shapes=[pltpu.VMEM((TILE, d), jnp.float32)],
        compiler_params=_cparams(2),
        name="projections",
    )(x, ctx, mod_l, lw["g1"], lw["win"], lw["qg"], lw["wuq"], lw["kvg"], lw["wk"], lw["wv"], lw["e"], tab)


def _mla_body(q_ref, k_ref, v_ref, o_ref, *, off, n_ctx):
    t = pl.program_id(1) + off
    lane = lax.broadcasted_iota(jnp.int32, (TILE, LANES), 1)

    def attend(nk):
        for c in range(MLA_HEADS // 2):
            res = []
            for hh in range(2):
                h = 2 * c + hh
                q = q_ref[:, h * HEAD_PAD:(h + 1) * HEAD_PAD]
                k = k_ref[0:nk, h * HEAD_PAD:(h + 1) * HEAD_PAD]
                s = _dot_nt(q, k)
                m = jnp.max(s, axis=-1, keepdims=True)
                p = jnp.exp(s - m)
                l = jnp.sum(p, axis=-1, keepdims=True)
                o = _dot(p.astype(jnp.bfloat16), v_ref[0:nk, c * LANES:(c + 1) * LANES])
                res.append(o / l)
            o_ref[:, c * LANES:(c + 1) * LANES] = jnp.where(lane < MLA_V, res[0], res[1]).astype(jnp.bfloat16)

    if off == 0:
        @pl.when(t == 0)
        def _():
            attend(n_ctx)

    @pl.when(t > 0)
    def _():
        attend(k_ref.shape[0])


def _mla_attention(qm, km, vm, n_ctx, with_ctx):
    b, t, _ = qm.shape
    off = 0 if with_ctx else 1
    nt = t // TILE - off
    return pl.pallas_call(
        functools.partial(_mla_body, off=off, n_ctx=n_ctx),
        grid=(b, nt),
        in_specs=[
            pl.BlockSpec((None, TILE, qm.shape[2]), lambda i, j: (i, j + off, 0)),
            pl.BlockSpec((None, t, km.shape[2]), lambda i, j: (i, 0, 0)),
            pl.BlockSpec((None, t, vm.shape[2]), lambda i, j: (i, 0, 0)),
        ],
        out_specs=pl.BlockSpec((None, TILE, vm.shape[2]), lambda i, j: (i, j + off, 0)),
        out_shape=jax.ShapeDtypeStruct((b, t, vm.shape[2]), jnp.bfloat16),
        compiler_params=_cparams(2),
        name="mla_attention",
    )(qm, km, vm)


def _swa_body(sink_ref, q_ref, k_ref, v_ref, o_ref, *, off, n_ctx, n_lat):
    t = pl.program_id(1) + off
    band = TILE + 2 * SWA_WINDOW
    lane = lax.broadcasted_iota(jnp.int32, (TILE, LANES), 1)
    lo_half = lane < SWA_HEAD_DIM

    def attend(kcat, vcat, valid):
        for c in range(SWA_REP):
            q2 = q_ref[:, c * LANES:(c + 1) * LANES]
            res = []
            for g in range(SWA_KV_HEADS):
                keep = lo_half if g == 0 else jnp.logical_not(lo_half)
                qg = jnp.where(keep, q2, jnp.zeros_like(q2))
                s = _dot_nt(qg, kcat)
                if valid is not None:
                    s = jnp.where(valid, s, NEG_BIG)
                sk = sink_ref[g * SWA_REP + c]
                m = jnp.maximum(jnp.max(s, axis=-1, keepdims=True), sk)
                p = jnp.exp(s - m)
                l = jnp.sum(p, axis=-1, keepdims=True) + jnp.exp(sk - m)
                res.append(_dot(p.astype(jnp.bfloat16), vcat) / l)
            o_ref[:, c * LANES:(c + 1) * LANES] = jnp.where(lo_half, res[0], res[1]).astype(jnp.bfloat16)

    if off == 0:
        @pl.when(t == 0)
        def _():
            attend(k_ref[0:n_ctx, :], v_ref[0:n_ctx, :], None)

    @pl.when(t > 0)
    def _():
        start = (t - 1) * TILE
        ks = jnp.clip(start - SWA_WINDOW, 0, n_lat - band)
        row0 = pl.multiple_of(n_ctx + ks, SWA_WINDOW)
        kcat = jnp.concatenate([k_ref[0:n_ctx, :], k_ref[pl.ds(row0, band), :]], axis=0)
        vcat = jnp.concatenate([v_ref[0:n_ctx, :], v_ref[pl.ds(row0, band), :]], axis=0)
        qpos = start + lax.broadcasted_iota(jnp.int32, (TILE, n_ctx + band), 0)
        col = lax.broadcasted_iota(jnp.int32, (TILE, n_ctx + band), 1)
        kpos = ks + col - n_ctx
        valid = (col < n_ctx) | (jnp.abs(qpos - kpos) <= SWA_WINDOW)
        attend(kcat, vcat, valid)


def _swa_attention(sink, qs, ks, vs, n_ctx, with_ctx):
    b, t, _ = qs.shape
    off = 0 if with_ctx else 1
    nt = t // TILE - off
    return pl.pallas_call(
        functools.partial(_swa_body, off=off, n_ctx=n_ctx, n_lat=t - n_ctx),
        grid=(b, nt),
        in_specs=[
            pl.BlockSpec(memory_space=pltpu.SMEM),
            pl.BlockSpec((None, TILE, qs.shape[2]), lambda i, j: (i, j + off, 0)),
            pl.BlockSpec((None, t, LANES), lambda i, j: (i, 0, 0)),
            pl.BlockSpec((None, t, LANES), lambda i, j: (i, 0, 0)),
        ],
        out_specs=pl.BlockSpec((None, TILE, qs.shape[2]), lambda i, j: (i, j + off, 0)),
        out_shape=jax.ShapeDtypeStruct((b, t, qs.shape[2]), jnp.bfloat16),
        compiler_params=_cparams(2),
        name="swa_attention",
    )(sink, qs, ks, vs)


def _mix_body(*refs, off, with_ctx, nt_all):
    if with_ctx:
        (a_ref, b_ref, cz_ref, hp_ref, hn_ref, x_ref, ctx_ref, mod_ref, modc_ref, cw_ref, wo_ref, n2_ref,
         rw_ref, x1_ref, aff_ref, xc1_ref, affc_ref) = refs
    else:
        (a_ref, b_ref, cz_ref, hp_ref, hn_ref, x_ref, mod_ref, cw_ref, wo_ref, n2_ref,
         rw_ref, x1_ref, aff_ref) = refs
    t = pl.program_id(1) + off
    d = D_MODEL
    halo = hp_ref.shape[0]

    def run(xin_ref, m_ref, xo_ref, ao_ref, first, last):
        z = cz_ref[:, CONV_CH:2 * CONV_CH].astype(jnp.float32)
        zp = hp_ref[halo - 1:halo, CONV_CH:2 * CONV_CH].astype(jnp.float32)
        zn = hn_ref[0:1, CONV_CH:2 * CONV_CH].astype(jnp.float32)
        zp = jnp.where(first, jnp.zeros_like(zp), zp)
        zn = jnp.where(last, jnp.zeros_like(zn), zn)
        row = lax.broadcasted_iota(jnp.int32, z.shape, 0)
        z_dn = jnp.where(row == 0, zp, pltpu.roll(z, 1, 0))
        z_up = jnp.where(row == TILE - 1, zn, pltpu.roll(z, TILE - 1, 0))
        y = z_dn * cw_ref[0:1, :] + z * cw_ref[1:2, :] + z_up * cw_ref[2:3, :]
        cv = (cz_ref[:, 0:CONV_CH].astype(jnp.float32) * y).astype(jnp.bfloat16)
        mix = jnp.concatenate([a_ref[...], b_ref[...], cv], axis=-1)
        g1 = m_ref[:, 2 * d:3 * d]
        sh2 = m_ref[:, 3 * d:4 * d]
        sc2 = m_ref[:, 4 * d:5 * d]
        x1 = xin_ref[...] + g1 * _dot(mix, wo_ref[...])
        xo_ref[...] = x1
        h2 = (_rms(x1, n2_ref[...]) * (1.0 + sc2) + sh2).astype(jnp.bfloat16)
        lg = _dot_nt(rw_ref[...], h2)
        ex = jnp.exp(lg - jnp.max(lg, axis=0, keepdims=True))
        ao_ref[...] = ex / jnp.sum(ex, axis=0, keepdims=True)

    if with_ctx:
        @pl.when(t == 0)
        def _():
            run(ctx_ref, modc_ref, xc1_ref, affc_ref, True, True)

    @pl.when(t > 0)
    def _():
        run(x_ref, mod_ref, x1_ref, aff_ref, t == 1, t == nt_all - 1)


def _mixer_out(a, bsw, cz, x, ctx, mod_l, lw, with_ctx):
    b, t, _ = a.shape
    s = x.shape[1]
    d = x.shape[2]
    n_ctx = t - s
    off = 0 if with_ctx else 1
    nt_all = t // TILE
    nt = nt_all - off
    halo = 16
    hb = TILE // halo
    lat = lambda i, j: (i, jnp.maximum(j + off - 1, 0), 0)
    const = lambda i, j: (0, 0)
    in_specs = [
        pl.BlockSpec((None, TILE, a.shape[2]), lambda i, j: (i, j + off, 0)),
        pl.BlockSpec((None, TILE, bsw.shape[2]), lambda i, j: (i, j + off, 0)),
        pl.BlockSpec((None, TILE, cz.shape[2]), lambda i, j: (i, j + off, 0)),
        pl.BlockSpec((None, halo, cz.shape[2]), lambda i, j: (i, jnp.maximum((j + off) * hb - 1, 0), 0)),
        pl.BlockSpec((None, halo, cz.shape[2]),
                     lambda i, j: (i, jnp.minimum((j + off + 1) * hb, t // halo - 1), 0)),
        pl.BlockSpec((None, TILE, d), lat),
    ]
    args = [a, bsw, cz, cz, cz, x]
    if with_ctx:
        in_specs.append(pl.BlockSpec((None, n_ctx, d), lambda i, j: (i, 0, 0)))
        args.append(ctx)
    in_specs.append(pl.BlockSpec((None, 1, N_MOD * d), lambda i, j: (i, 0, 0)))
    args.append(mod_l)
    if with_ctx:
        in_specs.append(pl.BlockSpec((None, 1, N_MOD * d), lambda i, j: (b, 0, 0)))
        args.append(mod_l)
    in_specs += [
        pl.BlockSpec((3, CONV_CH), const),
        pl.BlockSpec((d, d), const),
        pl.BlockSpec((1, d), const),
        pl.BlockSpec((N_EXPERTS, d), const),
    ]
    args += [lw["cw"], lw["wo"], lw["n2"], lw["rw"]]
    out_specs = [
        pl.BlockSpec((None, TILE, d), lat),
        pl.BlockSpec((None, N_EXPERTS, TILE), lambda i, j: (i, 0, jnp.maximum(j + off - 1, 0))),
    ]
    out_shape = [jax.ShapeDtypeStruct((b, s, d), jnp.float32),
                 jax.ShapeDtypeStruct((b, N_EXPERTS, s), jnp.float32)]
    if with_ctx:
        out_specs += [pl.BlockSpec((None, n_ctx, d), lambda i, j: (i, 0, 0)),
                      pl.BlockSpec((None, N_EXPERTS, n_ctx), lambda i, j: (i, 0, 0))]
        out_shape += [jax.ShapeDtypeStruct((b, n_ctx, d), jnp.float32),
                      jax.ShapeDtypeStruct((b, N_EXPERTS, n_ctx), jnp.float32)]
    return pl.pallas_call(
        functools.partial(_mix_body, off=off, with_ctx=with_ctx, nt_all=nt_all),
        grid=(b, nt),
        in_specs=in_specs,
        out_specs=out_specs,
        out_shape=out_shape,
        compiler_params=_cparams(2),
        name="mixer_out",
    )(*args)


ROUTE_ROWS = 128
ROUTE_SLOTS = 64


def _route_body(aff_ref, tri_ref, idx_ref, taff_ref, pos_ref, oi_ref, oa_ref, *, cap):
    rows, n = aff_ref.shape
    aff = aff_ref[...]
    bits = pltpu.bitcast(aff, jnp.int32)

    def count(mask):
        return jnp.sum(jnp.where(mask, 1.0, 0.0), axis=1, keepdims=True)

    def ones(mask):
        return jnp.where(mask, 1.0, 0.0).astype(jnp.bfloat16)

    def search(i, thr):
        cand = thr | (1 << (30 - i))
        return jnp.where(count(bits >= cand) >= cap, cand, thr)

    thr = lax.fori_loop(0, 31, search, jnp.zeros((rows, 1), jnp.int32))
    gt = bits > thr
    eq = bits == thr
    need = cap - count(gt)
    peq = _dot(ones(eq), tri_ref[...])
    sel = gt | (eq & (peq < need))
    pos = _dot(ones(sel), tri_ref[...])
    pos_ref[...] = jnp.where(sel, pos, -1.0)

    n_sweep = min(ROUTE_SLOTS, cap)
    tok = lax.broadcasted_iota(jnp.int32, (n_sweep, n), 1).astype(jnp.float32)
    lane = lax.broadcasted_iota(jnp.int32, (cap, rows), 1)
    oi_ref[...] = jnp.zeros_like(oi_ref)
    oa_ref[...] = jnp.zeros_like(oa_ref)

    def per_row(r, carry):
        prow = pos_ref[pl.ds(r, 1), :]
        arow = aff_ref[pl.ds(r, 1), :]
        icols = []
        acols = []
        for j0 in range(0, cap, n_sweep):
            slot = (lax.broadcasted_iota(jnp.int32, (n_sweep, 1), 0) + j0).astype(jnp.float32)
            hit = prow == slot
            icols.append(jnp.sum(jnp.where(hit, tok, 0.0), axis=1, keepdims=True))
            acols.append(jnp.sum(jnp.where(hit, arow, 0.0), axis=1, keepdims=True))
        icol = jnp.concatenate(icols, axis=0) if len(icols) > 1 else icols[0]
        acol = jnp.concatenate(acols, axis=0) if len(acols) > 1 else acols[0]
        oi_ref[...] = jnp.where(lane == r, icol, oi_ref[...])
        oa_ref[...] = jnp.where(lane == r, acol, oa_ref[...])
        return carry

    lax.fori_loop(0, rows, per_row, 0)
    idx_ref[...] = jnp.transpose(oi_ref[...]).astype(jnp.int32)
    taff_ref[...] = jnp.transpose(oa_ref[...])


def _routing(aff, tri, cap):
    b, e, n = aff.shape
    rows = b * e
    rr = min(ROUTE_ROWS, rows)
    idx, taff = pl.pallas_call(
        functools.partial(_route_body, cap=cap),
        grid=(rows // rr,),
        in_specs=[pl.BlockSpec((rr, n), lambda i: (i, 0)),
                  pl.BlockSpec((n, n), lambda i: (0, 0))],
        out_specs=[pl.BlockSpec((rr, cap), lambda i: (i, 0)),
                   pl.BlockSpec((rr, cap), lambda i: (i, 0))],
        out_shape=[jax.ShapeDtypeStruct((rows, cap), jnp.int32),
                   jax.ShapeDtypeStruct((rows, cap), jnp.float32)],
        scratch_shapes=[pltpu.VMEM((rr, n), jnp.float32),
                        pltpu.VMEM((cap, rr), jnp.float32),
                        pltpu.VMEM((cap, rr), jnp.float32)],
        compiler_params=_cparams(1),
        name="routing",
    )(aff.reshape(rows, n), tri)
    return idx.reshape(b, e, cap), taff.reshape(b, e, cap)


EXP_PER_STEP = 2
MOE_CHUNK = 512
SCATTER_GROUP = 16


def _moe_body(idx_ref, taff_ref, x_ref, mod_ref, n2_ref, fg_ref, wg_ref, wu_ref, wd_ref, o_ref,
              h_ref, acc_ref, *bufs, nch, cap, final):
    j = pl.program_id(1)
    d = D_MODEL
    nsl = d // LANES
    rows = MOE_CHUNK * nsl
    nes = N_EXPERTS // EXP_PER_STEP
    xg_refs = bufs[:EXP_PER_STEP]
    yb_refs = bufs[EXP_PER_STEP:]

    @pl.when(j < nch)
    def _():
        sh2 = mod_ref[:, 3 * d:4 * d]
        sc2 = mod_ref[:, 4 * d:5 * d]
        h2 = _rms(x_ref[...], n2_ref[...]) * (1.0 + sc2) + sh2
        base = pl.multiple_of(j * rows, rows)
        for s in range(nsl):
            h_ref[pl.ds(base + s, MOE_CHUNK, stride=nsl), :] = h2[:, s * LANES:(s + 1) * LANES]
        acc_ref[pl.ds(base, rows), :] = jnp.zeros((rows, LANES), jnp.float32)

    @pl.when((j >= nch) & (j < nch + nes))
    def _():
        for k in range(EXP_PER_STEP):
            xg_ref, yb_ref = xg_refs[k], yb_refs[k]
            tok = [pl.multiple_of(idx_ref[0, k * cap + jj] * nsl, nsl) for jj in range(cap)]
            for jj in range(cap):
                xg_ref[jj * nsl:(jj + 1) * nsl, :] = h_ref[pl.ds(tok[jj], nsl), :]
            xs = jnp.concatenate([xg_ref[pl.ds(s, cap, stride=nsl), :] for s in range(nsl)], axis=-1)
            xs = xs.astype(jnp.bfloat16)
            gate = _dot(xs, wg_ref[k])
            up = _dot(xs, wu_ref[k])
            hid = (_silu(gate) * up).astype(jnp.bfloat16)
            y = _dot(hid, wd_ref[k])
            for s in range(nsl):
                yb_ref[pl.ds(s, cap, stride=nsl), :] = y[:, s * LANES:(s + 1) * LANES]
            for g0 in range(0, cap, SCATTER_GROUP):
                grp = range(g0, g0 + SCATTER_GROUP)
                vals = [acc_ref[pl.ds(tok[jj], nsl), :]
                        + taff_ref[0, k * cap + jj] * yb_ref[jj * nsl:(jj + 1) * nsl, :] for jj in grp]
                for jj, v in zip(grp, vals):
                    acc_ref[pl.ds(tok[jj], nsl), :] = v

    @pl.when(j >= nch + nes)
    def _():
        c = j - nch - nes
        base = pl.multiple_of(c * rows, rows)
        m = jnp.concatenate([acc_ref[pl.ds(base + s, MOE_CHUNK, stride=nsl), :] for s in range(nsl)], axis=-1)
        g2 = mod_ref[:, 5 * d:6 * d]
        x2 = x_ref[...] + g2 * m
        if final:
            x2 = _rms(x2, fg_ref[...])
        o_ref[...] = x2


def _experts(x1, idx, taff, mod_l, mod_row0, lw, final_g, final):
    nb, n_tok, d = x1.shape
    cap = idx.shape[2]
    assert n_tok % MOE_CHUNK == 0 and cap % SCATTER_GROUP == 0
    nch = n_tok // MOE_CHUNK
    nes = N_EXPERTS // EXP_PER_STEP
    steps = 2 * nch + nes
    nsl = d // LANES
    idx = idx.reshape(nb, nes, 1, EXP_PER_STEP * cap)
    taff = taff.reshape(nb, nes, 1, EXP_PER_STEP * cap)

    def chunk_in(i, j):
        c = jnp.where(j < nch, j, jnp.where(j < nch + nes, nch - 1, j - nch - nes))
        return (i, c, 0)

    def chunk_out(i, j):
        return (i, jnp.clip(j - nch - nes, 0, nch - 1), 0)

    def slots(i, j):
        return (i, jnp.clip(j - nch, 0, nes - 1), 0, 0)

    def expert(i, j):
        return (jnp.clip(j - nch, 0, nes - 1), 0, 0)

    const = lambda i, j: (0, 0)
    if mod_row0 is None:
        mod_map = lambda i, j: (i, 0, 0)
    else:
        mod_map = lambda i, j: (mod_row0, 0, 0)
    slot_buf = pltpu.VMEM((cap * nsl, LANES), jnp.float32)
    return pl.pallas_call(
        functools.partial(_moe_body, nch=nch, cap=cap, final=final),
        grid=(nb, steps),
        in_specs=[
            pl.BlockSpec((None, None, 1, EXP_PER_STEP * cap), slots, memory_space=pltpu.SMEM),
            pl.BlockSpec((None, None, 1, EXP_PER_STEP * cap), slots, memory_space=pltpu.SMEM),
            pl.BlockSpec((None, MOE_CHUNK, d), chunk_in),
            pl.BlockSpec((None, 1, N_MOD * d), mod_map),
            pl.BlockSpec((1, d), const),
            pl.BlockSpec((1, d), const),
            pl.BlockSpec((EXP_PER_STEP, d, EXPERT_FF), expert),
            pl.BlockSpec((EXP_PER_STEP, d, EXPERT_FF), expert),
            pl.BlockSpec((EXP_PER_STEP, EXPERT_FF, d), expert),
        ],
        out_specs=pl.BlockSpec((None, MOE_CHUNK, d), chunk_out),
        out_shape=jax.ShapeDtypeStruct((nb, n_tok, d), jnp.float32),
        scratch_shapes=[pltpu.VMEM((n_tok * nsl, LANES), jnp.float32),
                        pltpu.VMEM((n_tok * nsl, LANES), jnp.float32)] + [slot_buf] * (2 * EXP_PER_STEP),
        compiler_params=_cparams(2),
        name="experts",
    )(idx, taff, x1, mod_l, lw["n2"], final_g, lw["wg"], lw["wu"], lw["wd"])


def _group_samples(x1, idx, taff):
    b, n_tok, d = x1.shape
    cap = idx.shape[2]
    g = max(k for k in range(1, 9) if b % k == 0 and (k * n_tok) % MOE_CHUNK == 0)
    off = (jnp.arange(b, dtype=jnp.int32) % g) * n_tok
    idx = idx + off[:, None, None]

    def merge(t):
        t = t.reshape(b // g, g, N_EXPERTS, cap)
        return jnp.swapaxes(t, 1, 2).reshape(b // g, N_EXPERTS, g * cap)

    return x1.reshape(b // g, g * n_tok, d), merge(idx), merge(taff)


def _swap_halves(w):
    half = w.shape[-1] // 2
    return jnp.concatenate([-w[..., half:], w[..., :half]], axis=-1)


def _swa_perm():
    cols = []
    for c in range(SWA_REP):
        for g in range(SWA_KV_HEADS):
            h = g * SWA_REP + c
            cols.extend(range(h * SWA_HEAD_DIM, (h + 1) * SWA_HEAD_DIM))
    return np.asarray(cols, np.int32)


def _layer_weights(w_in, q_g, w_uq, kv_g, w_ukv, conv_w, w_o, n1, n2, router_w, wg, wu, wd):
    bf = jnp.bfloat16
    d = w_in.shape[0]
    offs = np.cumsum([0, MLA_Q_RANK, MLA_KV_RANK, MLA_ROPE, SWA_HEADS * SWA_HEAD_DIM,
                      SWA_KV_HEADS * SWA_HEAD_DIM, SWA_KV_HEADS * SWA_HEAD_DIM, CONV_CH, CONV_CH, CONV_CH])
    part = [w_in[:, offs[i]:offs[i + 1]] for i in range(9)]
    w_cq, w_ckv, w_kr, w_sq, w_sk, w_sv, w_cb, w_cc, w_cu = part
    perm = _swa_perm()

    def per_head_swap(w, nh):
        wh = w.reshape(d, nh, -1)
        return _swap_halves(wh).reshape(d, -1)

    kr_block = jnp.concatenate([w_kr, _swap_halves(w_kr), jnp.zeros((d, LANES - 2 * MLA_ROPE), w_in.dtype)], axis=1)
    win = jnp.concatenate([
        w_cq, w_ckv, kr_block,
        w_sq[:, perm], per_head_swap(w_sq, SWA_HEADS)[:, perm],
        w_sk, per_head_swap(w_sk, SWA_KV_HEADS), w_sv, w_cb, w_cc, w_cu], axis=1).astype(bf)

    qd = MLA_NOPE + MLA_ROPE
    uq = w_uq.reshape(MLA_Q_RANK, MLA_HEADS, qd)
    zpad = jnp.zeros((MLA_Q_RANK, MLA_HEADS, HEAD_PAD - qd), w_uq.dtype)
    main = jnp.concatenate([uq, zpad], axis=-1)
    swp = jnp.concatenate([jnp.zeros_like(uq[..., :MLA_NOPE]), _swap_halves(uq[..., MLA_NOPE:]), zpad], axis=-1)
    wuq = jnp.concatenate([main.reshape(MLA_Q_RANK, -1), swp.reshape(MLA_Q_RANK, -1)], axis=1).astype(bf)

    ukv = w_ukv.reshape(MLA_KV_RANK, MLA_HEADS, MLA_NOPE + MLA_V)
    wk = jnp.concatenate([ukv[..., :MLA_NOPE],
                          jnp.zeros((MLA_KV_RANK, MLA_HEADS, HEAD_PAD - MLA_NOPE), w_ukv.dtype)], axis=-1)
    wk = wk.reshape(MLA_KV_RANK, -1).astype(bf)
    wv = ukv[..., MLA_NOPE:].reshape(MLA_KV_RANK, -1).astype(bf)

    e = np.zeros((LANES, MLA_HEADS * HEAD_PAD), np.float32)
    for h in range(MLA_HEADS):
        for l in range(MLA_ROPE):
            e[l, h * HEAD_PAD + MLA_NOPE + l] = 1.0
            e[MLA_ROPE + l, h * HEAD_PAD + MLA_NOPE + l] = 1.0

    n_mla = MLA_HEADS * MLA_V
    n_swa = SWA_HEADS * SWA_HEAD_DIM
    wo = jnp.concatenate([w_o[:n_mla], w_o[n_mla:n_mla + n_swa][perm], w_o[n_mla + n_swa:]], axis=0).astype(bf)
    return dict(
        g1=n1.reshape(1, -1), win=win, qg=q_g.reshape(1, -1), wuq=wuq, kvg=kv_g.reshape(1, -1),
        wk=wk, wv=wv, e=jnp.asarray(e, bf), cw=conv_w, wo=wo, n2=n2.reshape(1, -1),
        rw=router_w.T.astype(bf), wg=wg.astype(bf), wu=wu.astype(bf), wd=wd.astype(bf))


def _axial_tables(n_tokens, rot_dim):
    rows = n_tokens // GRID_W
    row = jnp.repeat(jnp.arange(rows, dtype=jnp.float32), GRID_W)
    col = jnp.tile(jnp.arange(GRID_W, dtype=jnp.float32), rows)
    n_freq = rot_dim // 4
    inv = ROPE_BASE ** (-jnp.arange(n_freq, dtype=jnp.float32) / n_freq)
    ang = jnp.concatenate([row[:, None] * inv, col[:, None] * inv], axis=-1)
    return jnp.cos(ang), jnp.sin(ang)


def _row_table(n_ctx, n_lat):
    def with_ctx(cos, sin):
        one = jnp.ones((n_ctx, cos.shape[1]), jnp.float32)
        return (jnp.concatenate([one, cos], axis=0), jnp.concatenate([0.0 * one, sin], axis=0))

    cm, sm = with_ctx(*_axial_tables(n_lat, MLA_ROPE))
    cs, ss = with_ctx(*_axial_tables(n_lat, SWA_HEAD_DIM))
    t = n_ctx + n_lat
    cm2 = jnp.concatenate([cm, cm], axis=1)
    sm2 = jnp.concatenate([sm, sm], axis=1)
    cs2 = jnp.concatenate([cs, cs], axis=1)
    ss2 = jnp.concatenate([ss, ss], axis=1)
    pad = jnp.zeros((t, HEAD_PAD - MLA_NOPE - MLA_ROPE), jnp.float32)
    cq = jnp.concatenate([jnp.ones((t, MLA_NOPE), jnp.float32), cm2, pad], axis=1) * MLA_SCALE
    sq = jnp.concatenate([jnp.zeros((t, MLA_NOPE), jnp.float32), sm2, pad], axis=1) * MLA_SCALE
    kr = jnp.concatenate([cm2, sm2, jnp.zeros((t, LANES - 2 * MLA_ROPE), jnp.float32)], axis=1)
    tab = jnp.concatenate([
        jnp.tile(cq, (1, MLA_HEADS)), jnp.tile(sq, (1, MLA_HEADS)), kr,
        jnp.tile(cs2, (1, SWA_HEADS)) * SWA_SCALE, jnp.tile(ss2, (1, SWA_HEADS)) * SWA_SCALE,
        jnp.tile(cs2, (1, SWA_KV_HEADS)), jnp.tile(ss2, (1, SWA_KV_HEADS))], axis=1)
    return tab


def _strict_upper(n):
    r = lax.broadcasted_iota(jnp.int32, (n, n), 0)
    c = lax.broadcasted_iota(jnp.int32, (n, n), 1)
    return (r < c).astype(jnp.bfloat16)


def kernel(x, c, ctx, c_ctx, ada_w, ada_b, norm1_g, w_in, mla_q_norm_g, mla_w_uq, mla_kv_norm_g, mla_w_ukv,
           swa_sink, conv_w, w_o, norm2_g, router_w, exp_w_gate, exp_w_up, exp_w_down, final_norm_g):
    b, s, d = x.shape
    n_ctx = ctx.shape[1]
    depth = ada_w.shape[0]
    assert d == D_MODEL and s % TILE == 0 and n_ctx == TILE and b + 1 <= MOD_ROWS
    cap_lat = EC_CAPACITY_FACTOR * s // N_EXPERTS
    cap_ctx = EC_CAPACITY_FACTOR * n_ctx // N_EXPERTS

    cc = jnp.concatenate([c, c_ctx[None, :], jnp.zeros((MOD_ROWS - b - 1, d), c.dtype)], axis=0)
    mod = _modulation(cc, ada_w, ada_b).reshape(depth, MOD_ROWS, 1, N_MOD * d)
    tab = _row_table(n_ctx, s)
    tri_lat = _strict_upper(s)
    tri_ctx = _strict_upper(n_ctx)
    sink_slots = jnp.zeros((8,), jnp.float32)
    fg = final_norm_g.reshape(1, d)

    xl, xc = x, ctx
    for li in range(depth):
        last = li == depth - 1
        lw = _layer_weights(w_in[li], mla_q_norm_g[li], mla_w_uq[li], mla_kv_norm_g[li], mla_w_ukv[li],
                            conv_w[li], w_o[li], norm1_g[li], norm2_g[li], router_w[li],
                            exp_w_gate[li], exp_w_up[li], exp_w_down[li])
        mod_l = mod[li]
        sink = sink_slots.at[:SWA_HEADS].set(swa_sink[li])
        qm, km, vm, qs, ks, vs, cz = _projections(xl, xc, mod_l, lw, tab)
        a = _mla_attention(qm, km, vm, n_ctx, with_ctx=not last)
        bsw = _swa_attention(sink, qs, ks, vs, n_ctx, with_ctx=not last)
        outs = _mixer_out(a, bsw, cz, xl, xc, mod_l, lw, with_ctx=not last)
        x1, aff = outs[0], outs[1]
        idx, taff = _routing(aff, tri_lat, cap_lat)
        xl = _experts(x1, idx, taff, mod_l, None, lw, fg, final=last)
        if not last:
            xc1, affc = outs[2], outs[3]
            idxc, taffc = _routing(affc, tri_ctx, cap_ctx)
            xc = _experts(*_group_samples(xc1, idxc, taffc), mod_l, b, lw, fg, final=False).reshape(xc1.shape)
    return xl
```

```python
import functools

import jax
import jax.numpy as jnp
import numpy as np
from jax import lax
from jax.experimental import pallas as pl
from jax.experimental.pallas import tpu as pltpu

D_MODEL = 1024
GRID_W = 64
NORM_EPS = 1e-6
ROPE_BASE = 10000.0

MLA_HEADS = 6
MLA_Q_RANK = 256
MLA_KV_RANK = 128
MLA_NOPE = 64
MLA_ROPE = 32
MLA_V = 64
MLA_SCALE = (MLA_NOPE + MLA_ROPE) ** -0.5

SWA_HEADS = 6
SWA_KV_HEADS = 2
SWA_REP = SWA_HEADS // SWA_KV_HEADS
SWA_HEAD_DIM = 64
SWA_WINDOW = 128
SWA_SCALE = SWA_HEAD_DIM ** -0.5

CONV_CH = 256
N_EXPERTS = 16
EXPERT_FF = 512
EC_CAPACITY_FACTOR = 2
N_MOD = 6

LANES = 128
SUBLANES = 8
TILE = 256
HEAD_PAD = 128
MOD_ROWS = 40
NEG_BIG = -1e30
LOG2E = 1.4426950408889634

_C_CQ = 0
_C_CKV = 256
_C_KR = 384
_C_SQ = 512
_C_SQSW = 896
_C_SK = 1280
_C_SKSW = 1408
_C_SV = 1536
_C_CB = 1664
_C_CC = 1920
_C_CU = 2176
IN_W = 2432

_T_CQ = 0
_T_SQ = 768
_T_KR = 1536
_T_CS = 1664
_T_SS = 2048
_T_CK = 2432
_T_SK = 2560
TAB_W = 2688

_VMEM_LIMIT = 56 * 1024 * 1024


def _cparams(n_grid):
    return pltpu.CompilerParams(dimension_semantics=("arbitrary",) * n_grid,
                                vmem_limit_bytes=_VMEM_LIMIT)


def _silu(v):
    return v * (1.0 / (1.0 + jnp.exp(-v)))


def _rms(v, g):
    return v * lax.rsqrt(jnp.mean(v * v, axis=-1, keepdims=True) + NORM_EPS) * g


def _dot(a, b):
    return jnp.dot(a, b, preferred_element_type=jnp.float32)


def _dot_nt(a, b):
    return lax.dot_general(a, b, (((1,), (1,)), ((), ())), preferred_element_type=jnp.float32)


def _mod_body(c_ref, w_ref, b_ref, o_ref):
    a = _silu(c_ref[...]).astype(jnp.bfloat16)
    o_ref[...] = _dot(a, w_ref[...].astype(jnp.bfloat16)) + b_ref[...]


def _modulation(cc, ada_w, ada_b):
    depth, d, n = ada_w.shape
    bn = 512
    return pl.pallas_call(
        _mod_body,
        grid=(depth, n // bn),
        in_specs=[
            pl.BlockSpec((MOD_ROWS, d), lambda l, i: (0, 0)),
            pl.BlockSpec((None, d, bn), lambda l, i: (l, 0, i)),
            pl.BlockSpec((None, 1, bn), lambda l, i: (l, 0, i)),
        ],
        out_specs=pl.BlockSpec((None, MOD_ROWS, bn), lambda l, i: (l, 0, i)),
        out_shape=jax.ShapeDtypeStruct((depth, MOD_ROWS, n), jnp.float32),
        compiler_params=_cparams(2),
        name="modulation",
    )(cc, ada_w, ada_b.reshape(depth, 1, n))


def _proj_body(x_ref, ctx_ref, mod_ref, g1_ref, win_ref, qg_ref, wuq_ref, kvg_ref, wk_ref, wvt_ref,
               e_ref, tab_ref, qm_ref, km_ref, vmt_ref, qs_ref, ks_ref, vs_ref, cz_ref, xs_ref):
    j = pl.program_id(0)
    is_ctx = j == pl.num_programs(0) - 1

    @pl.when(is_ctx)
    def _():
        xs_ref[...] = ctx_ref[...]

    @pl.when(jnp.logical_not(is_ctx))
    def _():
        xs_ref[...] = x_ref[...]

    d = D_MODEL
    sh1 = mod_ref[:, 0:d]
    sc1 = mod_ref[:, d:2 * d]
    h = (_rms(xs_ref[...], g1_ref[...]) * (1.0 + sc1) + sh1).astype(jnp.bfloat16)

    def proj(lo, hi):
        return _dot(h, win_ref[:, lo:hi])

    def tab(lo, n):
        return tab_ref[:, lo:lo + n]

    cq = _rms(proj(_C_CQ, _C_CKV), qg_ref[...]).astype(jnp.bfloat16)
    uq = _dot(cq, wuq_ref[...])
    nq = MLA_HEADS * HEAD_PAD
    qm_ref[...] = (uq[:, :nq] * tab(_T_CQ, nq) + uq[:, nq:] * tab(_T_SQ, nq)).astype(jnp.bfloat16)

    ckv = _rms(proj(_C_CKV, _C_KR), kvg_ref[...]).astype(jnp.bfloat16)
    krp = (proj(_C_KR, _C_SQ) * tab(_T_KR, LANES)).astype(jnp.bfloat16)
    km_ref[...] = (_dot(ckv, wk_ref[...]) + _dot(krp, e_ref[...])).astype(jnp.bfloat16)
    vmt_ref[...] = _dot_nt(wvt_ref[...], ckv).astype(jnp.bfloat16)

    nsq = SWA_HEADS * SWA_HEAD_DIM
    qs_ref[...] = (proj(_C_SQ, _C_SQSW) * tab(_T_CS, nsq)
                   + proj(_C_SQSW, _C_SK) * tab(_T_SS, nsq)).astype(jnp.bfloat16)
    ks_ref[...] = (proj(_C_SK, _C_SKSW) * tab(_T_CK, LANES)
                   + proj(_C_SKSW, _C_SV) * tab(_T_SK, LANES)).astype(jnp.bfloat16)
    vs_ref[...] = proj(_C_SV, _C_CB).astype(jnp.bfloat16)

    cz_ref[:, 0:CONV_CH] = proj(_C_CB, _C_CC).astype(jnp.bfloat16)
    cz_ref[:, CONV_CH:2 * CONV_CH] = (proj(_C_CC, _C_CU) * proj(_C_CU, IN_W)).astype(jnp.bfloat16)


def _projections(x, ctx, mod_l, lw, tab):
    b, s, d = x.shape
    n_ctx = ctx.shape[1]
    t = n_ctx + s
    nl = s // TILE
    const = lambda j, i: (0, 0)
    row_outs = [MLA_HEADS * HEAD_PAD, MLA_HEADS * HEAD_PAD, None, SWA_HEADS * SWA_HEAD_DIM, LANES, LANES,
                2 * CONV_CH]
    nv = MLA_HEADS * MLA_V
    out_specs = [pl.BlockSpec((None, nv, TILE), lambda j, i: (i, 0, j)) if w is None
                 else pl.BlockSpec((None, TILE, w), lambda j, i: (i, j, 0)) for w in row_outs]
    out_shape = [jax.ShapeDtypeStruct((b, nv, t) if w is None else (b, t, w), jnp.bfloat16) for w in row_outs]
    return pl.pallas_call(
        _proj_body,
        grid=(nl + 1, b),
        in_specs=[
            pl.BlockSpec((None, TILE, d), lambda j, i: (jnp.where(j == nl, 0, i), jnp.minimum(j, nl - 1), 0)),
            pl.BlockSpec((None, TILE, d), lambda j, i: (jnp.where(j == nl, i, 0), 0, 0)),
            pl.BlockSpec((None, 1, N_MOD * d), lambda j, i: (jnp.where(j == nl, b, i), 0, 0)),
            pl.BlockSpec((1, d), const),
            pl.BlockSpec((d, IN_W), const),
            pl.BlockSpec((1, MLA_Q_RANK), const),
            pl.BlockSpec((MLA_Q_RANK, 2 * MLA_HEADS * HEAD_PAD), const),
            pl.BlockSpec((1, MLA_KV_RANK), const),
            pl.BlockSpec((MLA_KV_RANK, MLA_HEADS * HEAD_PAD), const),
            pl.BlockSpec((nv, MLA_KV_RANK), const),
            pl.BlockSpec((LANES, MLA_HEADS * HEAD_PAD), const),
            pl.BlockSpec((TILE, TAB_W), lambda j, i: (j, 0)),
        ],
        out_specs=out_specs,
        out_shape=out_shape,
        scratch_shapes=[pltpu.VMEM((TILE, d), jnp.float32)],
        compiler_params=_cparams(2),
        name="projections",
    )(x, ctx, mod_l, lw["g1"], lw["win"], lw["qg"], lw["wuq"], lw["kvg"], lw["wk"], lw["wvt"], lw["e"], tab)


MLA_TQ = 2048


def _mla_pair(q_ref, k_ref, vt_ref, o_ref, c):
    tq = q_ref.shape[0]
    res = []
    for hh in range(2):
        lo = (2 * c + hh) * HEAD_PAD
        st = _dot_nt(k_ref[:, lo:lo + HEAD_PAD], q_ref[:, lo:lo + HEAD_PAD])
        m = jnp.max(st, axis=0, keepdims=True)
        p = jnp.exp2(st - m)
        l = jnp.sum(p, axis=0, keepdims=True)
        ot = _dot(vt_ref[c * LANES:(c + 1) * LANES, :], p.astype(jnp.bfloat16))
        res.append(ot / l)
    row = lax.broadcasted_iota(jnp.int32, (LANES, tq), 0)
    o_ref[:, c * LANES:(c + 1) * LANES] = jnp.transpose(
        jnp.where(row < MLA_V, res[0], res[1])).astype(jnp.bfloat16)


def _mla_main_body(q_ref, k_ref, vt_ref, o_ref):
    for c in range(MLA_HEADS // 2):
        _mla_pair(q_ref, k_ref, vt_ref, o_ref, c)


def _mla_ctx_body(q_ref, k_ref, vt_ref, o_ref):
    for c in range(MLA_HEADS // 2):
        _mla_pair(q_ref, k_ref, vt_ref, o_ref, c)


def _mla_attention(qm, km, vmt, n_lat):
    b, t, _ = qm.shape
    return pl.pallas_call(
        _mla_main_body,
        grid=(b, n_lat // MLA_TQ),
        in_specs=[
            pl.BlockSpec((None, MLA_TQ, qm.shape[2]), lambda i, j: (i, j, 0)),
            pl.BlockSpec((None, t, km.shape[2]), lambda i, j: (i, 0, 0)),
            pl.BlockSpec((None, vmt.shape[1], t), lambda i, j: (i, 0, 0)),
        ],
        out_specs=pl.BlockSpec((None, MLA_TQ, vmt.shape[1]), lambda i, j: (i, j, 0)),
        out_shape=jax.ShapeDtypeStruct((b, n_lat, vmt.shape[1]), jnp.bfloat16),
        compiler_params=_cparams(2),
        name="mla_attention",
    )(qm, km, vmt)


def _mla_attention_ctx(qm, km, vmt, n_lat):
    b, t, _ = qm.shape
    n_ctx = t - n_lat
    blk = n_lat // n_ctx
    return pl.pallas_call(
        _mla_ctx_body,
        grid=(b,),
        in_specs=[
            pl.BlockSpec((None, n_ctx, qm.shape[2]), lambda i: (i, blk, 0)),
            pl.BlockSpec((None, n_ctx, km.shape[2]), lambda i: (i, blk, 0)),
            pl.BlockSpec((None, vmt.shape[1], n_ctx), lambda i: (i, 0, blk)),
        ],
        out_specs=pl.BlockSpec((None, n_ctx, vmt.shape[1]), lambda i: (i, 0, 0)),
        out_shape=jax.ShapeDtypeStruct((b, n_ctx, vmt.shape[1]), jnp.bfloat16),
        compiler_params=_cparams(1),
        name="mla_attention_ctx",
    )(qm, km, vmt)


def _swa_attend(sink_ref, q_ref, kcat, vcat, valid, o_ref):
    rows = q_ref.shape[0]
    lane = lax.broadcasted_iota(jnp.int32, (rows, LANES), 1)
    lo_half = lane < SWA_HEAD_DIM
    for c in range(SWA_REP):
        q2 = q_ref[:, c * LANES:(c + 1) * LANES]
        res = []
        for g in range(SWA_KV_HEADS):
            keep = lo_half if g == 0 else jnp.logical_not(lo_half)
            qg = jnp.where(keep, q2, jnp.zeros_like(q2))
            s = _dot_nt(qg, kcat)
            if valid is not None:
                s = jnp.where(valid, s, NEG_BIG)
            sk = sink_ref[g * SWA_REP + c]
            m = jnp.maximum(jnp.max(s, axis=-1, keepdims=True), sk)
            p = jnp.exp2(s - m)
            l = jnp.sum(p, axis=-1, keepdims=True) + jnp.exp2(sk - m)
            res.append(_dot(p.astype(jnp.bfloat16), vcat) / l)
        o_ref[:, c * LANES:(c + 1) * LANES] = jnp.where(lo_half, res[0], res[1]).astype(jnp.bfloat16)


def _swa_main_body(sink_ref, q_ref, k_ref, v_ref, o_ref, *, n_ctx, n_lat):
    band = TILE + 2 * SWA_WINDOW
    start = pl.program_id(1) * TILE
    ks = pl.multiple_of(jnp.clip(start - SWA_WINDOW, 0, n_lat - band), SWA_WINDOW)
    kcat = jnp.concatenate([k_ref[n_lat:n_lat + n_ctx, :], k_ref[pl.ds(ks, band), :]], axis=0)
    vcat = jnp.concatenate([v_ref[n_lat:n_lat + n_ctx, :], v_ref[pl.ds(ks, band), :]], axis=0)
    qpos = start + lax.broadcasted_iota(jnp.int32, (TILE, n_ctx + band), 0)
    col = lax.broadcasted_iota(jnp.int32, (TILE, n_ctx + band), 1)
    kpos = ks + col - n_ctx
    valid = (col < n_ctx) | (jnp.abs(qpos - kpos) <= SWA_WINDOW)
    _swa_attend(sink_ref, q_ref, kcat, vcat, valid, o_ref)


def _swa_ctx_body(sink_ref, q_ref, k_ref, v_ref, o_ref):
    _swa_attend(sink_ref, q_ref, k_ref[...], v_ref[...], None, o_ref)


def _swa_attention(sink, qs, ks, vs, n_lat):
    b, t, w = qs.shape
    return pl.pallas_call(
        functools.partial(_swa_main_body, n_ctx=t - n_lat, n_lat=n_lat),
        grid=(b, n_lat // TILE),
        in_specs=[
            pl.BlockSpec(memory_space=pltpu.SMEM),
            pl.BlockSpec((None, TILE, w), lambda i, j: (i, j, 0)),
            pl.BlockSpec((None, t, LANES), lambda i, j: (i, 0, 0)),
            pl.BlockSpec((None, t, LANES), lambda i, j: (i, 0, 0)),
        ],
        out_specs=pl.BlockSpec((None, TILE, w), lambda i, j: (i, j, 0)),
        out_shape=jax.ShapeDtypeStruct((b, n_lat, w), jnp.bfloat16),
        compiler_params=_cparams(2),
        name="swa_attention",
    )(sink, qs, ks, vs)


def _swa_attention_ctx(sink, qs, ks, vs, n_lat):
    b, t, w = qs.shape
    n_ctx = t - n_lat
    blk = n_lat // n_ctx
    return pl.pallas_call(
        _swa_ctx_body,
        grid=(b,),
        in_specs=[
            pl.BlockSpec(memory_space=pltpu.SMEM),
            pl.BlockSpec((None, n_ctx, w), lambda i: (i, blk, 0)),
            pl.BlockSpec((None, n_ctx, LANES), lambda i: (i, blk, 0)),
            pl.BlockSpec((None, n_ctx, LANES), lambda i: (i, blk, 0)),
        ],
        out_specs=pl.BlockSpec((None, n_ctx, w), lambda i: (i, 0, 0)),
        out_shape=jax.ShapeDtypeStruct((b, n_ctx, w), jnp.bfloat16),
        compiler_params=_cparams(1),
        name="swa_attention_ctx",
    )(sink, qs, ks, vs)


HALO = 16


def _mix_body(a_ref, b_ref, cz_ref, hp_ref, hn_ref, x_ref, mod_ref, cw_ref, wo_ref, n2_ref, rw_ref,
              x1_ref, aff_ref):
    j = pl.program_id(1)
    d = D_MODEL
    first = j == 0
    last = j == pl.num_programs(1) - 1
    z = cz_ref[:, CONV_CH:2 * CONV_CH].astype(jnp.float32)
    zp = hp_ref[HALO - 1:HALO, CONV_CH:2 * CONV_CH].astype(jnp.float32)
    zn = hn_ref[0:1, CONV_CH:2 * CONV_CH].astype(jnp.float32)
    zp = jnp.where(first, jnp.zeros_like(zp), zp)
    zn = jnp.where(last, jnp.zeros_like(zn), zn)
    row = lax.broadcasted_iota(jnp.int32, z.shape, 0)
    z_dn = jnp.where(row == 0, zp, pltpu.roll(z, 1, 0))
    z_up = jnp.where(row == TILE - 1, zn, pltpu.roll(z, TILE - 1, 0))
    y = z_dn * cw_ref[0:1, :] + z * cw_ref[1:2, :] + z_up * cw_ref[2:3, :]
    cv = (cz_ref[:, 0:CONV_CH].astype(jnp.float32) * y).astype(jnp.bfloat16)
    mix = jnp.concatenate([a_ref[...], b_ref[...], cv], axis=-1)
    g1 = mod_ref[:, 2 * d:3 * d]
    sh2 = mod_ref[:, 3 * d:4 * d]
    sc2 = mod_ref[:, 4 * d:5 * d]
    x1 = x_ref[...] + g1 * _dot(mix, wo_ref[...])
    x1_ref[...] = x1
    h2 = (_rms(x1, n2_ref[...]) * (1.0 + sc2) + sh2).astype(jnp.bfloat16)
    lg = _dot_nt(rw_ref[...], h2)
    ex = jnp.exp(lg - jnp.max(lg, axis=0, keepdims=True))
    aff_ref[...] = ex / jnp.sum(ex, axis=0, keepdims=True)


def _mixer_out(a, bsw, cz, x, mod_l, mod_row0, lw, frame_tile0):
    b, n, d = x.shape
    t = cz.shape[1]
    hb = TILE // HALO
    here = lambda i, j: (i, j, 0)
    const = lambda i, j: (0, 0)
    if mod_row0 is None:
        mod_map = lambda i, j: (i, 0, 0)
    else:
        mod_map = lambda i, j: (mod_row0, 0, 0)
    return pl.pallas_call(
        _mix_body,
        grid=(b, n // TILE),
        in_specs=[
            pl.BlockSpec((None, TILE, a.shape[2]), here),
            pl.BlockSpec((None, TILE, bsw.shape[2]), here),
            pl.BlockSpec((None, TILE, cz.shape[2]), lambda i, j: (i, j + frame_tile0, 0)),
            pl.BlockSpec((None, HALO, cz.shape[2]),
                         lambda i, j: (i, jnp.maximum((j + frame_tile0) * hb - 1, 0), 0)),
            pl.BlockSpec((None, HALO, cz.shape[2]),
                         lambda i, j: (i, jnp.minimum((j + frame_tile0 + 1) * hb, t // HALO - 1), 0)),
            pl.BlockSpec((None, TILE, d), here),
            pl.BlockSpec((None, 1, N_MOD * d), mod_map),
            pl.BlockSpec((3, CONV_CH), const),
            pl.BlockSpec((d, d), const),
            pl.BlockSpec((1, d), const),
            pl.BlockSpec((N_EXPERTS, d), const),
        ],
        out_specs=[pl.BlockSpec((None, TILE, d), here),
                   pl.BlockSpec((None, N_EXPERTS, TILE), lambda i, j: (i, 0, j))],
        out_shape=[jax.ShapeDtypeStruct((b, n, d), jnp.float32),
                   jax.ShapeDtypeStruct((b, N_EXPERTS, n), jnp.float32)],
        compiler_params=_cparams(2),
        name="mixer_out",
    )(a, bsw, cz, cz, cz, x, mod_l, lw["cw"], lw["wo"], lw["n2"], lw["rw"])


ROUTE_ROWS = 128
ROUTE_SLOTS = 64


def _route_body(aff_ref, tri_ref, idx_ref, taff_ref, pos_ref, oi_ref, oa_ref, *, cap):
    rows, n = aff_ref.shape
    aff = aff_ref[...]
    bits = pltpu.bitcast(aff, jnp.int32)

    def count(mask):
        return jnp.sum(jnp.where(mask, 1.0, 0.0), axis=1, keepdims=True)

    def ones(mask):
        return jnp.where(mask, 1.0, 0.0).astype(jnp.bfloat16)

    def search(i, thr):
        cand = thr | (1 << (30 - i))
        return jnp.where(count(bits >= cand) >= cap, cand, thr)

    thr = lax.fori_loop(0, 31, search, jnp.zeros((rows, 1), jnp.int32))
    gt = bits > thr
    eq = bits == thr
    need = cap - count(gt)
    peq = _dot(ones(eq), tri_ref[...])
    sel = gt | (eq & (peq < need))
    pos = _dot(ones(sel), tri_ref[...])
    pos_ref[...] = jnp.where(sel, pos, -1.0)

    n_sweep = min(ROUTE_SLOTS, cap)
    tok = lax.broadcasted_iota(jnp.int32, (n_sweep, n), 1).astype(jnp.float32)
    lane = lax.broadcasted_iota(jnp.int32, (cap, rows), 1)
    oi_ref[...] = jnp.zeros_like(oi_ref)
    oa_ref[...] = jnp.zeros_like(oa_ref)

    def per_row(r, carry):
        prow = pos_ref[pl.ds(r, 1), :]
        arow = aff_ref[pl.ds(r, 1), :]
        icols = []
        acols = []
        for j0 in range(0, cap, n_sweep):
            slot = (lax.broadcasted_iota(jnp.int32, (n_sweep, 1), 0) + j0).astype(jnp.float32)
            hit = prow == slot
            icols.append(jnp.sum(jnp.where(hit, tok, 0.0), axis=1, keepdims=True))
            acols.append(jnp.sum(jnp.where(hit, arow, 0.0), axis=1, keepdims=True))
        icol = jnp.concatenate(icols, axis=0) if len(icols) > 1 else icols[0]
        acol = jnp.concatenate(acols, axis=0) if len(acols) > 1 else acols[0]
        oi_ref[...] = jnp.where(lane == r, icol, oi_ref[...])
        oa_ref[...] = jnp.where(lane == r, acol, oa_ref[...])
        return carry

    lax.fori_loop(0, rows, per_row, 0)
    idx_ref[...] = jnp.transpose(oi_ref[...]).astype(jnp.int32)
    taff_ref[...] = jnp.transpose(oa_ref[...])


def _routing(aff, tri, cap):
    b, e, n = aff.shape
    rows = b * e
    rr = min(ROUTE_ROWS, rows)
    idx, taff = pl.pallas_call(
        functools.partial(_route_body, cap=cap),
        grid=(rows // rr,),
        in_specs=[pl.BlockSpec((rr, n), lambda i: (i, 0)),
                  pl.BlockSpec((n, n), lambda i: (0, 0))],
        out_specs=[pl.BlockSpec((rr, cap), lambda i: (i, 0)),
                   pl.BlockSpec((rr, cap), lambda i: (i, 0))],
        out_shape=[jax.ShapeDtypeStruct((rows, cap), jnp.int32),
                   jax.ShapeDtypeStruct((rows, cap), jnp.float32)],
        scratch_shapes=[pltpu.VMEM((rr, n), jnp.float32),
                        pltpu.VMEM((cap, rr), jnp.float32),
                        pltpu.VMEM((cap, rr), jnp.float32)],
        compiler_params=_cparams(1),
        name="routing",
    )(aff.reshape(rows, n), tri)
    return idx.reshape(b, e, cap), taff.reshape(b, e, cap)


EXP_PER_STEP = 2
MOE_CHUNK = 512
SCATTER_GROUP = 16


def _moe_body(idx_ref, taff_ref, x_ref, mod_ref, n2_ref, fg_ref, wg_ref, wu_ref, wd_ref, o_ref,
              h_ref, acc_ref, *bufs, nch, cap, final):
    j = pl.program_id(1)
    d = D_MODEL
    nsl = d // LANES
    rows = MOE_CHUNK * nsl
    nes = N_EXPERTS // EXP_PER_STEP
    xg_refs = bufs[:EXP_PER_STEP]
    yb_refs = bufs[EXP_PER_STEP:]

    @pl.when(j < nch)
    def _():
        sh2 = mod_ref[:, 3 * d:4 * d]
        sc2 = mod_ref[:, 4 * d:5 * d]
        h2 = _rms(x_ref[...], n2_ref[...]) * (1.0 + sc2) + sh2
        base = pl.multiple_of(j * rows, rows)
        for s in range(nsl):
            h_ref[pl.ds(base + s, MOE_CHUNK, stride=nsl), :] = h2[:, s * LANES:(s + 1) * LANES]
        acc_ref[pl.ds(base, rows), :] = jnp.zeros((rows, LANES), jnp.float32)

    @pl.when((j >= nch) & (j < nch + nes))
    def _():
        for k in range(EXP_PER_STEP):
            xg_ref, yb_ref = xg_refs[k], yb_refs[k]
            tok = [pl.multiple_of(idx_ref[0, k * cap + jj] * nsl, nsl) for jj in range(cap)]
            for jj in range(cap):
                xg_ref[jj * nsl:(jj + 1) * nsl, :] = h_ref[pl.ds(tok[jj], nsl), :]
            xs = jnp.concatenate([xg_ref[pl.ds(s, cap, stride=nsl), :] for s in range(nsl)], axis=-1)
            xs = xs.astype(jnp.bfloat16)
            gate = _dot(xs, wg_ref[k])
            up = _dot(xs, wu_ref[k])
            hid = (_silu(gate) * up).astype(jnp.bfloat16)
            y = _dot(hid, wd_ref[k])
            for s in range(nsl):
                yb_ref[pl.ds(s, cap, stride=nsl), :] = y[:, s * LANES:(s + 1) * LANES]
            for g0 in range(0, cap, SCATTER_GROUP):
                grp = range(g0, g0 + SCATTER_GROUP)
                vals = [acc_ref[pl.ds(tok[jj], nsl), :]
                        + taff_ref[0, k * cap + jj] * yb_ref[jj * nsl:(jj + 1) * nsl, :] for jj in grp]
                for jj, v in zip(grp, vals):
                    acc_ref[pl.ds(tok[jj], nsl), :] = v

    @pl.when(j >= nch + nes)
    def _():
        c = j - nch - nes
        base = pl.multiple_of(c * rows, rows)
        m = jnp.concatenate([acc_ref[pl.ds(base + s, MOE_CHUNK, stride=nsl), :] for s in range(nsl)], axis=-1)
        g2 = mod_ref[:, 5 * d:6 * d]
        x2 = x_ref[...] + g2 * m
        if final:
            x2 = _rms(x2, fg_ref[...])
        o_ref[...] = x2


def _experts(x1, idx, taff, mod_l, mod_row0, lw, final_g, final):
    nb, n_tok, d = x1.shape
    cap = idx.shape[2]
    assert n_tok % MOE_CHUNK == 0 and cap % SCATTER_GROUP == 0
    nch = n_tok // MOE_CHUNK
    nes = N_EXPERTS // EXP_PER_STEP
    steps = 2 * nch + nes
    nsl = d // LANES
    idx = idx.reshape(nb, nes, 1, EXP_PER_STEP * cap)
    taff = taff.reshape(nb, nes, 1, EXP_PER_STEP * cap)

    def chunk_in(i, j):
        c = jnp.where(j < nch, j, jnp.where(j < nch + nes, nch - 1, j - nch - nes))
        return (i, c, 0)

    def chunk_out(i, j):
        return (i, jnp.clip(j - nch - nes, 0, nch - 1), 0)

    def slots(i, j):
        return (i, jnp.clip(j - nch, 0, nes - 1), 0, 0)

    def expert(i, j):
        return (jnp.clip(j - nch, 0, nes - 1), 0, 0)

    const = lambda i, j: (0, 0)
    if mod_row0 is None:
        mod_map = lambda i, j: (i, 0, 0)
    else:
        mod_map = lambda i, j: (mod_row0, 0, 0)
    slot_buf = pltpu.VMEM((cap * nsl, LANES), jnp.float32)
    return pl.pallas_call(
        functools.partial(_moe_body, nch=nch, cap=cap, final=final),
        grid=(nb, steps),
        in_specs=[
            pl.BlockSpec((None, None, 1, EXP_PER_STEP * cap), slots, memory_space=pltpu.SMEM),
            pl.BlockSpec((None, None, 1, EXP_PER_STEP * cap), slots, memory_space=pltpu.SMEM),
            pl.BlockSpec((None, MOE_CHUNK, d), chunk_in),
            pl.BlockSpec((None, 1, N_MOD * d), mod_map),
            pl.BlockSpec((1, d), const),
            pl.BlockSpec((1, d), const),
            pl.BlockSpec((EXP_PER_STEP, d, EXPERT_FF), expert),
            pl.BlockSpec((EXP_PER_STEP, d, EXPERT_FF), expert),
            pl.BlockSpec((EXP_PER_STEP, EXPERT_FF, d), expert),
        ],
        out_specs=pl.BlockSpec((None, MOE_CHUNK, d), chunk_out),
        out_shape=jax.ShapeDtypeStruct((nb, n_tok, d), jnp.float32),
        scratch_shapes=[pltpu.VMEM((n_tok * nsl, LANES), jnp.float32),
                        pltpu.VMEM((n_tok * nsl, LANES), jnp.float32)] + [slot_buf] * (2 * EXP_PER_STEP),
        compiler_params=_cparams(2),
        name="experts",
    )(idx, taff, x1, mod_l, lw["n2"], final_g, lw["wg"], lw["wu"], lw["wd"])


def _group_samples(x1, idx, taff):
    b, n_tok, d = x1.shape
    cap = idx.shape[2]
    g = max(k for k in range(1, 9) if b % k == 0 and (k * n_tok) % MOE_CHUNK == 0)
    off = (jnp.arange(b, dtype=jnp.int32) % g) * n_tok
    idx = idx + off[:, None, None]

    def merge(t):
        t = t.reshape(b // g, g, N_EXPERTS, cap)
        return jnp.swapaxes(t, 1, 2).reshape(b // g, N_EXPERTS, g * cap)

    return x1.reshape(b // g, g * n_tok, d), merge(idx), merge(taff)


def _swap_halves(w):
    half = w.shape[-1] // 2
    return jnp.concatenate([-w[..., half:], w[..., :half]], axis=-1)


def _swa_perm():
    cols = []
    for c in range(SWA_REP):
        for g in range(SWA_KV_HEADS):
            h = g * SWA_REP + c
            cols.extend(range(h * SWA_HEAD_DIM, (h + 1) * SWA_HEAD_DIM))
    return np.asarray(cols, np.int32)


def _layer_weights(w_in, q_g, w_uq, kv_g, w_ukv, conv_w, w_o, n1, n2, router_w, wg, wu, wd):
    bf = jnp.bfloat16
    d = w_in.shape[0]
    offs = np.cumsum([0, MLA_Q_RANK, MLA_KV_RANK, MLA_ROPE, SWA_HEADS * SWA_HEAD_DIM,
                      SWA_KV_HEADS * SWA_HEAD_DIM, SWA_KV_HEADS * SWA_HEAD_DIM, CONV_CH, CONV_CH, CONV_CH])
    part = [w_in[:, offs[i]:offs[i + 1]] for i in range(9)]
    w_cq, w_ckv, w_kr, w_sq, w_sk, w_sv, w_cb, w_cc, w_cu = part
    perm = _swa_perm()

    def per_head_swap(w, nh):
        wh = w.reshape(d, nh, -1)
        return _swap_halves(wh).reshape(d, -1)

    kr_block = jnp.concatenate([w_kr, _swap_halves(w_kr), jnp.zeros((d, LANES - 2 * MLA_ROPE), w_in.dtype)], axis=1)
    win = jnp.concatenate([
        w_cq, w_ckv, kr_block,
        w_sq[:, perm], per_head_swap(w_sq, SWA_HEADS)[:, perm],
        w_sk, per_head_swap(w_sk, SWA_KV_HEADS), w_sv, w_cb, w_cc, w_cu], axis=1).astype(bf)

    qd = MLA_NOPE + MLA_ROPE
    uq = w_uq.reshape(MLA_Q_RANK, MLA_HEADS, qd)
    zpad = jnp.zeros((MLA_Q_RANK, MLA_HEADS, HEAD_PAD - qd), w_uq.dtype)
    main = jnp.concatenate([uq, zpad], axis=-1)
    swp = jnp.concatenate([jnp.zeros_like(uq[..., :MLA_NOPE]), _swap_halves(uq[..., MLA_NOPE:]), zpad], axis=-1)
    wuq = jnp.concatenate([main.reshape(MLA_Q_RANK, -1), swp.reshape(MLA_Q_RANK, -1)], axis=1).astype(bf)

    ukv = w_ukv.reshape(MLA_KV_RANK, MLA_HEADS, MLA_NOPE + MLA_V)
    wk = jnp.concatenate([ukv[..., :MLA_NOPE],
                          jnp.zeros((MLA_KV_RANK, MLA_HEADS, HEAD_PAD - MLA_NOPE), w_ukv.dtype)], axis=-1)
    wk = wk.reshape(MLA_KV_RANK, -1).astype(bf)
    wv = ukv[..., MLA_NOPE:].reshape(MLA_KV_RANK, -1).astype(bf)

    e = np.zeros((LANES, MLA_HEADS * HEAD_PAD), np.float32)
    for h in range(MLA_HEADS):
        for l in range(MLA_ROPE):
            e[l, h * HEAD_PAD + MLA_NOPE + l] = 1.0
            e[MLA_ROPE + l, h * HEAD_PAD + MLA_NOPE + l] = 1.0

    n_mla = MLA_HEADS * MLA_V
    n_swa = SWA_HEADS * SWA_HEAD_DIM
    wo = jnp.concatenate([w_o[:n_mla], w_o[n_mla:n_mla + n_swa][perm], w_o[n_mla + n_swa:]], axis=0).astype(bf)
    return dict(
        g1=n1.reshape(1, -1), win=win, qg=q_g.reshape(1, -1), wuq=wuq, kvg=kv_g.reshape(1, -1),
        wk=wk, wvt=wv.T, e=jnp.asarray(e, bf), cw=conv_w, wo=wo, n2=n2.reshape(1, -1),
        rw=router_w.T.astype(bf), wg=wg.astype(bf), wu=wu.astype(bf), wd=wd.astype(bf))


def _axial_tables(n_tokens, rot_dim):
    rows = n_tokens // GRID_W
    row = jnp.repeat(jnp.arange(rows, dtype=jnp.float32), GRID_W)
    col = jnp.tile(jnp.arange(GRID_W, dtype=jnp.float32), rows)
    n_freq = rot_dim // 4
    inv = ROPE_BASE ** (-jnp.arange(n_freq, dtype=jnp.float32) / n_freq)
    ang = jnp.concatenate([row[:, None] * inv, col[:, None] * inv], axis=-1)
    return jnp.cos(ang), jnp.sin(ang)


def _row_table(n_ctx, n_lat):
    def with_ctx(cos, sin):
        one = jnp.ones((n_ctx, cos.shape[1]), jnp.float32)
        return (jnp.concatenate([cos, one], axis=0), jnp.concatenate([sin, 0.0 * one], axis=0))

    cm, sm = with_ctx(*_axial_tables(n_lat, MLA_ROPE))
    cs, ss = with_ctx(*_axial_tables(n_lat, SWA_HEAD_DIM))
    t = n_ctx + n_lat
    cm2 = jnp.concatenate([cm, cm], axis=1)
    sm2 = jnp.concatenate([sm, sm], axis=1)
    cs2 = jnp.concatenate([cs, cs], axis=1)
    ss2 = jnp.concatenate([ss, ss], axis=1)
    pad = jnp.zeros((t, HEAD_PAD - MLA_NOPE - MLA_ROPE), jnp.float32)
    cq = jnp.concatenate([jnp.ones((t, MLA_NOPE), jnp.float32), cm2, pad], axis=1) * (MLA_SCALE * LOG2E)
    sq = jnp.concatenate([jnp.zeros((t, MLA_NOPE), jnp.float32), sm2, pad], axis=1) * (MLA_SCALE * LOG2E)
    kr = jnp.concatenate([cm2, sm2, jnp.zeros((t, LANES - 2 * MLA_ROPE), jnp.float32)], axis=1)
    tab = jnp.concatenate([
        jnp.tile(cq, (1, MLA_HEADS)), jnp.tile(sq, (1, MLA_HEADS)), kr,
        jnp.tile(cs2, (1, SWA_HEADS)) * (SWA_SCALE * LOG2E), jnp.tile(ss2, (1, SWA_HEADS)) * (SWA_SCALE * LOG2E),
        jnp.tile(cs2, (1, SWA_KV_HEADS)), jnp.tile(ss2, (1, SWA_KV_HEADS))], axis=1)
    return tab


def _strict_upper(n):
    r = lax.broadcasted_iota(jnp.int32, (n, n), 0)
    c = lax.broadcasted_iota(jnp.int32, (n, n), 1)
    return (r < c).astype(jnp.bfloat16)


def kernel(x, c, ctx, c_ctx, ada_w, ada_b, norm1_g, w_in, mla_q_norm_g, mla_w_uq, mla_kv_norm_g, mla_w_ukv,
           swa_sink, conv_w, w_o, norm2_g, router_w, exp_w_gate, exp_w_up, exp_w_down, final_norm_g):
    b, s, d = x.shape
    n_ctx = ctx.shape[1]
    depth = ada_w.shape[0]
    assert d == D_MODEL and s % MLA_TQ == 0 and n_ctx == TILE and b + 1 <= MOD_ROWS
    cap_lat = EC_CAPACITY_FACTOR * s // N_EXPERTS
    cap_ctx = EC_CAPACITY_FACTOR * n_ctx // N_EXPERTS

    cc = jnp.concatenate([c, c_ctx[None, :], jnp.zeros((MOD_ROWS - b - 1, d), c.dtype)], axis=0)
    mod = _modulation(cc, ada_w, ada_b).reshape(depth, MOD_ROWS, 1, N_MOD * d)
    tab = _row_table(n_ctx, s)
    tri_lat = _strict_upper(s)
    tri_ctx = _strict_upper(n_ctx)
    sink_slots = jnp.zeros((8,), jnp.float32)
    fg = final_norm_g.reshape(1, d)

    xl, xc = x, ctx
    for li in range(depth):
        last = li == depth - 1
        lw = _layer_weights(w_in[li], mla_q_norm_g[li], mla_w_uq[li], mla_kv_norm_g[li], mla_w_ukv[li],
                            conv_w[li], w_o[li], norm1_g[li], norm2_g[li], router_w[li],
                            exp_w_gate[li], exp_w_up[li], exp_w_down[li])
        mod_l = mod[li]
        sink = sink_slots.at[:SWA_HEADS].set(swa_sink[li] * LOG2E)
        qm, km, vmt, qs, ks, vs, cz = _projections(xl, xc, mod_l, lw, tab)
        a = _mla_attention(qm, km, vmt, s)
        bsw = _swa_attention(sink, qs, ks, vs, s)
        x1, aff = _mixer_out(a, bsw, cz, xl, mod_l, None, lw, 0)
        idx, taff = _routing(aff, tri_lat, cap_lat)
        if not last:
            ac = _mla_attention_ctx(qm, km, vmt, s)
            bc = _swa_attention_ctx(sink, qs, ks, vs, s)
            xc1, affc = _mixer_out(ac, bc, cz, xc, mod_l, b, lw, s // TILE)
            idxc, taffc = _routing(affc, tri_ctx, cap_ctx)
            xc = _experts(*_group_samples(xc1, idxc, taffc), mod_l, b, lw, fg, final=False).reshape(xc1.shape)
        xl = _experts(x1, idx, taff, mod_l, None, lw, fg, final=last)
    return xl
```

```python
import functools

import jax
import jax.numpy as jnp
import numpy as np
from jax import lax
from jax.experimental import pallas as pl
from jax.experimental.pallas import tpu as pltpu

D_MODEL = 1024
GRID_W = 64
NORM_EPS = 1e-6
ROPE_BASE = 10000.0

MLA_HEADS = 6
MLA_Q_RANK = 256
MLA_KV_RANK = 128
MLA_NOPE = 64
MLA_ROPE = 32
MLA_V = 64
MLA_SCALE = (MLA_NOPE + MLA_ROPE) ** -0.5

SWA_HEADS = 6
SWA_KV_HEADS = 2
SWA_REP = SWA_HEADS // SWA_KV_HEADS
SWA_HEAD_DIM = 64
SWA_WINDOW = 128
SWA_SCALE = SWA_HEAD_DIM ** -0.5

CONV_CH = 256
N_EXPERTS = 16
EXPERT_FF = 512
EC_CAPACITY_FACTOR = 2
N_MOD = 6

LANES = 128
SUBLANES = 8
TILE = 256
HEAD_PAD = 128
MOD_ROWS = 40
NEG_BIG = -1e30
LOG2E = 1.4426950408889634

_C_CQ = 0
_C_CKV = 256
_C_KR = 384
_C_SQ = 512
_C_SQSW = 896
_C_SK = 1280
_C_SKSW = 1408
_C_SV = 1536
_C_CB = 1664
_C_CC = 1920
_C_CU = 2176
IN_W = 2432

_T_CQ = 0
_T_SQ = 768
_T_KR = 1536
_T_CS = 1664
_T_SS = 2048
_T_CK = 2432
_T_SK = 2560
TAB_W = 2688

_VMEM_LIMIT = 56 * 1024 * 1024


def _cparams(n_grid):
    return pltpu.CompilerParams(dimension_semantics=("arbitrary",) * n_grid,
                                vmem_limit_bytes=_VMEM_LIMIT)


def _silu(v):
    return v * (1.0 / (1.0 + jnp.exp(-v)))


def _rms(v, g):
    return v * lax.rsqrt(jnp.mean(v * v, axis=-1, keepdims=True) + NORM_EPS) * g


def _dot(a, b):
    return jnp.dot(a, b, preferred_element_type=jnp.float32)


def _dot_nt(a, b):
    return lax.dot_general(a, b, (((1,), (1,)), ((), ())), preferred_element_type=jnp.float32)


def _mod_body(c_ref, w_ref, b_ref, o_ref):
    a = _silu(c_ref[...]).astype(jnp.bfloat16)
    o_ref[...] = _dot(a, w_ref[...].astype(jnp.bfloat16)) + b_ref[...]


def _modulation(cc, ada_w, ada_b):
    depth, d, n = ada_w.shape
    bn = 512
    return pl.pallas_call(
        _mod_body,
        grid=(depth, n // bn),
        in_specs=[
            pl.BlockSpec((MOD_ROWS, d), lambda l, i: (0, 0)),
            pl.BlockSpec((None, d, bn), lambda l, i: (l, 0, i)),
            pl.BlockSpec((None, 1, bn), lambda l, i: (l, 0, i)),
        ],
        out_specs=pl.BlockSpec((None, MOD_ROWS, bn), lambda l, i: (l, 0, i)),
        out_shape=jax.ShapeDtypeStruct((depth, MOD_ROWS, n), jnp.float32),
        compiler_params=_cparams(2),
        name="modulation",
    )(cc, ada_w, ada_b.reshape(depth, 1, n))


def _proj_body(x_ref, ctx_ref, mod_ref, g1_ref, win_ref, qg_ref, wuq_ref, kvg_ref, wk_ref, wvt_ref,
               e_ref, tab_ref, qm_ref, km_ref, vmt_ref, qs_ref, ks_ref, vs_ref, cz_ref, xs_ref):
    j = pl.program_id(0)
    is_ctx = j == pl.num_programs(0) - 1

    @pl.when(is_ctx)
    def _():
        xs_ref[...] = ctx_ref[...]

    @pl.when(jnp.logical_not(is_ctx))
    def _():
        xs_ref[...] = x_ref[...]

    d = D_MODEL
    sh1 = mod_ref[:, 0:d]
    sc1 = mod_ref[:, d:2 * d]
    h = (_rms(xs_ref[...], g1_ref[...]) * (1.0 + sc1) + sh1).astype(jnp.bfloat16)

    def proj(lo, hi):
        return _dot(h, win_ref[:, lo:hi])

    def tab(lo, n):
        return tab_ref[:, lo:lo + n]

    cq = _rms(proj(_C_CQ, _C_CKV), qg_ref[...]).astype(jnp.bfloat16)
    uq = _dot(cq, wuq_ref[...])
    nq = MLA_HEADS * HEAD_PAD
    qm_ref[...] = (uq[:, :nq] * tab(_T_CQ, nq) + uq[:, nq:] * tab(_T_SQ, nq)).astype(jnp.bfloat16)

    ckv = _rms(proj(_C_CKV, _C_KR), kvg_ref[...]).astype(jnp.bfloat16)
    krp = (proj(_C_KR, _C_SQ) * tab(_T_KR, LANES)).astype(jnp.bfloat16)
    km_ref[...] = (_dot(ckv, wk_ref[...]) + _dot(krp, e_ref[...])).astype(jnp.bfloat16)
    vmt_ref[...] = _dot_nt(wvt_ref[...], ckv).astype(jnp.bfloat16)

    nsq = SWA_HEADS * SWA_HEAD_DIM
    qs_ref[...] = (proj(_C_SQ, _C_SQSW) * tab(_T_CS, nsq)
                   + proj(_C_SQSW, _C_SK) * tab(_T_SS, nsq)).astype(jnp.bfloat16)
    ks_ref[...] = (proj(_C_SK, _C_SKSW) * tab(_T_CK, LANES)
                   + proj(_C_SKSW, _C_SV) * tab(_T_SK, LANES)).astype(jnp.bfloat16)
    vs_ref[...] = proj(_C_SV, _C_CB).astype(jnp.bfloat16)

    cz_ref[:, 0:CONV_CH] = proj(_C_CB, _C_CC).astype(jnp.bfloat16)
    cz_ref[:, CONV_CH:2 * CONV_CH] = (proj(_C_CC, _C_CU) * proj(_C_CU, IN_W)).astype(jnp.bfloat16)


def _projections(x, ctx, mod_l, lw, tab):
    b, s, d = x.shape
    n_ctx = ctx.shape[1]
    t = n_ctx + s
    nl = s // TILE
    const = lambda j, i: (0, 0)
    row_outs = [MLA_HEADS * HEAD_PAD, MLA_HEADS * HEAD_PAD, None, SWA_HEADS * SWA_HEAD_DIM, LANES, LANES,
                2 * CONV_CH]
    nv = MLA_HEADS * MLA_V
    out_specs = [pl.BlockSpec((None, nv, TILE), lambda j, i: (i, 0, j)) if w is None
                 else pl.BlockSpec((None, TILE, w), lambda j, i: (i, j, 0)) for w in row_outs]
    out_shape = [jax.ShapeDtypeStruct((b, nv, t) if w is None else (b, t, w), jnp.bfloat16) for w in row_outs]
    return pl.pallas_call(
        _proj_body,
        grid=(nl + 1, b),
        in_specs=[
            pl.BlockSpec((None, TILE, d), lambda j, i: (jnp.where(j == nl, 0, i), jnp.minimum(j, nl - 1), 0)),
            pl.BlockSpec((None, TILE, d), lambda j, i: (jnp.where(j == nl, i, 0), 0, 0)),
            pl.BlockSpec((None, 1, N_MOD * d), lambda j, i: (jnp.where(j == nl, b, i), 0, 0)),
            pl.BlockSpec((1, d), const),
            pl.BlockSpec((d, IN_W), const),
            pl.BlockSpec((1, MLA_Q_RANK), const),
            pl.BlockSpec((MLA_Q_RANK, 2 * MLA_HEADS * HEAD_PAD), const),
            pl.BlockSpec((1, MLA_KV_RANK), const),
            pl.BlockSpec((MLA_KV_RANK, MLA_HEADS * HEAD_PAD), const),
            pl.BlockSpec((nv, MLA_KV_RANK), const),
            pl.BlockSpec((LANES, MLA_HEADS * HEAD_PAD), const),
            pl.BlockSpec((TILE, TAB_W), lambda j, i: (j, 0)),
        ],
        out_specs=out_specs,
        out_shape=out_shape,
        scratch_shapes=[pltpu.VMEM((TILE, d), jnp.float32)],
        compiler_params=_cparams(2),
        name="projections",
    )(x, ctx, mod_l, lw["g1"], lw["win"], lw["qg"], lw["wuq"], lw["kvg"], lw["wk"], lw["wvt"], lw["e"], tab)


MLA_TQ = 2048


def _mla_pair(q_ref, k_ref, vt_ref, o_ref, c):
    tq = q_ref.shape[0]
    res = []
    for hh in range(2):
        lo = (2 * c + hh) * HEAD_PAD
        st = _dot_nt(k_ref[:, lo:lo + HEAD_PAD], q_ref[:, lo:lo + HEAD_PAD])
        m = jnp.max(st, axis=0, keepdims=True)
        p = jnp.exp2(st - m)
        l = jnp.sum(p, axis=0, keepdims=True)
        ot = _dot(vt_ref[c * LANES:(c + 1) * LANES, :], p.astype(jnp.bfloat16))
        res.append(ot / l)
    row = lax.broadcasted_iota(jnp.int32, (LANES, tq), 0)
    o_ref[:, c * LANES:(c + 1) * LANES] = jnp.transpose(
        jnp.where(row < MLA_V, res[0], res[1])).astype(jnp.bfloat16)


def _mla_main_body(q_ref, k_ref, vt_ref, o_ref):
    for c in range(MLA_HEADS // 2):
        _mla_pair(q_ref, k_ref, vt_ref, o_ref, c)


def _mla_ctx_body(q_ref, k_ref, vt_ref, o_ref):
    for c in range(MLA_HEADS // 2):
        _mla_pair(q_ref, k_ref, vt_ref, o_ref, c)


def _mla_attention(qm, km, vmt, n_lat):
    b, t, _ = qm.shape
    return pl.pallas_call(
        _mla_main_body,
        grid=(b, n_lat // MLA_TQ),
        in_specs=[
            pl.BlockSpec((None, MLA_TQ, qm.shape[2]), lambda i, j: (i, j, 0)),
            pl.BlockSpec((None, t, km.shape[2]), lambda i, j: (i, 0, 0)),
            pl.BlockSpec((None, vmt.shape[1], t), lambda i, j: (i, 0, 0)),
        ],
        out_specs=pl.BlockSpec((None, MLA_TQ, vmt.shape[1]), lambda i, j: (i, j, 0)),
        out_shape=jax.ShapeDtypeStruct((b, n_lat, vmt.shape[1]), jnp.bfloat16),
        compiler_params=_cparams(2),
        name="mla_attention",
    )(qm, km, vmt)


def _mla_attention_ctx(qm, km, vmt, n_lat):
    b, t, _ = qm.shape
    n_ctx = t - n_lat
    blk = n_lat // n_ctx
    return pl.pallas_call(
        _mla_ctx_body,
        grid=(b,),
        in_specs=[
            pl.BlockSpec((None, n_ctx, qm.shape[2]), lambda i: (i, blk, 0)),
            pl.BlockSpec((None, n_ctx, km.shape[2]), lambda i: (i, blk, 0)),
            pl.BlockSpec((None, vmt.shape[1], n_ctx), lambda i: (i, 0, blk)),
        ],
        out_specs=pl.BlockSpec((None, n_ctx, vmt.shape[1]), lambda i: (i, 0, 0)),
        out_shape=jax.ShapeDtypeStruct((b, n_ctx, vmt.shape[1]), jnp.bfloat16),
        compiler_params=_cparams(1),
        name="mla_attention_ctx",
    )(qm, km, vmt)


def _swa_attend(sink_ref, q_ref, kcat, vcat, valid, o_ref):
    rows = q_ref.shape[0]
    lane = lax.broadcasted_iota(jnp.int32, (rows, LANES), 1)
    lo_half = lane < SWA_HEAD_DIM
    for c in range(SWA_REP):
        q2 = q_ref[:, c * LANES:(c + 1) * LANES]
        res = []
        for g in range(SWA_KV_HEADS):
            keep = lo_half if g == 0 else jnp.logical_not(lo_half)
            qg = jnp.where(keep, q2, jnp.zeros_like(q2))
            s = _dot_nt(qg, kcat)
            if valid is not None:
                s = jnp.where(valid, s, NEG_BIG)
            sk = sink_ref[g * SWA_REP + c]
            m = jnp.maximum(jnp.max(s, axis=-1, keepdims=True), sk)
            p = jnp.exp2(s - m)
            l = jnp.sum(p, axis=-1, keepdims=True) + jnp.exp2(sk - m)
            res.append(_dot(p.astype(jnp.bfloat16), vcat) / l)
        o_ref[:, c * LANES:(c + 1) * LANES] = jnp.where(lo_half, res[0], res[1]).astype(jnp.bfloat16)


SWA_BAND = TILE + 2 * SWA_WINDOW


def _swa_band_start(j, n_lat):
    return jnp.clip(j * TILE - SWA_WINDOW, 0, n_lat - SWA_BAND)


def _swa_main_body(sink_ref, q_ref, k_ref, v_ref, bias_ref, o_ref, *, n_ctx, n_lat):
    ks = pl.multiple_of(_swa_band_start(pl.program_id(1), n_lat), SWA_WINDOW)
    kcat = jnp.concatenate([k_ref[n_lat:n_lat + n_ctx, :], k_ref[pl.ds(ks, SWA_BAND), :]], axis=0)
    vcat = jnp.concatenate([v_ref[n_lat:n_lat + n_ctx, :], v_ref[pl.ds(ks, SWA_BAND), :]], axis=0)
    vt = jnp.transpose(vcat.astype(jnp.float32)).astype(jnp.bfloat16)
    bias = jnp.concatenate([bias_ref[...]] * SWA_REP, axis=1)
    lane = lax.broadcasted_iota(jnp.int32, (TILE, LANES), 1)
    lo_half = lane < SWA_HEAD_DIM
    qcol = lax.broadcasted_iota(jnp.int32, (1, SWA_REP * TILE), 1)
    res = []
    for g in range(SWA_KV_HEADS):
        keep = lo_half if g == 0 else jnp.logical_not(lo_half)
        qg = jnp.concatenate([jnp.where(keep, q_ref[:, c * LANES:(c + 1) * LANES], 0.0).astype(jnp.bfloat16)
                              for c in range(SWA_REP)], axis=0)
        sk = jnp.full((1, SWA_REP * TILE), sink_ref[g * SWA_REP + SWA_REP - 1], jnp.float32)
        for c in range(SWA_REP - 2, -1, -1):
            sk = jnp.where(qcol < (c + 1) * TILE, sink_ref[g * SWA_REP + c], sk)
        st = _dot_nt(kcat, qg) + bias
        m = jnp.maximum(jnp.max(st, axis=0, keepdims=True), sk)
        p = jnp.exp2(st - m)
        l = jnp.sum(p, axis=0, keepdims=True) + jnp.exp2(sk - m)
        res.append(_dot(vt, p.astype(jnp.bfloat16)) / l)
    row = lax.broadcasted_iota(jnp.int32, (LANES, TILE), 0)
    for c in range(SWA_REP):
        pair = jnp.where(row < SWA_HEAD_DIM, res[0][:, c * TILE:(c + 1) * TILE], res[1][:, c * TILE:(c + 1) * TILE])
        o_ref[:, c * LANES:(c + 1) * LANES] = jnp.transpose(pair).astype(jnp.bfloat16)


def _swa_bias(n_ctx, n_lat):
    nt = n_lat // TILE
    r = lax.broadcasted_iota(jnp.int32, (n_ctx + SWA_BAND, TILE), 0)
    q = lax.broadcasted_iota(jnp.int32, (n_ctx + SWA_BAND, TILE), 1)
    out = []
    for j in (0, 1, nt - 1):
        kpos = _swa_band_start(j, n_lat) + r - n_ctx
        valid = (r < n_ctx) | (jnp.abs(j * TILE + q - kpos) <= SWA_WINDOW)
        out.append(jnp.where(valid, 0.0, NEG_BIG).astype(jnp.float32))
    return jnp.stack(out)


def _swa_ctx_body(sink_ref, q_ref, k_ref, v_ref, o_ref):
    _swa_attend(sink_ref, q_ref, k_ref[...], v_ref[...], None, o_ref)


def _swa_attention(sink, qs, ks, vs, bias, n_lat):
    b, t, w = qs.shape
    nt = n_lat // TILE
    assert nt >= 3
    return pl.pallas_call(
        functools.partial(_swa_main_body, n_ctx=t - n_lat, n_lat=n_lat),
        grid=(b, nt),
        in_specs=[
            pl.BlockSpec(memory_space=pltpu.SMEM),
            pl.BlockSpec((None, TILE, w), lambda i, j: (i, j, 0)),
            pl.BlockSpec((None, t, LANES), lambda i, j: (i, 0, 0)),
            pl.BlockSpec((None, t, LANES), lambda i, j: (i, 0, 0)),
            pl.BlockSpec((None,) + bias.shape[1:], lambda i, j: (jnp.where(j == 0, 0, jnp.where(j == nt - 1, 2, 1)), 0, 0)),
        ],
        out_specs=pl.BlockSpec((None, TILE, w), lambda i, j: (i, j, 0)),
        out_shape=jax.ShapeDtypeStruct((b, n_lat, w), jnp.bfloat16),
        compiler_params=_cparams(2),
        name="swa_attention",
    )(sink, qs, ks, vs, bias)


def _swa_attention_ctx(sink, qs, ks, vs, n_lat):
    b, t, w = qs.shape
    n_ctx = t - n_lat
    blk = n_lat // n_ctx
    return pl.pallas_call(
        _swa_ctx_body,
        grid=(b,),
        in_specs=[
            pl.BlockSpec(memory_space=pltpu.SMEM),
            pl.BlockSpec((None, n_ctx, w), lambda i: (i, blk, 0)),
            pl.BlockSpec((None, n_ctx, LANES), lambda i: (i, blk, 0)),
            pl.BlockSpec((None, n_ctx, LANES), lambda i: (i, blk, 0)),
        ],
        out_specs=pl.BlockSpec((None, n_ctx, w), lambda i: (i, 0, 0)),
        out_shape=jax.ShapeDtypeStruct((b, n_ctx, w), jnp.bfloat16),
        compiler_params=_cparams(1),
        name="swa_attention_ctx",
    )(sink, qs, ks, vs)


HALO = 16


def _mix_body(a_ref, b_ref, cz_ref, hp_ref, hn_ref, x_ref, mod_ref, cw_ref, wo_ref, n2_ref, rw_ref,
              x1_ref, aff_ref):
    j = pl.program_id(1)
    d = D_MODEL
    first = j == 0
    last = j == pl.num_programs(1) - 1
    z = cz_ref[:, CONV_CH:2 * CONV_CH].astype(jnp.float32)
    zp = hp_ref[HALO - 1:HALO, CONV_CH:2 * CONV_CH].astype(jnp.float32)
    zn = hn_ref[0:1, CONV_CH:2 * CONV_CH].astype(jnp.float32)
    zp = jnp.where(first, jnp.zeros_like(zp), zp)
    zn = jnp.where(last, jnp.zeros_like(zn), zn)
    row = lax.broadcasted_iota(jnp.int32, z.shape, 0)
    z_dn = jnp.where(row == 0, zp, pltpu.roll(z, 1, 0))
    z_up = jnp.where(row == TILE - 1, zn, pltpu.roll(z, TILE - 1, 0))
    y = z_dn * cw_ref[0:1, :] + z * cw_ref[1:2, :] + z_up * cw_ref[2:3, :]
    cv = (cz_ref[:, 0:CONV_CH].astype(jnp.float32) * y).astype(jnp.bfloat16)
    mix = jnp.concatenate([a_ref[...], b_ref[...], cv], axis=-1)
    g1 = mod_ref[:, 2 * d:3 * d]
    sh2 = mod_ref[:, 3 * d:4 * d]
    sc2 = mod_ref[:, 4 * d:5 * d]
    x1 = x_ref[...] + g1 * _dot(mix, wo_ref[...])
    x1_ref[...] = x1
    h2 = (_rms(x1, n2_ref[...]) * (1.0 + sc2) + sh2).astype(jnp.bfloat16)
    lg = _dot_nt(rw_ref[...], h2)
    ex = jnp.exp(lg - jnp.max(lg, axis=0, keepdims=True))
    aff_ref[...] = ex / jnp.sum(ex, axis=0, keepdims=True)


def _mixer_out(a, bsw, cz, x, mod_l, mod_row0, lw, frame_tile0):
    b, n, d = x.shape
    t = cz.shape[1]
    hb = TILE // HALO
    here = lambda i, j: (i, j, 0)
    const = lambda i, j: (0, 0)
    if mod_row0 is None:
        mod_map = lambda i, j: (i, 0, 0)
    else:
        mod_map = lambda i, j: (mod_row0, 0, 0)
    return pl.pallas_call(
        _mix_body,
        grid=(b, n // TILE),
        in_specs=[
            pl.BlockSpec((None, TILE, a.shape[2]), here),
            pl.BlockSpec((None, TILE, bsw.shape[2]), here),
            pl.BlockSpec((None, TILE, cz.shape[2]), lambda i, j: (i, j + frame_tile0, 0)),
            pl.BlockSpec((None, HALO, cz.shape[2]),
                         lambda i, j: (i, jnp.maximum((j + frame_tile0) * hb - 1, 0), 0)),
            pl.BlockSpec((None, HALO, cz.shape[2]),
                         lambda i, j: (i, jnp.minimum((j + frame_tile0 + 1) * hb, t // HALO - 1), 0)),
            pl.BlockSpec((None, TILE, d), here),
            pl.BlockSpec((None, 1, N_MOD * d), mod_map),
            pl.BlockSpec((3, CONV_CH), const),
            pl.BlockSpec((d, d), const),
            pl.BlockSpec((1, d), const),
            pl.BlockSpec((N_EXPERTS, d), const),
        ],
        out_specs=[pl.BlockSpec((None, TILE, d), here),
                   pl.BlockSpec((None, N_EXPERTS, TILE), lambda i, j: (i, 0, j))],
        out_shape=[jax.ShapeDtypeStruct((b, n, d), jnp.float32),
                   jax.ShapeDtypeStruct((b, N_EXPERTS, n), jnp.float32)],
        compiler_params=_cparams(2),
        name="mixer_out",
    )(a, bsw, cz, cz, cz, x, mod_l, lw["cw"], lw["wo"], lw["n2"], lw["rw"])


ROUTE_ROWS = 128
ROUTE_SLOTS = 64


def _route_body(aff_ref, tri_ref, idx_ref, taff_ref, pos_ref, oi_ref, oa_ref, *, cap):
    rows, n = aff_ref.shape
    aff = aff_ref[...]
    bits = pltpu.bitcast(aff, jnp.int32)

    def count(mask):
        return jnp.sum(jnp.where(mask, 1.0, 0.0), axis=1, keepdims=True)

    def ones(mask):
        return jnp.where(mask, 1.0, 0.0).astype(jnp.bfloat16)

    def search(i, thr):
        cand = thr | (1 << (30 - i))
        return jnp.where(count(bits >= cand) >= cap, cand, thr)

    thr = lax.fori_loop(0, 31, search, jnp.zeros((rows, 1), jnp.int32))
    gt = bits > thr
    eq = bits == thr
    need = cap - count(gt)
    peq = _dot(ones(eq), tri_ref[...])
    sel = gt | (eq & (peq < need))
    pos = _dot(ones(sel), tri_ref[...])
    pos_ref[...] = jnp.where(sel, pos, -1.0)

    n_sweep = min(ROUTE_SLOTS, cap)
    tok = lax.broadcasted_iota(jnp.int32, (n_sweep, n), 1).astype(jnp.float32)
    lane = lax.broadcasted_iota(jnp.int32, (cap, rows), 1)
    oi_ref[...] = jnp.zeros_like(oi_ref)
    oa_ref[...] = jnp.zeros_like(oa_ref)

    def per_row(r, carry):
        prow = pos_ref[pl.ds(r, 1), :]
        arow = aff_ref[pl.ds(r, 1), :]
        icols = []
        acols = []
        for j0 in range(0, cap, n_sweep):
            slot = (lax.broadcasted_iota(jnp.int32, (n_sweep, 1), 0) + j0).astype(jnp.float32)
            hit = prow == slot
            icols.append(jnp.sum(jnp.where(hit, tok, 0.0), axis=1, keepdims=True))
            acols.append(jnp.sum(jnp.where(hit, arow, 0.0), axis=1, keepdims=True))
        icol = jnp.concatenate(icols, axis=0) if len(icols) > 1 else icols[0]
        acol = jnp.concatenate(acols, axis=0) if len(acols) > 1 else acols[0]
        oi_ref[...] = jnp.where(lane == r, icol, oi_ref[...])
        oa_ref[...] = jnp.where(lane == r, acol, oa_ref[...])
        return carry

    lax.fori_loop(0, rows, per_row, 0)
    idx_ref[...] = jnp.transpose(oi_ref[...]).astype(jnp.int32)
    taff_ref[...] = jnp.transpose(oa_ref[...])


def _routing(aff, tri, cap):
    b, e, n = aff.shape
    rows = b * e
    rr = min(ROUTE_ROWS, rows)
    idx, taff = pl.pallas_call(
        functools.partial(_route_body, cap=cap),
        grid=(rows // rr,),
        in_specs=[pl.BlockSpec((rr, n), lambda i: (i, 0)),
                  pl.BlockSpec((n, n), lambda i: (0, 0))],
        out_specs=[pl.BlockSpec((rr, cap), lambda i: (i, 0)),
                   pl.BlockSpec((rr, cap), lambda i: (i, 0))],
        out_shape=[jax.ShapeDtypeStruct((rows, cap), jnp.int32),
                   jax.ShapeDtypeStruct((rows, cap), jnp.float32)],
        scratch_shapes=[pltpu.VMEM((rr, n), jnp.float32),
                        pltpu.VMEM((cap, rr), jnp.float32),
                        pltpu.VMEM((cap, rr), jnp.float32)],
        compiler_params=_cparams(1),
        name="routing",
    )(aff.reshape(rows, n), tri)
    return idx.reshape(b, e, cap), taff.reshape(b, e, cap)


EXP_PER_STEP = 2
MOE_CHUNK = 512
SCATTER_GROUP = 16


def _moe_body(idx_ref, taff_ref, x_ref, mod_ref, n2_ref, fg_ref, wg_ref, wu_ref, wd_ref, o_ref,
              h_ref, acc_ref, *bufs, nch, cap, final):
    j = pl.program_id(1)
    d = D_MODEL
    nsl = d // LANES
    rows = MOE_CHUNK * nsl
    nes = N_EXPERTS // EXP_PER_STEP
    xg_refs = bufs[:EXP_PER_STEP]
    yb_refs = bufs[EXP_PER_STEP:]

    @pl.when(j < nch)
    def _():
        sh2 = mod_ref[:, 3 * d:4 * d]
        sc2 = mod_ref[:, 4 * d:5 * d]
        h2 = _rms(x_ref[...], n2_ref[...]) * (1.0 + sc2) + sh2
        base = pl.multiple_of(j * rows, rows)
        for s in range(nsl):
            h_ref[pl.ds(base + s, MOE_CHUNK, stride=nsl), :] = h2[:, s * LANES:(s + 1) * LANES]
        acc_ref[pl.ds(base, rows), :] = jnp.zeros((rows, LANES), jnp.float32)

    @pl.when((j >= nch) & (j < nch + nes))
    def _():
        tcol = jnp.transpose(taff_ref[...])
        for k in range(EXP_PER_STEP):
            xg_ref, yb_ref = xg_refs[k], yb_refs[k]
            tok = [pl.multiple_of(idx_ref[0, k * cap + jj], nsl) for jj in range(cap)]
            for jj in range(cap):
                xg_ref[jj * nsl:(jj + 1) * nsl, :] = h_ref[pl.ds(tok[jj], nsl), :]
            xs = jnp.concatenate([xg_ref[pl.ds(s, cap, stride=nsl), :] for s in range(nsl)], axis=-1)
            xs = xs.astype(jnp.bfloat16)
            gate = _dot(xs, wg_ref[k])
            up = _dot(xs, wu_ref[k])
            hid = (_silu(gate) * up).astype(jnp.bfloat16)
            y = _dot(hid, wd_ref[k]) * tcol[k * cap:(k + 1) * cap, 0:1]
            for s in range(nsl):
                yb_ref[pl.ds(s, cap, stride=nsl), :] = y[:, s * LANES:(s + 1) * LANES]
            for g0 in range(0, cap, SCATTER_GROUP):
                grp = range(g0, g0 + SCATTER_GROUP)
                vals = [acc_ref[pl.ds(tok[jj], nsl), :] + yb_ref[jj * nsl:(jj + 1) * nsl, :] for jj in grp]
                for jj, v in zip(grp, vals):
                    acc_ref[pl.ds(tok[jj], nsl), :] = v

    @pl.when(j >= nch + nes)
    def _():
        c = j - nch - nes
        base = pl.multiple_of(c * rows, rows)
        m = jnp.concatenate([acc_ref[pl.ds(base + s, MOE_CHUNK, stride=nsl), :] for s in range(nsl)], axis=-1)
        g2 = mod_ref[:, 5 * d:6 * d]
        x2 = x_ref[...] + g2 * m
        if final:
            x2 = _rms(x2, fg_ref[...])
        o_ref[...] = x2


def _experts(x1, idx, taff, mod_l, mod_row0, lw, final_g, final):
    nb, n_tok, d = x1.shape
    cap = idx.shape[2]
    assert n_tok % MOE_CHUNK == 0 and cap % SCATTER_GROUP == 0
    nch = n_tok // MOE_CHUNK
    nes = N_EXPERTS // EXP_PER_STEP
    steps = 2 * nch + nes
    nsl = d // LANES
    idx = (idx * nsl).reshape(nb, nes, 1, EXP_PER_STEP * cap)
    taff = jnp.broadcast_to(taff.reshape(nb, nes, 1, EXP_PER_STEP * cap), (nb, nes, SUBLANES, EXP_PER_STEP * cap))

    def chunk_in(i, j):
        c = jnp.where(j < nch, j, jnp.where(j < nch + nes, nch - 1, j - nch - nes))
        return (i, c, 0)

    def chunk_out(i, j):
        return (i, jnp.clip(j - nch - nes, 0, nch - 1), 0)

    def slots(i, j):
        return (i, jnp.clip(j - nch, 0, nes - 1), 0, 0)

    def expert(i, j):
        return (jnp.clip(j - nch, 0, nes - 1), 0, 0)

    const = lambda i, j: (0, 0)
    if mod_row0 is None:
        mod_map = lambda i, j: (i, 0, 0)
    else:
        mod_map = lambda i, j: (mod_row0, 0, 0)
    slot_buf = pltpu.VMEM((cap * nsl, LANES), jnp.float32)
    return pl.pallas_call(
        functools.partial(_moe_body, nch=nch, cap=cap, final=final),
        grid=(nb, steps),
        in_specs=[
            pl.BlockSpec((None, None, 1, EXP_PER_STEP * cap), slots, memory_space=pltpu.SMEM),
            pl.BlockSpec((None, None, SUBLANES, EXP_PER_STEP * cap), slots),
            pl.BlockSpec((None, MOE_CHUNK, d), chunk_in),
            pl.BlockSpec((None, 1, N_MOD * d), mod_map),
            pl.BlockSpec((1, d), const),
            pl.BlockSpec((1, d), const),
            pl.BlockSpec((EXP_PER_STEP, d, EXPERT_FF), expert),
            pl.BlockSpec((EXP_PER_STEP, d, EXPERT_FF), expert),
            pl.BlockSpec((EXP_PER_STEP, EXPERT_FF, d), expert),
        ],
        out_specs=pl.BlockSpec((None, MOE_CHUNK, d), chunk_out),
        out_shape=jax.ShapeDtypeStruct((nb, n_tok, d), jnp.float32),
        scratch_shapes=[pltpu.VMEM((n_tok * nsl, LANES), jnp.float32),
                        pltpu.VMEM((n_tok * nsl, LANES), jnp.float32)] + [slot_buf] * (2 * EXP_PER_STEP),
        compiler_params=_cparams(2),
        name="experts",
    )(idx, taff, x1, mod_l, lw["n2"], final_g, lw["wg"], lw["wu"], lw["wd"])


def _group_samples(x1, idx, taff):
    b, n_tok, d = x1.shape
    cap = idx.shape[2]
    g = max(k for k in range(1, 9) if b % k == 0 and (k * n_tok) % MOE_CHUNK == 0)
    off = (jnp.arange(b, dtype=jnp.int32) % g) * n_tok
    idx = idx + off[:, None, None]

    def merge(t):
        t = t.reshape(b // g, g, N_EXPERTS, cap)
        return jnp.swapaxes(t, 1, 2).reshape(b // g, N_EXPERTS, g * cap)

    return x1.reshape(b // g, g * n_tok, d), merge(idx), merge(taff)


def _swap_halves(w):
    half = w.shape[-1] // 2
    return jnp.concatenate([-w[..., half:], w[..., :half]], axis=-1)


def _swa_perm():
    cols = []
    for c in range(SWA_REP):
        for g in range(SWA_KV_HEADS):
            h = g * SWA_REP + c
            cols.extend(range(h * SWA_HEAD_DIM, (h + 1) * SWA_HEAD_DIM))
    return np.asarray(cols, np.int32)


def _layer_weights(w_in, q_g, w_uq, kv_g, w_ukv, conv_w, w_o, n1, n2, router_w, wg, wu, wd):
    bf = jnp.bfloat16
    d = w_in.shape[0]
    offs = np.cumsum([0, MLA_Q_RANK, MLA_KV_RANK, MLA_ROPE, SWA_HEADS * SWA_HEAD_DIM,
                      SWA_KV_HEADS * SWA_HEAD_DIM, SWA_KV_HEADS * SWA_HEAD_DIM, CONV_CH, CONV_CH, CONV_CH])
    part = [w_in[:, offs[i]:offs[i + 1]] for i in range(9)]
    w_cq, w_ckv, w_kr, w_sq, w_sk, w_sv, w_cb, w_cc, w_cu = part
    perm = _swa_perm()

    def per_head_swap(w, nh):
        wh = w.reshape(d, nh, -1)
        return _swap_halves(wh).reshape(d, -1)

    kr_block = jnp.concatenate([w_kr, _swap_halves(w_kr), jnp.zeros((d, LANES - 2 * MLA_ROPE), w_in.dtype)], axis=1)
    win = jnp.concatenate([
        w_cq, w_ckv, kr_block,
        w_sq[:, perm], per_head_swap(w_sq, SWA_HEADS)[:, perm],
        w_sk, per_head_swap(w_sk, SWA_KV_HEADS), w_sv, w_cb, w_cc, w_cu], axis=1).astype(bf)

    qd = MLA_NOPE + MLA_ROPE
    uq = w_uq.reshape(MLA_Q_RANK, MLA_HEADS, qd)
    zpad = jnp.zeros((MLA_Q_RANK, MLA_HEADS, HEAD_PAD - qd), w_uq.dtype)
    main = jnp.concatenate([uq, zpad], axis=-1)
    swp = jnp.concatenate([jnp.zeros_like(uq[..., :MLA_NOPE]), _swap_halves(uq[..., MLA_NOPE:]), zpad], axis=-1)
    wuq = jnp.concatenate([main.reshape(MLA_Q_RANK, -1), swp.reshape(MLA_Q_RANK, -1)], axis=1).astype(bf)

    ukv = w_ukv.reshape(MLA_KV_RANK, MLA_HEADS, MLA_NOPE + MLA_V)
    wk = jnp.concatenate([ukv[..., :MLA_NOPE],
                          jnp.zeros((MLA_KV_RANK, MLA_HEADS, HEAD_PAD - MLA_NOPE), w_ukv.dtype)], axis=-1)
    wk = wk.reshape(MLA_KV_RANK, -1).astype(bf)
    wv = ukv[..., MLA_NOPE:].reshape(MLA_KV_RANK, -1).astype(bf)

    e = np.zeros((LANES, MLA_HEADS * HEAD_PAD), np.float32)
    for h in range(MLA_HEADS):
        for l in range(MLA_ROPE):
            e[l, h * HEAD_PAD + MLA_NOPE + l] = 1.0
            e[MLA_ROPE + l, h * HEAD_PAD + MLA_NOPE + l] = 1.0

    n_mla = MLA_HEADS * MLA_V
    n_swa = SWA_HEADS * SWA_HEAD_DIM
    wo = jnp.concatenate([w_o[:n_mla], w_o[n_mla:n_mla + n_swa][perm], w_o[n_mla + n_swa:]], axis=0).astype(bf)
    return dict(
        g1=n1.reshape(1, -1), win=win, qg=q_g.reshape(1, -1), wuq=wuq, kvg=kv_g.reshape(1, -1),
        wk=wk, wvt=wv.T, e=jnp.asarray(e, bf), cw=conv_w, wo=wo, n2=n2.reshape(1, -1),
        rw=router_w.T.astype(bf), wg=wg.astype(bf), wu=wu.astype(bf), wd=wd.astype(bf))


def _axial_tables(n_tokens, rot_dim):
    rows = n_tokens // GRID_W
    row = jnp.repeat(jnp.arange(rows, dtype=jnp.float32), GRID_W)
    col = jnp.tile(jnp.arange(GRID_W, dtype=jnp.float32), rows)
    n_freq = rot_dim // 4
    inv = ROPE_BASE ** (-jnp.arange(n_freq, dtype=jnp.float32) / n_freq)
    ang = jnp.concatenate([row[:, None] * inv, col[:, None] * inv], axis=-1)
    return jnp.cos(ang), jnp.sin(ang)


def _row_table(n_ctx, n_lat):
    def with_ctx(cos, sin):
        one = jnp.ones((n_ctx, cos.shape[1]), jnp.float32)
        return (jnp.concatenate([cos, one], axis=0), jnp.concatenate([sin, 0.0 * one], axis=0))

    cm, sm = with_ctx(*_axial_tables(n_lat, MLA_ROPE))
    cs, ss = with_ctx(*_axial_tables(n_lat, SWA_HEAD_DIM))
    t = n_ctx + n_lat
    cm2 = jnp.concatenate([cm, cm], axis=1)
    sm2 = jnp.concatenate([sm, sm], axis=1)
    cs2 = jnp.concatenate([cs, cs], axis=1)
    ss2 = jnp.concatenate([ss, ss], axis=1)
    pad = jnp.zeros((t, HEAD_PAD - MLA_NOPE - MLA_ROPE), jnp.float32)
    cq = jnp.concatenate([jnp.ones((t, MLA_NOPE), jnp.float32), cm2, pad], axis=1) * (MLA_SCALE * LOG2E)
    sq = jnp.concatenate([jnp.zeros((t, MLA_NOPE), jnp.float32), sm2, pad], axis=1) * (MLA_SCALE * LOG2E)
    kr = jnp.concatenate([cm2, sm2, jnp.zeros((t, LANES - 2 * MLA_ROPE), jnp.float32)], axis=1)
    tab = jnp.concatenate([
        jnp.tile(cq, (1, MLA_HEADS)), jnp.tile(sq, (1, MLA_HEADS)), kr,
        jnp.tile(cs2, (1, SWA_HEADS)) * (SWA_SCALE * LOG2E), jnp.tile(ss2, (1, SWA_HEADS)) * (SWA_SCALE * LOG2E),
        jnp.tile(cs2, (1, SWA_KV_HEADS)), jnp.tile(ss2, (1, SWA_KV_HEADS))], axis=1)
    return tab


def _strict_upper(n):
    r = lax.broadcasted_iota(jnp.int32, (n, n), 0)
    c = lax.broadcasted_iota(jnp.int32, (n, n), 1)
    return (r < c).astype(jnp.bfloat16)


def kernel(x, c, ctx, c_ctx, ada_w, ada_b, norm1_g, w_in, mla_q_norm_g, mla_w_uq, mla_kv_norm_g, mla_w_ukv,
           swa_sink, conv_w, w_o, norm2_g, router_w, exp_w_gate, exp_w_up, exp_w_down, final_norm_g):
    b, s, d = x.shape
    n_ctx = ctx.shape[1]
    depth = ada_w.shape[0]
    assert d == D_MODEL and s % MLA_TQ == 0 and n_ctx == TILE and b + 1 <= MOD_ROWS
    cap_lat = EC_CAPACITY_FACTOR * s // N_EXPERTS
    cap_ctx = EC_CAPACITY_FACTOR * n_ctx // N_EXPERTS

    cc = jnp.concatenate([c, c_ctx[None, :], jnp.zeros((MOD_ROWS - b - 1, d), c.dtype)], axis=0)
    mod = _modulation(cc, ada_w, ada_b).reshape(depth, MOD_ROWS, 1, N_MOD * d)
    tab = _row_table(n_ctx, s)
    tri_lat = _strict_upper(s)
    tri_ctx = _strict_upper(n_ctx)
    swa_bias = _swa_bias(n_ctx, s)
    sink_slots = jnp.zeros((8,), jnp.float32)
    fg = final_norm_g.reshape(1, d)

    xl, xc = x, ctx
    for li in range(depth):
        last = li == depth - 1
        lw = _layer_weights(w_in[li], mla_q_norm_g[li], mla_w_uq[li], mla_kv_norm_g[li], mla_w_ukv[li],
                            conv_w[li], w_o[li], norm1_g[li], norm2_g[li], router_w[li],
                            exp_w_gate[li], exp_w_up[li], exp_w_down[li])
        mod_l = mod[li]
        sink = sink_slots.at[:SWA_HEADS].set(swa_sink[li] * LOG2E)
        qm, km, vmt, qs, ks, vs, cz = _projections(xl, xc, mod_l, lw, tab)
        a = _mla_attention(qm, km, vmt, s)
        bsw = _swa_attention(sink, qs, ks, vs, swa_bias, s)
        x1, aff = _mixer_out(a, bsw, cz, xl, mod_l, None, lw, 0)
        idx, taff = _routing(aff, tri_lat, cap_lat)
        if not last:
            ac = _mla_attention_ctx(qm, km, vmt, s)
            bc = _swa_attention_ctx(sink, qs, ks, vs, s)
            xc1, affc = _mixer_out(ac, bc, cz, xc, mod_l, b, lw, s // TILE)
            idxc, taffc = _routing(affc, tri_ctx, cap_ctx)
            xc = _experts(*_group_samples(xc1, idxc, taffc), mod_l, b, lw, fg, final=False).reshape(xc1.shape)
        xl = _experts(x1, idx, taff, mod_l, None, lw, fg, final=last)
    return xl
```

```python
import functools

import jax
import jax.numpy as jnp
import numpy as np
from jax import lax
from jax.experimental import pallas as pl
from jax.experimental.pallas import tpu as pltpu

D_MODEL = 1024
GRID_W = 64
NORM_EPS = 1e-6
ROPE_BASE = 10000.0

MLA_HEADS = 6
MLA_Q_RANK = 256
MLA_KV_RANK = 128
MLA_NOPE = 64
MLA_ROPE = 32
MLA_V = 64
MLA_SCALE = (MLA_NOPE + MLA_ROPE) ** -0.5

SWA_HEADS = 6
SWA_KV_HEADS = 2
SWA_REP = SWA_HEADS // SWA_KV_HEADS
SWA_HEAD_DIM = 64
SWA_WINDOW = 128
SWA_SCALE = SWA_HEAD_DIM ** -0.5

CONV_CH = 256
N_EXPERTS = 16
EXPERT_FF = 512
EC_CAPACITY_FACTOR = 2
N_MOD = 6

LANES = 128
SUBLANES = 8
TILE = 256
HEAD_PAD = 128
MOD_ROWS = 40
NEG_BIG = -1e30
LOG2E = 1.4426950408889634

_C_CQ = 0
_C_CKV = 256
_C_KR = 384
_C_SQ = 512
_C_SQSW = 896
_C_SK = 1280
_C_SKSW = 1408
_C_SV = 1536
_C_CB = 1664
_C_CC = 1920
_C_CU = 2176
IN_W = 2432

_T_CQ = 0
_T_SQ = 768
_T_KR = 1536
_T_CS = 1664
_T_SS = 2048
_T_CK = 2432
_T_SK = 2560
TAB_W = 2688

_VMEM_LIMIT = 56 * 1024 * 1024


def _cparams(n_grid):
    return pltpu.CompilerParams(dimension_semantics=("arbitrary",) * n_grid,
                                vmem_limit_bytes=_VMEM_LIMIT)


def _silu(v):
    return v * (1.0 / (1.0 + jnp.exp(-v)))


def _rms(v, g):
    return v * lax.rsqrt(jnp.mean(v * v, axis=-1, keepdims=True) + NORM_EPS) * g


def _dot(a, b):
    return jnp.dot(a, b, preferred_element_type=jnp.float32)


def _dot_nt(a, b):
    return lax.dot_general(a, b, (((1,), (1,)), ((), ())), preferred_element_type=jnp.float32)


def _mod_body(c_ref, w_ref, b_ref, o_ref):
    a = _silu(c_ref[...]).astype(jnp.bfloat16)
    o_ref[...] = _dot(a, w_ref[...].astype(jnp.bfloat16)) + b_ref[...]


def _modulation(cc, ada_w, ada_b):
    depth, d, n = ada_w.shape
    bn = 512
    return pl.pallas_call(
        _mod_body,
        grid=(depth, n // bn),
        in_specs=[
            pl.BlockSpec((MOD_ROWS, d), lambda l, i: (0, 0)),
            pl.BlockSpec((None, d, bn), lambda l, i: (l, 0, i)),
            pl.BlockSpec((None, 1, bn), lambda l, i: (l, 0, i)),
        ],
        out_specs=pl.BlockSpec((None, MOD_ROWS, bn), lambda l, i: (l, 0, i)),
        out_shape=jax.ShapeDtypeStruct((depth, MOD_ROWS, n), jnp.float32),
        compiler_params=_cparams(2),
        name="modulation",
    )(cc, ada_w, ada_b.reshape(depth, 1, n))


def _proj_body(x_ref, ctx_ref, mod_ref, g1_ref, win_ref, qg_ref, wuq_ref, kvg_ref, wk_ref, wvt_ref,
               e_ref, tab_ref, qm_ref, km_ref, vmt_ref, qs_ref, ks_ref, vs_ref, cz_ref, xs_ref):
    j = pl.program_id(0)
    is_ctx = j == pl.num_programs(0) - 1

    @pl.when(is_ctx)
    def _():
        xs_ref[...] = ctx_ref[...]

    @pl.when(jnp.logical_not(is_ctx))
    def _():
        xs_ref[...] = x_ref[...]

    d = D_MODEL
    sh1 = mod_ref[:, 0:d]
    sc1 = mod_ref[:, d:2 * d]
    h = (_rms(xs_ref[...], g1_ref[...]) * (1.0 + sc1) + sh1).astype(jnp.bfloat16)

    def proj(lo, hi):
        return _dot(h, win_ref[:, lo:hi])

    def tab(lo, n):
        return tab_ref[:, lo:lo + n]

    cq = _rms(proj(_C_CQ, _C_CKV), qg_ref[...]).astype(jnp.bfloat16)
    uq = _dot(cq, wuq_ref[...])
    nq = MLA_HEADS * HEAD_PAD
    qm_ref[...] = (uq[:, :nq] * tab(_T_CQ, nq) + uq[:, nq:] * tab(_T_SQ, nq)).astype(jnp.bfloat16)

    ckv = _rms(proj(_C_CKV, _C_KR), kvg_ref[...]).astype(jnp.bfloat16)
    krp = (proj(_C_KR, _C_SQ) * tab(_T_KR, LANES)).astype(jnp.bfloat16)
    km_ref[...] = (_dot(ckv, wk_ref[...]) + _dot(krp, e_ref[...])).astype(jnp.bfloat16)
    vmt_ref[...] = _dot_nt(wvt_ref[...], ckv).astype(jnp.bfloat16)

    nsq = SWA_HEADS * SWA_HEAD_DIM
    qs_ref[...] = (proj(_C_SQ, _C_SQSW) * tab(_T_CS, nsq)
                   + proj(_C_SQSW, _C_SK) * tab(_T_SS, nsq)).astype(jnp.bfloat16)
    ks_ref[...] = (proj(_C_SK, _C_SKSW) * tab(_T_CK, LANES)
                   + proj(_C_SKSW, _C_SV) * tab(_T_SK, LANES)).astype(jnp.bfloat16)
    vs_ref[...] = proj(_C_SV, _C_CB).astype(jnp.bfloat16)

    cz_ref[:, 0:CONV_CH] = proj(_C_CB, _C_CC).astype(jnp.bfloat16)
    cz_ref[:, CONV_CH:2 * CONV_CH] = (proj(_C_CC, _C_CU) * proj(_C_CU, IN_W)).astype(jnp.bfloat16)


def _projections(x, ctx, mod_l, lw, tab):
    b, s, d = x.shape
    n_ctx = ctx.shape[1]
    t = n_ctx + s
    nl = s // TILE
    const = lambda j, i: (0, 0)
    row_outs = [MLA_HEADS * HEAD_PAD, MLA_HEADS * HEAD_PAD, None, SWA_HEADS * SWA_HEAD_DIM, LANES, LANES,
                2 * CONV_CH]
    nv = MLA_HEADS * MLA_V
    out_specs = [pl.BlockSpec((None, nv, TILE), lambda j, i: (i, 0, j)) if w is None
                 else pl.BlockSpec((None, TILE, w), lambda j, i: (i, j, 0)) for w in row_outs]
    out_shape = [jax.ShapeDtypeStruct((b, nv, t) if w is None else (b, t, w), jnp.bfloat16) for w in row_outs]
    return pl.pallas_call(
        _proj_body,
        grid=(nl + 1, b),
        in_specs=[
            pl.BlockSpec((None, TILE, d), lambda j, i: (jnp.where(j == nl, 0, i), jnp.minimum(j, nl - 1), 0)),
            pl.BlockSpec((None, TILE, d), lambda j, i: (jnp.where(j == nl, i, 0), 0, 0)),
            pl.BlockSpec((None, 1, N_MOD * d), lambda j, i: (jnp.where(j == nl, b, i), 0, 0)),
            pl.BlockSpec((1, d), const),
            pl.BlockSpec((d, IN_W), const),
            pl.BlockSpec((1, MLA_Q_RANK), const),
            pl.BlockSpec((MLA_Q_RANK, 2 * MLA_HEADS * HEAD_PAD), const),
            pl.BlockSpec((1, MLA_KV_RANK), const),
            pl.BlockSpec((MLA_KV_RANK, MLA_HEADS * HEAD_PAD), const),
            pl.BlockSpec((nv, MLA_KV_RANK), const),
            pl.BlockSpec((LANES, MLA_HEADS * HEAD_PAD), const),
            pl.BlockSpec((TILE, TAB_W), lambda j, i: (j, 0)),
        ],
        out_specs=out_specs,
        out_shape=out_shape,
        scratch_shapes=[pltpu.VMEM((TILE, d), jnp.float32)],
        compiler_params=_cparams(2),
        name="projections",
    )(x, ctx, mod_l, lw["g1"], lw["win"], lw["qg"], lw["wuq"], lw["kvg"], lw["wk"], lw["wvt"], lw["e"], tab)


MLA_TQ = 2048


def _mla_pair(q_ref, k_ref, vt_ref, o_ref, c):
    tq = q_ref.shape[0]
    res = []
    for hh in range(2):
        lo = (2 * c + hh) * HEAD_PAD
        st = _dot_nt(k_ref[:, lo:lo + HEAD_PAD], q_ref[:, lo:lo + HEAD_PAD])
        m = jnp.max(st, axis=0, keepdims=True)
        p = jnp.exp2(st - m)
        l = jnp.sum(p, axis=0, keepdims=True)
        ot = _dot(vt_ref[c * LANES:(c + 1) * LANES, :], p.astype(jnp.bfloat16))
        res.append(ot / l)
    row = lax.broadcasted_iota(jnp.int32, (LANES, tq), 0)
    o_ref[:, c * LANES:(c + 1) * LANES] = jnp.transpose(
        jnp.where(row < MLA_V, res[0], res[1])).astype(jnp.bfloat16)


def _mla_main_body(q_ref, k_ref, vt_ref, o_ref):
    for c in range(MLA_HEADS // 2):
        _mla_pair(q_ref, k_ref, vt_ref, o_ref, c)


def _mla_ctx_body(q_ref, k_ref, vt_ref, o_ref):
    for c in range(MLA_HEADS // 2):
        _mla_pair(q_ref, k_ref, vt_ref, o_ref, c)


def _mla_attention(qm, km, vmt, n_lat):
    b, t, _ = qm.shape
    return pl.pallas_call(
        _mla_main_body,
        grid=(b, n_lat // MLA_TQ),
        in_specs=[
            pl.BlockSpec((None, MLA_TQ, qm.shape[2]), lambda i, j: (i, j, 0)),
            pl.BlockSpec((None, t, km.shape[2]), lambda i, j: (i, 0, 0)),
            pl.BlockSpec((None, vmt.shape[1], t), lambda i, j: (i, 0, 0)),
        ],
        out_specs=pl.BlockSpec((None, MLA_TQ, vmt.shape[1]), lambda i, j: (i, j, 0)),
        out_shape=jax.ShapeDtypeStruct((b, n_lat, vmt.shape[1]), jnp.bfloat16),
        compiler_params=_cparams(2),
        name="mla_attention",
    )(qm, km, vmt)


def _mla_attention_ctx(qm, km, vmt, n_lat):
    b, t, _ = qm.shape
    n_ctx = t - n_lat
    blk = n_lat // n_ctx
    return pl.pallas_call(
        _mla_ctx_body,
        grid=(b,),
        in_specs=[
            pl.BlockSpec((None, n_ctx, qm.shape[2]), lambda i: (i, blk, 0)),
            pl.BlockSpec((None, n_ctx, km.shape[2]), lambda i: (i, blk, 0)),
            pl.BlockSpec((None, vmt.shape[1], n_ctx), lambda i: (i, 0, blk)),
        ],
        out_specs=pl.BlockSpec((None, n_ctx, vmt.shape[1]), lambda i: (i, 0, 0)),
        out_shape=jax.ShapeDtypeStruct((b, n_ctx, vmt.shape[1]), jnp.bfloat16),
        compiler_params=_cparams(1),
        name="mla_attention_ctx",
    )(qm, km, vmt)


def _swa_attend(sink_ref, q_ref, kcat, vcat, valid, o_ref):
    rows = q_ref.shape[0]
    lane = lax.broadcasted_iota(jnp.int32, (rows, LANES), 1)
    lo_half = lane < SWA_HEAD_DIM
    for c in range(SWA_REP):
        q2 = q_ref[:, c * LANES:(c + 1) * LANES]
        res = []
        for g in range(SWA_KV_HEADS):
            keep = lo_half if g == 0 else jnp.logical_not(lo_half)
            qg = jnp.where(keep, q2, jnp.zeros_like(q2))
            s = _dot_nt(qg, kcat)
            if valid is not None:
                s = jnp.where(valid, s, NEG_BIG)
            sk = sink_ref[g * SWA_REP + c]
            m = jnp.maximum(jnp.max(s, axis=-1, keepdims=True), sk)
            p = jnp.exp2(s - m)
            l = jnp.sum(p, axis=-1, keepdims=True) + jnp.exp2(sk - m)
            res.append(_dot(p.astype(jnp.bfloat16), vcat) / l)
        o_ref[:, c * LANES:(c + 1) * LANES] = jnp.where(lo_half, res[0], res[1]).astype(jnp.bfloat16)


SWA_BAND = TILE + 2 * SWA_WINDOW


def _swa_band_start(j, n_lat):
    return jnp.clip(j * TILE - SWA_WINDOW, 0, n_lat - SWA_BAND)


def _swa_main_body(sink_ref, q_ref, k_ref, v_ref, bias_ref, o_ref, *, n_ctx, n_lat):
    ks = pl.multiple_of(_swa_band_start(pl.program_id(1), n_lat), SWA_WINDOW)
    kcat = jnp.concatenate([k_ref[n_lat:n_lat + n_ctx, :], k_ref[pl.ds(ks, SWA_BAND), :]], axis=0)
    vcat = jnp.concatenate([v_ref[n_lat:n_lat + n_ctx, :], v_ref[pl.ds(ks, SWA_BAND), :]], axis=0)
    vt = jnp.transpose(vcat.astype(jnp.float32)).astype(jnp.bfloat16)
    bias = jnp.concatenate([bias_ref[...]] * SWA_REP, axis=1)
    lane = lax.broadcasted_iota(jnp.int32, (TILE, LANES), 1)
    lo_half = lane < SWA_HEAD_DIM
    qcol = lax.broadcasted_iota(jnp.int32, (1, SWA_REP * TILE), 1)
    res = []
    for g in range(SWA_KV_HEADS):
        keep = lo_half if g == 0 else jnp.logical_not(lo_half)
        qg = jnp.concatenate([jnp.where(keep, q_ref[:, c * LANES:(c + 1) * LANES], 0.0).astype(jnp.bfloat16)
                              for c in range(SWA_REP)], axis=0)
        sk = jnp.full((1, SWA_REP * TILE), sink_ref[g * SWA_REP + SWA_REP - 1], jnp.float32)
        for c in range(SWA_REP - 2, -1, -1):
            sk = jnp.where(qcol < (c + 1) * TILE, sink_ref[g * SWA_REP + c], sk)
        st = _dot_nt(kcat, qg) + bias
        m = jnp.maximum(jnp.max(st, axis=0, keepdims=True), sk)
        p = jnp.exp2(st - m)
        l = jnp.sum(p, axis=0, keepdims=True) + jnp.exp2(sk - m)
        res.append(_dot(vt, p.astype(jnp.bfloat16)) / l)
    row = lax.broadcasted_iota(jnp.int32, (LANES, TILE), 0)
    for c in range(SWA_REP):
        pair = jnp.where(row < SWA_HEAD_DIM, res[0][:, c * TILE:(c + 1) * TILE], res[1][:, c * TILE:(c + 1) * TILE])
        o_ref[:, c * LANES:(c + 1) * LANES] = jnp.transpose(pair).astype(jnp.bfloat16)


def _swa_bias(n_ctx, n_lat):
    nt = n_lat // TILE
    r = lax.broadcasted_iota(jnp.int32, (n_ctx + SWA_BAND, TILE), 0)
    q = lax.broadcasted_iota(jnp.int32, (n_ctx + SWA_BAND, TILE), 1)
    out = []
    for j in (0, 1, nt - 1):
        kpos = _swa_band_start(j, n_lat) + r - n_ctx
        valid = (r < n_ctx) | (jnp.abs(j * TILE + q - kpos) <= SWA_WINDOW)
        out.append(jnp.where(valid, 0.0, NEG_BIG).astype(jnp.float32))
    return jnp.stack(out)


def _swa_ctx_body(sink_ref, q_ref, k_ref, v_ref, o_ref):
    _swa_attend(sink_ref, q_ref, k_ref[...], v_ref[...], None, o_ref)


def _swa_attention(sink, qs, ks, vs, bias, n_lat):
    b, t, w = qs.shape
    nt = n_lat // TILE
    assert nt >= 3
    return pl.pallas_call(
        functools.partial(_swa_main_body, n_ctx=t - n_lat, n_lat=n_lat),
        grid=(b, nt),
        in_specs=[
            pl.BlockSpec(memory_space=pltpu.SMEM),
            pl.BlockSpec((None, TILE, w), lambda i, j: (i, j, 0)),
            pl.BlockSpec((None, t, LANES), lambda i, j: (i, 0, 0)),
            pl.BlockSpec((None, t, LANES), lambda i, j: (i, 0, 0)),
            pl.BlockSpec((None,) + bias.shape[1:], lambda i, j: (jnp.where(j == 0, 0, jnp.where(j == nt - 1, 2, 1)), 0, 0)),
        ],
        out_specs=pl.BlockSpec((None, TILE, w), lambda i, j: (i, j, 0)),
        out_shape=jax.ShapeDtypeStruct((b, n_lat, w), jnp.bfloat16),
        compiler_params=_cparams(2),
        name="swa_attention",
    )(sink, qs, ks, vs, bias)


def _swa_attention_ctx(sink, qs, ks, vs, n_lat):
    b, t, w = qs.shape
    n_ctx = t - n_lat
    blk = n_lat // n_ctx
    return pl.pallas_call(
        _swa_ctx_body,
        grid=(b,),
        in_specs=[
            pl.BlockSpec(memory_space=pltpu.SMEM),
            pl.BlockSpec((None, n_ctx, w), lambda i: (i, blk, 0)),
            pl.BlockSpec((None, n_ctx, LANES), lambda i: (i, blk, 0)),
            pl.BlockSpec((None, n_ctx, LANES), lambda i: (i, blk, 0)),
        ],
        out_specs=pl.BlockSpec((None, n_ctx, w), lambda i: (i, 0, 0)),
        out_shape=jax.ShapeDtypeStruct((b, n_ctx, w), jnp.bfloat16),
        compiler_params=_cparams(1),
        name="swa_attention_ctx",
    )(sink, qs, ks, vs)


HALO = 16


def _mix_body(a_ref, b_ref, cz_ref, hp_ref, hn_ref, x_ref, mod_ref, cw_ref, wo_ref, n2_ref, rw_ref,
              x1_ref, aff_ref):
    j = pl.program_id(1)
    d = D_MODEL
    first = j == 0
    last = j == pl.num_programs(1) - 1
    z = cz_ref[:, CONV_CH:2 * CONV_CH].astype(jnp.float32)
    zp = hp_ref[HALO - 1:HALO, CONV_CH:2 * CONV_CH].astype(jnp.float32)
    zn = hn_ref[0:1, CONV_CH:2 * CONV_CH].astype(jnp.float32)
    zp = jnp.where(first, jnp.zeros_like(zp), zp)
    zn = jnp.where(last, jnp.zeros_like(zn), zn)
    row = lax.broadcasted_iota(jnp.int32, z.shape, 0)
    z_dn = jnp.where(row == 0, zp, pltpu.roll(z, 1, 0))
    rows = z.shape[0]
    z_up = jnp.where(row == rows - 1, zn, pltpu.roll(z, rows - 1, 0))
    y = z_dn * cw_ref[0:1, :] + z * cw_ref[1:2, :] + z_up * cw_ref[2:3, :]
    cv = (cz_ref[:, 0:CONV_CH].astype(jnp.float32) * y).astype(jnp.bfloat16)
    mix = jnp.concatenate([a_ref[...], b_ref[...], cv], axis=-1)
    g1 = mod_ref[:, 2 * d:3 * d]
    sh2 = mod_ref[:, 3 * d:4 * d]
    sc2 = mod_ref[:, 4 * d:5 * d]
    x1 = x_ref[...] + g1 * _dot(mix, wo_ref[...])
    x1_ref[...] = x1
    h2 = (_rms(x1, n2_ref[...]) * (1.0 + sc2) + sh2).astype(jnp.bfloat16)
    lg = _dot_nt(rw_ref[...], h2)
    ex = jnp.exp(lg - jnp.max(lg, axis=0, keepdims=True))
    aff_ref[...] = ex / jnp.sum(ex, axis=0, keepdims=True)


MIX_TILE = 512


def _mixer_out(a, bsw, cz, x, mod_l, mod_row0, lw, frame_row0):
    b, n, d = x.shape
    t = cz.shape[1]
    tm = min(MIX_TILE, n)
    assert n % tm == 0 and frame_row0 % tm == 0
    tile0 = frame_row0 // tm
    hb = tm // HALO
    here = lambda i, j: (i, j, 0)
    const = lambda i, j: (0, 0)
    if mod_row0 is None:
        mod_map = lambda i, j: (i, 0, 0)
    else:
        mod_map = lambda i, j: (mod_row0, 0, 0)
    return pl.pallas_call(
        _mix_body,
        grid=(b, n // tm),
        in_specs=[
            pl.BlockSpec((None, tm, a.shape[2]), here),
            pl.BlockSpec((None, tm, bsw.shape[2]), here),
            pl.BlockSpec((None, tm, cz.shape[2]), lambda i, j: (i, j + tile0, 0)),
            pl.BlockSpec((None, HALO, cz.shape[2]),
                         lambda i, j: (i, jnp.maximum((j + tile0) * hb - 1, 0), 0)),
            pl.BlockSpec((None, HALO, cz.shape[2]),
                         lambda i, j: (i, jnp.minimum((j + tile0 + 1) * hb, t // HALO - 1), 0)),
            pl.BlockSpec((None, tm, d), here),
            pl.BlockSpec((None, 1, N_MOD * d), mod_map),
            pl.BlockSpec((3, CONV_CH), const),
            pl.BlockSpec((d, d), const),
            pl.BlockSpec((1, d), const),
            pl.BlockSpec((N_EXPERTS, d), const),
        ],
        out_specs=[pl.BlockSpec((None, tm, d), here),
                   pl.BlockSpec((None, N_EXPERTS, tm), lambda i, j: (i, 0, j))],
        out_shape=[jax.ShapeDtypeStruct((b, n, d), jnp.float32),
                   jax.ShapeDtypeStruct((b, N_EXPERTS, n), jnp.float32)],
        compiler_params=_cparams(2),
        name="mixer_out",
    )(a, bsw, cz, cz, cz, x, mod_l, lw["cw"], lw["wo"], lw["n2"], lw["rw"])


ROUTE_ROWS = 128
ROUTE_SPLIT = 64


def _route_body(aff_ref, tri_ref, idx_ref, taff_ref, pos_ref, parts_ref, *, cap):
    rows, n = aff_ref.shape
    aff = aff_ref[...]
    bits = pltpu.bitcast(aff, jnp.int32)

    def count(mask):
        return jnp.sum(jnp.where(mask, 1.0, 0.0), axis=1, keepdims=True)

    def ones(mask):
        return jnp.where(mask, 1.0, 0.0).astype(jnp.bfloat16)

    def search(i, thr):
        cand = thr | (1 << (30 - i))
        return jnp.where(count(bits >= cand) >= cap, cand, thr)

    thr = lax.fori_loop(0, 31, search, jnp.zeros((rows, 1), jnp.int32))
    gt = bits > thr
    eq = bits == thr
    need = cap - count(gt)
    peq = _dot(ones(eq), tri_ref[...])
    sel = gt | (eq & (peq < need))
    pos = _dot(ones(sel), tri_ref[...])
    pos_ref[...] = jnp.where(sel, pos, -1.0)

    a1 = aff.astype(jnp.bfloat16).astype(jnp.float32)
    a2 = (aff - a1).astype(jnp.bfloat16).astype(jnp.float32)
    parts_ref[0] = a1
    parts_ref[1] = a2
    parts_ref[2] = aff - a1 - a2
    tok = lax.broadcasted_iota(jnp.int32, (1, n), 1)
    tok_hi = (tok // ROUTE_SPLIT).astype(jnp.float32)
    tok_lo = (tok % ROUTE_SPLIT).astype(jnp.float32)
    sub = lax.broadcasted_iota(jnp.int32, (SUBLANES, n), 0)
    slot = lax.broadcasted_iota(jnp.int32, (cap, 1), 0).astype(jnp.float32)

    def per_row(r, carry):
        onehot = jnp.where(pos_ref[pl.ds(r, 1), :] == slot, 1.0, 0.0).astype(jnp.bfloat16)
        vals = jnp.where(sub == 0, tok_hi, jnp.where(sub == 1, tok_lo, 0.0))
        for k in range(3):
            vals = jnp.where(sub == 2 + k, parts_ref[k, pl.ds(r, 1), :], vals)
        got = _dot_nt(vals.astype(jnp.bfloat16), onehot)
        idx_ref[pl.ds(r, 1), :] = (got[0:1, :] * ROUTE_SPLIT + got[1:2, :]).astype(jnp.int32)
        taff_ref[pl.ds(r, 1), :] = got[2:3, :] + got[3:4, :] + got[4:5, :]
        return carry

    lax.fori_loop(0, rows, per_row, 0)


def _routing(aff, tri, cap):
    b, e, n = aff.shape
    rows = b * e
    rr = min(ROUTE_ROWS, rows)
    idx, taff = pl.pallas_call(
        functools.partial(_route_body, cap=cap),
        grid=(rows // rr,),
        in_specs=[pl.BlockSpec((rr, n), lambda i: (i, 0)),
                  pl.BlockSpec((n, n), lambda i: (0, 0))],
        out_specs=[pl.BlockSpec((rr, cap), lambda i: (i, 0)),
                   pl.BlockSpec((rr, cap), lambda i: (i, 0))],
        out_shape=[jax.ShapeDtypeStruct((rows, cap), jnp.int32),
                   jax.ShapeDtypeStruct((rows, cap), jnp.float32)],
        scratch_shapes=[pltpu.VMEM((rr, n), jnp.float32),
                        pltpu.VMEM((3, rr, n), jnp.float32)],
        compiler_params=_cparams(1),
        name="routing",
    )(aff.reshape(rows, n), tri)
    return idx.reshape(b, e, cap), taff.reshape(b, e, cap)


EXP_PER_STEP = 2
MOE_CHUNK = 1024
SCATTER_GROUP = 16


def _moe_body(idx_ref, taff_ref, x_ref, mod_ref, n2_ref, fg_ref, wg_ref, wu_ref, wd_ref, o_ref,
              h_ref, acc_ref, *bufs, nch, cap, final):
    j = pl.program_id(1)
    d = D_MODEL
    nsl = d // LANES
    chunk = x_ref.shape[0]
    rows = chunk * nsl
    nes = N_EXPERTS // EXP_PER_STEP
    xg_refs = bufs[:EXP_PER_STEP]
    yb_refs = bufs[EXP_PER_STEP:]

    @pl.when(j < nch)
    def _():
        sh2 = mod_ref[:, 3 * d:4 * d]
        sc2 = mod_ref[:, 4 * d:5 * d]
        h2 = _rms(x_ref[...], n2_ref[...]) * (1.0 + sc2) + sh2
        base = pl.multiple_of(j * rows, rows)
        for s in range(nsl):
            h_ref[pl.ds(base + s, chunk, stride=nsl), :] = h2[:, s * LANES:(s + 1) * LANES]
        acc_ref[pl.ds(base, rows), :] = jnp.zeros((rows, LANES), jnp.float32)

    @pl.when((j >= nch) & (j < nch + nes))
    def _():
        tcol = jnp.transpose(taff_ref[...])
        for k in range(EXP_PER_STEP):
            xg_ref, yb_ref = xg_refs[k], yb_refs[k]
            tok = [pl.multiple_of(idx_ref[0, k * cap + jj], nsl) for jj in range(cap)]
            for jj in range(cap):
                xg_ref[jj * nsl:(jj + 1) * nsl, :] = h_ref[pl.ds(tok[jj], nsl), :]
            xs = jnp.concatenate([xg_ref[pl.ds(s, cap, stride=nsl), :] for s in range(nsl)], axis=-1)
            xs = xs.astype(jnp.bfloat16)
            gate = _dot(xs, wg_ref[k])
            up = _dot(xs, wu_ref[k])
            hid = (_silu(gate) * up).astype(jnp.bfloat16)
            y = _dot(hid, wd_ref[k]) * tcol[k * cap:(k + 1) * cap, 0:1]
            for s in range(nsl):
                yb_ref[pl.ds(s, cap, stride=nsl), :] = y[:, s * LANES:(s + 1) * LANES]
            for g0 in range(0, cap, SCATTER_GROUP):
                grp = range(g0, g0 + SCATTER_GROUP)
                vals = [acc_ref[pl.ds(tok[jj], nsl), :] + yb_ref[jj * nsl:(jj + 1) * nsl, :] for jj in grp]
                for jj, v in zip(grp, vals):
                    acc_ref[pl.ds(tok[jj], nsl), :] = v

    @pl.when(j >= nch + nes)
    def _():
        c = j - nch - nes
        base = pl.multiple_of(c * rows, rows)
        m = jnp.concatenate([acc_ref[pl.ds(base + s, chunk, stride=nsl), :] for s in range(nsl)], axis=-1)
        g2 = mod_ref[:, 5 * d:6 * d]
        x2 = x_ref[...] + g2 * m
        if final:
            x2 = _rms(x2, fg_ref[...])
        o_ref[...] = x2


def _experts(x1, idx, taff, mod_l, mod_row0, lw, final_g, final):
    nb, n_tok, d = x1.shape
    cap = idx.shape[2]
    chunk = MOE_CHUNK if n_tok % MOE_CHUNK == 0 else MOE_CHUNK // 2
    assert n_tok % chunk == 0 and cap % SCATTER_GROUP == 0
    nch = n_tok // chunk
    nes = N_EXPERTS // EXP_PER_STEP
    steps = 2 * nch + nes
    nsl = d // LANES
    idx = (idx * nsl).reshape(nb, nes, 1, EXP_PER_STEP * cap)
    taff = jnp.broadcast_to(taff.reshape(nb, nes, 1, EXP_PER_STEP * cap), (nb, nes, SUBLANES, EXP_PER_STEP * cap))

    def chunk_in(i, j):
        c = jnp.where(j < nch, j, jnp.where(j < nch + nes, nch - 1, j - nch - nes))
        return (i, c, 0)

    def chunk_out(i, j):
        return (i, jnp.clip(j - nch - nes, 0, nch - 1), 0)

    def slots(i, j):
        return (i, jnp.clip(j - nch, 0, nes - 1), 0, 0)

    def expert(i, j):
        return (jnp.clip(j - nch, 0, nes - 1), 0, 0)

    const = lambda i, j: (0, 0)
    if mod_row0 is None:
        mod_map = lambda i, j: (i, 0, 0)
    else:
        mod_map = lambda i, j: (mod_row0, 0, 0)
    slot_buf = pltpu.VMEM((cap * nsl, LANES), jnp.float32)
    return pl.pallas_call(
        functools.partial(_moe_body, nch=nch, cap=cap, final=final),
        grid=(nb, steps),
        in_specs=[
            pl.BlockSpec((None, None, 1, EXP_PER_STEP * cap), slots, memory_space=pltpu.SMEM),
            pl.BlockSpec((None, None, SUBLANES, EXP_PER_STEP * cap), slots),
            pl.BlockSpec((None, chunk, d), chunk_in),
            pl.BlockSpec((None, 1, N_MOD * d), mod_map),
            pl.BlockSpec((1, d), const),
            pl.BlockSpec((1, d), const),
            pl.BlockSpec((EXP_PER_STEP, d, EXPERT_FF), expert),
            pl.BlockSpec((EXP_PER_STEP, d, EXPERT_FF), expert),
            pl.BlockSpec((EXP_PER_STEP, EXPERT_FF, d), expert),
        ],
        out_specs=pl.BlockSpec((None, chunk, d), chunk_out),
        out_shape=jax.ShapeDtypeStruct((nb, n_tok, d), jnp.float32),
        scratch_shapes=[pltpu.VMEM((n_tok * nsl, LANES), jnp.float32),
                        pltpu.VMEM((n_tok * nsl, LANES), jnp.float32)] + [slot_buf] * (2 * EXP_PER_STEP),
        compiler_params=_cparams(2),
        name="experts",
    )(idx, taff, x1, mod_l, lw["n2"], final_g, lw["wg"], lw["wu"], lw["wd"])


def _group_samples(x1, idx, taff):
    b, n_tok, d = x1.shape
    cap = idx.shape[2]
    g = max(k for k in range(1, 9) if b % k == 0 and (k * n_tok) % (MOE_CHUNK // 2) == 0)
    off = (jnp.arange(b, dtype=jnp.int32) % g) * n_tok
    idx = idx + off[:, None, None]

    def merge(t):
        t = t.reshape(b // g, g, N_EXPERTS, cap)
        return jnp.swapaxes(t, 1, 2).reshape(b // g, N_EXPERTS, g * cap)

    return x1.reshape(b // g, g * n_tok, d), merge(idx), merge(taff)


def _swap_halves(w):
    half = w.shape[-1] // 2
    return jnp.concatenate([-w[..., half:], w[..., :half]], axis=-1)


def _swa_perm():
    cols = []
    for c in range(SWA_REP):
        for g in range(SWA_KV_HEADS):
            h = g * SWA_REP + c
            cols.extend(range(h * SWA_HEAD_DIM, (h + 1) * SWA_HEAD_DIM))
    return np.asarray(cols, np.int32)


def _layer_weights(w_in, q_g, w_uq, kv_g, w_ukv, conv_w, w_o, n1, n2, router_w, wg, wu, wd):
    bf = jnp.bfloat16
    d = w_in.shape[0]
    offs = np.cumsum([0, MLA_Q_RANK, MLA_KV_RANK, MLA_ROPE, SWA_HEADS * SWA_HEAD_DIM,
                      SWA_KV_HEADS * SWA_HEAD_DIM, SWA_KV_HEADS * SWA_HEAD_DIM, CONV_CH, CONV_CH, CONV_CH])
    part = [w_in[:, offs[i]:offs[i + 1]] for i in range(9)]
    w_cq, w_ckv, w_kr, w_sq, w_sk, w_sv, w_cb, w_cc, w_cu = part
    perm = _swa_perm()

    def per_head_swap(w, nh):
        wh = w.reshape(d, nh, -1)
        return _swap_halves(wh).reshape(d, -1)

    kr_block = jnp.concatenate([w_kr, _swap_halves(w_kr), jnp.zeros((d, LANES - 2 * MLA_ROPE), w_in.dtype)], axis=1)
    win = jnp.concatenate([
        w_cq, w_ckv, kr_block,
        w_sq[:, perm], per_head_swap(w_sq, SWA_HEADS)[:, perm],
        w_sk, per_head_swap(w_sk, SWA_KV_HEADS), w_sv, w_cb, w_cc, w_cu], axis=1).astype(bf)

    qd = MLA_NOPE + MLA_ROPE
    uq = w_uq.reshape(MLA_Q_RANK, MLA_HEADS, qd)
    zpad = jnp.zeros((MLA_Q_RANK, MLA_HEADS, HEAD_PAD - qd), w_uq.dtype)
    main = jnp.concatenate([uq, zpad], axis=-1)
    swp = jnp.concatenate([jnp.zeros_like(uq[..., :MLA_NOPE]), _swap_halves(uq[..., MLA_NOPE:]), zpad], axis=-1)
    wuq = jnp.concatenate([main.reshape(MLA_Q_RANK, -1), swp.reshape(MLA_Q_RANK, -1)], axis=1).astype(bf)

    ukv = w_ukv.reshape(MLA_KV_RANK, MLA_HEADS, MLA_NOPE + MLA_V)
    wk = jnp.concatenate([ukv[..., :MLA_NOPE],
                          jnp.zeros((MLA_KV_RANK, MLA_HEADS, HEAD_PAD - MLA_NOPE), w_ukv.dtype)], axis=-1)
    wk = wk.reshape(MLA_KV_RANK, -1).astype(bf)
    wv = ukv[..., MLA_NOPE:].reshape(MLA_KV_RANK, -1).astype(bf)

    e = np.zeros((LANES, MLA_HEADS * HEAD_PAD), np.float32)
    for h in range(MLA_HEADS):
        for l in range(MLA_ROPE):
            e[l, h * HEAD_PAD + MLA_NOPE + l] = 1.0
            e[MLA_ROPE + l, h * HEAD_PAD + MLA_NOPE + l] = 1.0

    n_mla = MLA_HEADS * MLA_V
    n_swa = SWA_HEADS * SWA_HEAD_DIM
    wo = jnp.concatenate([w_o[:n_mla], w_o[n_mla:n_mla + n_swa][perm], w_o[n_mla + n_swa:]], axis=0).astype(bf)
    return dict(
        g1=n1.reshape(1, -1), win=win, qg=q_g.reshape(1, -1), wuq=wuq, kvg=kv_g.reshape(1, -1),
        wk=wk, wvt=wv.T, e=jnp.asarray(e, bf), cw=conv_w, wo=wo, n2=n2.reshape(1, -1),
        rw=router_w.T.astype(bf), wg=wg.astype(bf), wu=wu.astype(bf), wd=wd.astype(bf))


def _axial_tables(n_tokens, rot_dim):
    rows = n_tokens // GRID_W
    row = jnp.repeat(jnp.arange(rows, dtype=jnp.float32), GRID_W)
    col = jnp.tile(jnp.arange(GRID_W, dtype=jnp.float32), rows)
    n_freq = rot_dim // 4
    inv = ROPE_BASE ** (-jnp.arange(n_freq, dtype=jnp.float32) / n_freq)
    ang = jnp.concatenate([row[:, None] * inv, col[:, None] * inv], axis=-1)
    return jnp.cos(ang), jnp.sin(ang)


def _row_table(n_ctx, n_lat):
    def with_ctx(cos, sin):
        one = jnp.ones((n_ctx, cos.shape[1]), jnp.float32)
        return (jnp.concatenate([cos, one], axis=0), jnp.concatenate([sin, 0.0 * one], axis=0))

    cm, sm = with_ctx(*_axial_tables(n_lat, MLA_ROPE))
    cs, ss = with_ctx(*_axial_tables(n_lat, SWA_HEAD_DIM))
    t = n_ctx + n_lat
    cm2 = jnp.concatenate([cm, cm], axis=1)
    sm2 = jnp.concatenate([sm, sm], axis=1)
    cs2 = jnp.concatenate([cs, cs], axis=1)
    ss2 = jnp.concatenate([ss, ss], axis=1)
    pad = jnp.zeros((t, HEAD_PAD - MLA_NOPE - MLA_ROPE), jnp.float32)
    cq = jnp.concatenate([jnp.ones((t, MLA_NOPE), jnp.float32), cm2, pad], axis=1) * (MLA_SCALE * LOG2E)
    sq = jnp.concatenate([jnp.zeros((t, MLA_NOPE), jnp.float32), sm2, pad], axis=1) * (MLA_SCALE * LOG2E)
    kr = jnp.concatenate([cm2, sm2, jnp.zeros((t, LANES - 2 * MLA_ROPE), jnp.float32)], axis=1)
    tab = jnp.concatenate([
        jnp.tile(cq, (1, MLA_HEADS)), jnp.tile(sq, (1, MLA_HEADS)), kr,
        jnp.tile(cs2, (1, SWA_HEADS)) * (SWA_SCALE * LOG2E), jnp.tile(ss2, (1, SWA_HEADS)) * (SWA_SCALE * LOG2E),
        jnp.tile(cs2, (1, SWA_KV_HEADS)), jnp.tile(ss2, (1, SWA_KV_HEADS))], axis=1)
    return tab


def _strict_upper(n):
    r = lax.broadcasted_iota(jnp.int32, (n, n), 0)
    c = lax.broadcasted_iota(jnp.int32, (n, n), 1)
    return (r < c).astype(jnp.bfloat16)


def kernel(x, c, ctx, c_ctx, ada_w, ada_b, norm1_g, w_in, mla_q_norm_g, mla_w_uq, mla_kv_norm_g, mla_w_ukv,
           swa_sink, conv_w, w_o, norm2_g, router_w, exp_w_gate, exp_w_up, exp_w_down, final_norm_g):
    b, s, d = x.shape
    n_ctx = ctx.shape[1]
    depth = ada_w.shape[0]
    assert d == D_MODEL and s % MLA_TQ == 0 and n_ctx == TILE and b + 1 <= MOD_ROWS
    cap_lat = EC_CAPACITY_FACTOR * s // N_EXPERTS
    cap_ctx = EC_CAPACITY_FACTOR * n_ctx // N_EXPERTS

    cc = jnp.concatenate([c, c_ctx[None, :], jnp.zeros((MOD_ROWS - b - 1, d), c.dtype)], axis=0)
    mod = _modulation(cc, ada_w, ada_b).reshape(depth, MOD_ROWS, 1, N_MOD * d)
    tab = _row_table(n_ctx, s)
    tri_lat = _strict_upper(s)
    tri_ctx = _strict_upper(n_ctx)
    swa_bias = _swa_bias(n_ctx, s)
    sink_slots = jnp.zeros((8,), jnp.float32)
    fg = final_norm_g.reshape(1, d)

    xl, xc = x, ctx
    for li in range(depth):
        last = li == depth - 1
        lw = _layer_weights(w_in[li], mla_q_norm_g[li], mla_w_uq[li], mla_kv_norm_g[li], mla_w_ukv[li],
                            conv_w[li], w_o[li], norm1_g[li], norm2_g[li], router_w[li],
                            exp_w_gate[li], exp_w_up[li], exp_w_down[li])
        mod_l = mod[li]
        sink = sink_slots.at[:SWA_HEADS].set(swa_sink[li] * LOG2E)
        qm, km, vmt, qs, ks, vs, cz = _projections(xl, xc, mod_l, lw, tab)
        a = _mla_attention(qm, km, vmt, s)
        bsw = _swa_attention(sink, qs, ks, vs, swa_bias, s)
        x1, aff = _mixer_out(a, bsw, cz, xl, mod_l, None, lw, 0)
        idx, taff = _routing(aff, tri_lat, cap_lat)
        if not last:
            ac = _mla_attention_ctx(qm, km, vmt, s)
            bc = _swa_attention_ctx(sink, qs, ks, vs, s)
            xc1, affc = _mixer_out(ac, bc, cz, xc, mod_l, b, lw, s)
            idxc, taffc = _routing(affc, tri_ctx, cap_ctx)
            xc = _experts(*_group_samples(xc1, idxc, taffc), mod_l, b, lw, fg, final=False).reshape(xc1.shape)
        xl = _experts(x1, idx, taff, mod_l, None, lw, fg, final=last)
    return xl
```

```python
import functools

import jax
import jax.numpy as jnp
import numpy as np
from jax import lax
from jax.experimental import pallas as pl
from jax.experimental.pallas import tpu as pltpu

D_MODEL = 1024
GRID_W = 64
NORM_EPS = 1e-6
ROPE_BASE = 10000.0

MLA_HEADS = 6
MLA_Q_RANK = 256
MLA_KV_RANK = 128
MLA_NOPE = 64
MLA_ROPE = 32
MLA_V = 64
MLA_SCALE = (MLA_NOPE + MLA_ROPE) ** -0.5

SWA_HEADS = 6
SWA_KV_HEADS = 2
SWA_REP = SWA_HEADS // SWA_KV_HEADS
SWA_HEAD_DIM = 64
SWA_WINDOW = 128
SWA_SCALE = SWA_HEAD_DIM ** -0.5

CONV_CH = 256
N_EXPERTS = 16
EXPERT_FF = 512
EC_CAPACITY_FACTOR = 2
N_MOD = 6

LANES = 128
SUBLANES = 8
TILE = 256
HEAD_PAD = 128
MOD_ROWS = 40
NEG_BIG = -1e30
LOG2E = 1.4426950408889634

_C_CQ = 0
_C_CKV = 256
_C_KR = 384
_C_SQ = 512
_C_SQSW = 896
_C_SK = 1280
_C_SKSW = 1408
_C_SV = 1536
_C_CB = 1664
_C_CC = 1920
_C_CU = 2176
IN_W = 2432

_T_CQ = 0
_T_SQ = 768
_T_KR = 1536
_T_CS = 1664
_T_SS = 2048
_T_CK = 2432
_T_SK = 2560
TAB_W = 2688

_VMEM_LIMIT = 56 * 1024 * 1024


def _cparams(n_grid):
    return pltpu.CompilerParams(dimension_semantics=("arbitrary",) * n_grid,
                                vmem_limit_bytes=_VMEM_LIMIT)


def _silu(v):
    return v * (1.0 / (1.0 + jnp.exp(-v)))


def _rms(v, g):
    return v * lax.rsqrt(jnp.mean(v * v, axis=-1, keepdims=True) + NORM_EPS) * g


def _dot(a, b):
    return jnp.dot(a, b, preferred_element_type=jnp.float32)


def _dot_nt(a, b):
    return lax.dot_general(a, b, (((1,), (1,)), ((), ())), preferred_element_type=jnp.float32)


def _mod_body(c_ref, w_ref, b_ref, o_ref):
    a = _silu(c_ref[...]).astype(jnp.bfloat16)
    o_ref[...] = _dot(a, w_ref[...].astype(jnp.bfloat16)) + b_ref[...]


def _modulation(cc, ada_w, ada_b):
    depth, d, n = ada_w.shape
    bn = 512
    return pl.pallas_call(
        _mod_body,
        grid=(depth, n // bn),
        in_specs=[
            pl.BlockSpec((MOD_ROWS, d), lambda l, i: (0, 0)),
            pl.BlockSpec((None, d, bn), lambda l, i: (l, 0, i)),
            pl.BlockSpec((None, 1, bn), lambda l, i: (l, 0, i)),
        ],
        out_specs=pl.BlockSpec((None, MOD_ROWS, bn), lambda l, i: (l, 0, i)),
        out_shape=jax.ShapeDtypeStruct((depth, MOD_ROWS, n), jnp.float32),
        compiler_params=_cparams(2),
        name="modulation",
    )(cc, ada_w, ada_b.reshape(depth, 1, n))


def _proj_body(x_ref, ctx_ref, mod_ref, g1_ref, win_ref, qg_ref, wuq_ref, kvg_ref, wk_ref, wvt_ref,
               e_ref, tab_ref, qm_ref, km_ref, vmt_ref, qs_ref, ks_ref, vs_ref, cz_ref, xs_ref):
    j = pl.program_id(0)
    is_ctx = j == pl.num_programs(0) - 1

    @pl.when(is_ctx)
    def _():
        xs_ref[...] = ctx_ref[...]

    @pl.when(jnp.logical_not(is_ctx))
    def _():
        xs_ref[...] = x_ref[...]

    d = D_MODEL
    sh1 = mod_ref[:, 0:d]
    sc1 = mod_ref[:, d:2 * d]
    h = (_rms(xs_ref[...], g1_ref[...]) * (1.0 + sc1) + sh1).astype(jnp.bfloat16)

    def proj(lo, hi):
        return _dot(h, win_ref[:, lo:hi])

    def tab(lo, n):
        return tab_ref[:, lo:lo + n]

    cq = _rms(proj(_C_CQ, _C_CKV), qg_ref[...]).astype(jnp.bfloat16)
    uq = _dot(cq, wuq_ref[...])
    nq = MLA_HEADS * HEAD_PAD
    qm_ref[...] = (uq[:, :nq] * tab(_T_CQ, nq) + uq[:, nq:] * tab(_T_SQ, nq)).astype(jnp.bfloat16)

    ckv = _rms(proj(_C_CKV, _C_KR), kvg_ref[...]).astype(jnp.bfloat16)
    krp = (proj(_C_KR, _C_SQ) * tab(_T_KR, LANES)).astype(jnp.bfloat16)
    km_ref[...] = (_dot(ckv, wk_ref[...]) + _dot(krp, e_ref[...])).astype(jnp.bfloat16)
    vmt_ref[...] = _dot_nt(wvt_ref[...], ckv).astype(jnp.bfloat16)

    nsq = SWA_HEADS * SWA_HEAD_DIM
    qs_ref[...] = (proj(_C_SQ, _C_SQSW) * tab(_T_CS, nsq)
                   + proj(_C_SQSW, _C_SK) * tab(_T_SS, nsq)).astype(jnp.bfloat16)
    ks_ref[...] = (proj(_C_SK, _C_SKSW) * tab(_T_CK, LANES)
                   + proj(_C_SKSW, _C_SV) * tab(_T_SK, LANES)).astype(jnp.bfloat16)
    vs_ref[...] = proj(_C_SV, _C_CB).astype(jnp.bfloat16)

    cz_ref[:, 0:CONV_CH] = proj(_C_CB, _C_CC).astype(jnp.bfloat16)
    cz_ref[:, CONV_CH:2 * CONV_CH] = (proj(_C_CC, _C_CU) * proj(_C_CU, IN_W)).astype(jnp.bfloat16)


def _projections(x, ctx, mod_l, lw, tab):
    b, s, d = x.shape
    n_ctx = ctx.shape[1]
    t = n_ctx + s
    nl = s // TILE
    const = lambda j, i: (0, 0)
    row_outs = [MLA_HEADS * HEAD_PAD, MLA_HEADS * HEAD_PAD, None, SWA_HEADS * SWA_HEAD_DIM, LANES, LANES,
                2 * CONV_CH]
    nv = MLA_HEADS * MLA_V
    out_specs = [pl.BlockSpec((None, nv, TILE), lambda j, i: (i, 0, j)) if w is None
                 else pl.BlockSpec((None, TILE, w), lambda j, i: (i, j, 0)) for w in row_outs]
    out_shape = [jax.ShapeDtypeStruct((b, nv, t) if w is None else (b, t, w), jnp.bfloat16) for w in row_outs]
    return pl.pallas_call(
        _proj_body,
        grid=(nl + 1, b),
        in_specs=[
            pl.BlockSpec((None, TILE, d), lambda j, i: (jnp.where(j == nl, 0, i), jnp.minimum(j, nl - 1), 0)),
            pl.BlockSpec((None, TILE, d), lambda j, i: (jnp.where(j == nl, i, 0), 0, 0)),
            pl.BlockSpec((None, 1, N_MOD * d), lambda j, i: (jnp.where(j == nl, b, i), 0, 0)),
            pl.BlockSpec((1, d), const),
            pl.BlockSpec((d, IN_W), const),
            pl.BlockSpec((1, MLA_Q_RANK), const),
            pl.BlockSpec((MLA_Q_RANK, 2 * MLA_HEADS * HEAD_PAD), const),
            pl.BlockSpec((1, MLA_KV_RANK), const),
            pl.BlockSpec((MLA_KV_RANK, MLA_HEADS * HEAD_PAD), const),
            pl.BlockSpec((nv, MLA_KV_RANK), const),
            pl.BlockSpec((LANES, MLA_HEADS * HEAD_PAD), const),
            pl.BlockSpec((TILE, TAB_W), lambda j, i: (j, 0)),
        ],
        out_specs=out_specs,
        out_shape=out_shape,
        scratch_shapes=[pltpu.VMEM((TILE, d), jnp.float32)],
        compiler_params=_cparams(2),
        name="projections",
    )(x, ctx, mod_l, lw["g1"], lw["win"], lw["qg"], lw["wuq"], lw["kvg"], lw["wk"], lw["wvt"], lw["e"], tab)


MLA_TQ = 2048


def _mla_pair(q_ref, k_ref, vt_ref, o_ref, c):
    tq = q_ref.shape[0]
    res = []
    for hh in range(2):
        lo = (2 * c + hh) * HEAD_PAD
        st = _dot_nt(k_ref[:, lo:lo + HEAD_PAD], q_ref[:, lo:lo + HEAD_PAD])
        m = jnp.max(st, axis=0, keepdims=True)
        p = jnp.exp2(st - m)
        l = jnp.sum(p, axis=0, keepdims=True)
        ot = _dot(vt_ref[c * LANES:(c + 1) * LANES, :], p.astype(jnp.bfloat16))
        res.append(ot / l)
    row = lax.broadcasted_iota(jnp.int32, (LANES, tq), 0)
    o_ref[:, c * LANES:(c + 1) * LANES] = jnp.transpose(
        jnp.where(row < MLA_V, res[0], res[1])).astype(jnp.bfloat16)


def _mla_main_body(q_ref, k_ref, vt_ref, o_ref):
    for c in range(MLA_HEADS // 2):
        _mla_pair(q_ref, k_ref, vt_ref, o_ref, c)


def _mla_ctx_body(q_ref, k_ref, vt_ref, o_ref):
    for c in range(MLA_HEADS // 2):
        _mla_pair(q_ref, k_ref, vt_ref, o_ref, c)


def _mla_attention(qm, km, vmt, n_lat):
    b, t, _ = qm.shape
    return pl.pallas_call(
        _mla_main_body,
        grid=(b, n_lat // MLA_TQ),
        in_specs=[
            pl.BlockSpec((None, MLA_TQ, qm.shape[2]), lambda i, j: (i, j, 0)),
            pl.BlockSpec((None, t, km.shape[2]), lambda i, j: (i, 0, 0)),
            pl.BlockSpec((None, vmt.shape[1], t), lambda i, j: (i, 0, 0)),
        ],
        out_specs=pl.BlockSpec((None, MLA_TQ, vmt.shape[1]), lambda i, j: (i, j, 0)),
        out_shape=jax.ShapeDtypeStruct((b, n_lat, vmt.shape[1]), jnp.bfloat16),
        compiler_params=_cparams(2),
        name="mla_attention",
    )(qm, km, vmt)


def _mla_attention_ctx(qm, km, vmt, n_lat):
    b, t, _ = qm.shape
    n_ctx = t - n_lat
    blk = n_lat // n_ctx
    return pl.pallas_call(
        _mla_ctx_body,
        grid=(b,),
        in_specs=[
            pl.BlockSpec((None, n_ctx, qm.shape[2]), lambda i: (i, blk, 0)),
            pl.BlockSpec((None, n_ctx, km.shape[2]), lambda i: (i, blk, 0)),
            pl.BlockSpec((None, vmt.shape[1], n_ctx), lambda i: (i, 0, blk)),
        ],
        out_specs=pl.BlockSpec((None, n_ctx, vmt.shape[1]), lambda i: (i, 0, 0)),
        out_shape=jax.ShapeDtypeStruct((b, n_ctx, vmt.shape[1]), jnp.bfloat16),
        compiler_params=_cparams(1),
        name="mla_attention_ctx",
    )(qm, km, vmt)


def _swa_attend(sink_ref, q_ref, kcat, vcat, valid, o_ref):
    rows = q_ref.shape[0]
    lane = lax.broadcasted_iota(jnp.int32, (rows, LANES), 1)
    lo_half = lane < SWA_HEAD_DIM
    for c in range(SWA_REP):
        q2 = q_ref[:, c * LANES:(c + 1) * LANES]
        res = []
        for g in range(SWA_KV_HEADS):
            keep = lo_half if g == 0 else jnp.logical_not(lo_half)
            qg = jnp.where(keep, q2, jnp.zeros_like(q2))
            s = _dot_nt(qg, kcat)
            if valid is not None:
                s = jnp.where(valid, s, NEG_BIG)
            sk = sink_ref[g * SWA_REP + c]
            m = jnp.maximum(jnp.max(s, axis=-1, keepdims=True), sk)
            p = jnp.exp2(s - m)
            l = jnp.sum(p, axis=-1, keepdims=True) + jnp.exp2(sk - m)
            res.append(_dot(p.astype(jnp.bfloat16), vcat) / l)
        o_ref[:, c * LANES:(c + 1) * LANES] = jnp.where(lo_half, res[0], res[1]).astype(jnp.bfloat16)


SWA_BAND = TILE + 2 * SWA_WINDOW


def _swa_band_start(j, n_lat):
    return jnp.clip(j * TILE - SWA_WINDOW, 0, n_lat - SWA_BAND)


def _swa_main_body(sink_ref, q_ref, k_ref, v_ref, bias_ref, o_ref, *, n_ctx, n_lat):
    ks = pl.multiple_of(_swa_band_start(pl.program_id(1), n_lat), SWA_WINDOW)
    kcat = jnp.concatenate([k_ref[n_lat:n_lat + n_ctx, :], k_ref[pl.ds(ks, SWA_BAND), :]], axis=0)
    vcat = jnp.concatenate([v_ref[n_lat:n_lat + n_ctx, :], v_ref[pl.ds(ks, SWA_BAND), :]], axis=0)
    vt = jnp.transpose(vcat.astype(jnp.float32)).astype(jnp.bfloat16)
    bias = jnp.concatenate([bias_ref[...]] * SWA_REP, axis=1)
    lane = lax.broadcasted_iota(jnp.int32, (TILE, LANES), 1)
    lo_half = lane < SWA_HEAD_DIM
    qcol = lax.broadcasted_iota(jnp.int32, (1, SWA_REP * TILE), 1)
    res = []
    for g in range(SWA_KV_HEADS):
        keep = lo_half if g == 0 else jnp.logical_not(lo_half)
        qg = jnp.concatenate([jnp.where(keep, q_ref[:, c * LANES:(c + 1) * LANES], 0.0).astype(jnp.bfloat16)
                              for c in range(SWA_REP)], axis=0)
        sk = jnp.full((1, SWA_REP * TILE), sink_ref[g * SWA_REP + SWA_REP - 1], jnp.float32)
        for c in range(SWA_REP - 2, -1, -1):
            sk = jnp.where(qcol < (c + 1) * TILE, sink_ref[g * SWA_REP + c], sk)
        st = _dot_nt(kcat, qg) + bias
        m = jnp.maximum(jnp.max(st, axis=0, keepdims=True), sk)
        p = jnp.exp2(st - m)
        l = jnp.sum(p, axis=0, keepdims=True) + jnp.exp2(sk - m)
        res.append(_dot(vt, p.astype(jnp.bfloat16)) / l)
    row = lax.broadcasted_iota(jnp.int32, (LANES, TILE), 0)
    for c in range(SWA_REP):
        pair = jnp.where(row < SWA_HEAD_DIM, res[0][:, c * TILE:(c + 1) * TILE], res[1][:, c * TILE:(c + 1) * TILE])
        o_ref[:, c * LANES:(c + 1) * LANES] = jnp.transpose(pair).astype(jnp.bfloat16)


def _swa_bias(n_ctx, n_lat):
    nt = n_lat // TILE
    r = lax.broadcasted_iota(jnp.int32, (n_ctx + SWA_BAND, TILE), 0)
    q = lax.broadcasted_iota(jnp.int32, (n_ctx + SWA_BAND, TILE), 1)
    out = []
    for j in (0, 1, nt - 1):
        kpos = _swa_band_start(j, n_lat) + r - n_ctx
        valid = (r < n_ctx) | (jnp.abs(j * TILE + q - kpos) <= SWA_WINDOW)
        out.append(jnp.where(valid, 0.0, NEG_BIG).astype(jnp.float32))
    return jnp.stack(out)


def _swa_ctx_body(sink_ref, q_ref, k_ref, v_ref, o_ref):
    _swa_attend(sink_ref, q_ref, k_ref[...], v_ref[...], None, o_ref)


def _swa_attention(sink, qs, ks, vs, bias, n_lat):
    b, t, w = qs.shape
    nt = n_lat // TILE
    assert nt >= 3
    return pl.pallas_call(
        functools.partial(_swa_main_body, n_ctx=t - n_lat, n_lat=n_lat),
        grid=(b, nt),
        in_specs=[
            pl.BlockSpec(memory_space=pltpu.SMEM),
            pl.BlockSpec((None, TILE, w), lambda i, j: (i, j, 0)),
            pl.BlockSpec((None, t, LANES), lambda i, j: (i, 0, 0)),
            pl.BlockSpec((None, t, LANES), lambda i, j: (i, 0, 0)),
            pl.BlockSpec((None,) + bias.shape[1:], lambda i, j: (jnp.where(j == 0, 0, jnp.where(j == nt - 1, 2, 1)), 0, 0)),
        ],
        out_specs=pl.BlockSpec((None, TILE, w), lambda i, j: (i, j, 0)),
        out_shape=jax.ShapeDtypeStruct((b, n_lat, w), jnp.bfloat16),
        compiler_params=_cparams(2),
        name="swa_attention",
    )(sink, qs, ks, vs, bias)


def _swa_attention_ctx(sink, qs, ks, vs, n_lat):
    b, t, w = qs.shape
    n_ctx = t - n_lat
    blk = n_lat // n_ctx
    return pl.pallas_call(
        _swa_ctx_body,
        grid=(b,),
        in_specs=[
            pl.BlockSpec(memory_space=pltpu.SMEM),
            pl.BlockSpec((None, n_ctx, w), lambda i: (i, blk, 0)),
            pl.BlockSpec((None, n_ctx, LANES), lambda i: (i, blk, 0)),
            pl.BlockSpec((None, n_ctx, LANES), lambda i: (i, blk, 0)),
        ],
        out_specs=pl.BlockSpec((None, n_ctx, w), lambda i: (i, 0, 0)),
        out_shape=jax.ShapeDtypeStruct((b, n_ctx, w), jnp.bfloat16),
        compiler_params=_cparams(1),
        name="swa_attention_ctx",
    )(sink, qs, ks, vs)


HALO = 16


def _mix_body(a_ref, b_ref, cz_ref, hp_ref, hn_ref, x_ref, mod_ref, cw_ref, wo_ref, n2_ref, rw_ref,
              x1_ref, aff_ref):
    j = pl.program_id(1)
    d = D_MODEL
    first = j == 0
    last = j == pl.num_programs(1) - 1
    z = cz_ref[:, CONV_CH:2 * CONV_CH].astype(jnp.float32)
    zp = hp_ref[HALO - 1:HALO, CONV_CH:2 * CONV_CH].astype(jnp.float32)
    zn = hn_ref[0:1, CONV_CH:2 * CONV_CH].astype(jnp.float32)
    zp = jnp.where(first, jnp.zeros_like(zp), zp)
    zn = jnp.where(last, jnp.zeros_like(zn), zn)
    row = lax.broadcasted_iota(jnp.int32, z.shape, 0)
    z_dn = jnp.where(row == 0, zp, pltpu.roll(z, 1, 0))
    rows = z.shape[0]
    z_up = jnp.where(row == rows - 1, zn, pltpu.roll(z, rows - 1, 0))
    y = z_dn * cw_ref[0:1, :] + z * cw_ref[1:2, :] + z_up * cw_ref[2:3, :]
    cv = (cz_ref[:, 0:CONV_CH].astype(jnp.float32) * y).astype(jnp.bfloat16)
    mix = jnp.concatenate([a_ref[...], b_ref[...], cv], axis=-1)
    g1 = mod_ref[:, 2 * d:3 * d]
    sh2 = mod_ref[:, 3 * d:4 * d]
    sc2 = mod_ref[:, 4 * d:5 * d]
    x1 = x_ref[...] + g1 * _dot(mix, wo_ref[...])
    x1_ref[...] = x1
    h2 = (_rms(x1, n2_ref[...]) * (1.0 + sc2) + sh2).astype(jnp.bfloat16)
    lg = _dot_nt(rw_ref[...], h2)
    ex = jnp.exp(lg - jnp.max(lg, axis=0, keepdims=True))
    aff_ref[...] = ex / jnp.sum(ex, axis=0, keepdims=True)


MIX_TILE = 512


def _mixer_out(a, bsw, cz, x, mod_l, mod_row0, lw, frame_row0):
    b, n, d = x.shape
    t = cz.shape[1]
    tm = min(MIX_TILE, n)
    assert n % tm == 0 and frame_row0 % tm == 0
    tile0 = frame_row0 // tm
    hb = tm // HALO
    here = lambda i, j: (i, j, 0)
    const = lambda i, j: (0, 0)
    if mod_row0 is None:
        mod_map = lambda i, j: (i, 0, 0)
    else:
        mod_map = lambda i, j: (mod_row0, 0, 0)
    return pl.pallas_call(
        _mix_body,
        grid=(b, n // tm),
        in_specs=[
            pl.BlockSpec((None, tm, a.shape[2]), here),
            pl.BlockSpec((None, tm, bsw.shape[2]), here),
            pl.BlockSpec((None, tm, cz.shape[2]), lambda i, j: (i, j + tile0, 0)),
            pl.BlockSpec((None, HALO, cz.shape[2]),
                         lambda i, j: (i, jnp.maximum((j + tile0) * hb - 1, 0), 0)),
            pl.BlockSpec((None, HALO, cz.shape[2]),
                         lambda i, j: (i, jnp.minimum((j + tile0 + 1) * hb, t // HALO - 1), 0)),
            pl.BlockSpec((None, tm, d), here),
            pl.BlockSpec((None, 1, N_MOD * d), mod_map),
            pl.BlockSpec((3, CONV_CH), const),
            pl.BlockSpec((d, d), const),
            pl.BlockSpec((1, d), const),
            pl.BlockSpec((N_EXPERTS, d), const),
        ],
        out_specs=[pl.BlockSpec((None, tm, d), here),
                   pl.BlockSpec((None, N_EXPERTS, tm), lambda i, j: (i, 0, j))],
        out_shape=[jax.ShapeDtypeStruct((b, n, d), jnp.float32),
                   jax.ShapeDtypeStruct((b, N_EXPERTS, n), jnp.float32)],
        compiler_params=_cparams(2),
        name="mixer_out",
    )(a, bsw, cz, cz, cz, x, mod_l, lw["cw"], lw["wo"], lw["n2"], lw["rw"])


ROUTE_ROWS = 128
ROUTE_SPLIT = 64


def _route_body(aff_ref, tri_ref, idx_ref, taff_ref, pos_ref, parts_ref, *, cap):
    rows, n = aff_ref.shape
    aff = aff_ref[...]
    bits = pltpu.bitcast(aff, jnp.int32)

    def count(mask):
        return jnp.sum(jnp.where(mask, 1.0, 0.0), axis=1, keepdims=True)

    def ones(mask):
        return jnp.where(mask, 1.0, 0.0).astype(jnp.bfloat16)

    def search(i, thr):
        cand = thr | (1 << (30 - i))
        return jnp.where(count(bits >= cand) >= cap, cand, thr)

    thr = lax.fori_loop(0, 31, search, jnp.zeros((rows, 1), jnp.int32))
    gt = bits > thr
    eq = bits == thr
    need = cap - count(gt)
    peq = _dot(ones(eq), tri_ref[...])
    sel = gt | (eq & (peq < need))
    pos = _dot(ones(sel), tri_ref[...])
    pos_ref[...] = jnp.where(sel, pos, -1.0)

    a1 = aff.astype(jnp.bfloat16).astype(jnp.float32)
    a2 = (aff - a1).astype(jnp.bfloat16).astype(jnp.float32)
    parts_ref[0] = a1
    parts_ref[1] = a2
    parts_ref[2] = aff - a1 - a2
    tok = lax.broadcasted_iota(jnp.int32, (1, n), 1)
    tok_hi = (tok // ROUTE_SPLIT).astype(jnp.float32)
    tok_lo = (tok % ROUTE_SPLIT).astype(jnp.float32)
    sub = lax.broadcasted_iota(jnp.int32, (SUBLANES, n), 0)
    slot = lax.broadcasted_iota(jnp.int32, (cap, 1), 0).astype(jnp.float32)

    def per_row(r, carry):
        onehot = jnp.where(pos_ref[pl.ds(r, 1), :] == slot, 1.0, 0.0).astype(jnp.bfloat16)
        vals = jnp.where(sub == 0, tok_hi, jnp.where(sub == 1, tok_lo, 0.0))
        for k in range(3):
            vals = jnp.where(sub == 2 + k, parts_ref[k, pl.ds(r, 1), :], vals)
        got = _dot_nt(vals.astype(jnp.bfloat16), onehot)
        idx_ref[pl.ds(r, 1), :] = (got[0:1, :] * ROUTE_SPLIT + got[1:2, :]).astype(jnp.int32)
        taff_ref[pl.ds(r, 1), :] = got[2:3, :] + got[3:4, :] + got[4:5, :]
        return carry

    lax.fori_loop(0, rows, per_row, 0)


def _routing(aff, tri, cap):
    b, e, n = aff.shape
    rows = b * e
    rr = min(ROUTE_ROWS, rows)
    idx, taff = pl.pallas_call(
        functools.partial(_route_body, cap=cap),
        grid=(rows // rr,),
        in_specs=[pl.BlockSpec((rr, n), lambda i: (i, 0)),
                  pl.BlockSpec((n, n), lambda i: (0, 0))],
        out_specs=[pl.BlockSpec((rr, cap), lambda i: (i, 0)),
                   pl.BlockSpec((rr, cap), lambda i: (i, 0))],
        out_shape=[jax.ShapeDtypeStruct((rows, cap), jnp.int32),
                   jax.ShapeDtypeStruct((rows, cap), jnp.float32)],
        scratch_shapes=[pltpu.VMEM((rr, n), jnp.float32),
                        pltpu.VMEM((3, rr, n), jnp.float32)],
        compiler_params=_cparams(1),
        name="routing",
    )(aff.reshape(rows, n), tri)
    return idx.reshape(b, e, cap), taff.reshape(b, e, cap)


EXP_PER_STEP = 2
MOE_CHUNK = 512
SCATTER_GROUP = 16


def _moe_body(idx_ref, taff_ref, x_ref, mod_ref, n2_ref, fg_ref, wg_ref, wu_ref, wd_ref, o_ref,
              h_ref, acc_ref, *bufs, nch, cap, final):
    j = pl.program_id(1)
    d = D_MODEL
    nsl = d // LANES
    chunk = x_ref.shape[0]
    rows = chunk * nsl
    nes = N_EXPERTS // EXP_PER_STEP
    span = EXP_PER_STEP * cap

    def xg(p, k):
        return bufs[p * 2 * EXP_PER_STEP + k]

    def yb(p, k):
        return bufs[p * 2 * EXP_PER_STEP + EXP_PER_STEP + k]

    def tok(base, k, jj):
        return pl.multiple_of(idx_ref[0, base + k * cap + jj], nsl)

    def gather(base, p, experts=range(EXP_PER_STEP)):
        for k in experts:
            for jj in range(cap):
                xg(p, k)[jj * nsl:(jj + 1) * nsl, :] = h_ref[pl.ds(tok(base, k, jj), nsl), :]

    def scatter(base, p, experts=range(EXP_PER_STEP)):
        for k in experts:
            for g0 in range(0, cap, SCATTER_GROUP):
                grp = range(g0, g0 + SCATTER_GROUP)
                dst = [tok(base, k, jj) for jj in grp]
                vals = [acc_ref[pl.ds(t, nsl), :] + yb(p, k)[jj * nsl:(jj + 1) * nsl, :] for jj, t in zip(grp, dst)]
                for t, v in zip(dst, vals):
                    acc_ref[pl.ds(t, nsl), :] = v

    @pl.when(j < nch)
    def _():
        sh2 = mod_ref[:, 3 * d:4 * d]
        sc2 = mod_ref[:, 4 * d:5 * d]
        h2 = _rms(x_ref[...], n2_ref[...]) * (1.0 + sc2) + sh2
        base = pl.multiple_of(j * rows, rows)
        for s in range(nsl):
            h_ref[pl.ds(base + s, chunk, stride=nsl), :] = h2[:, s * LANES:(s + 1) * LANES]
        acc_ref[pl.ds(base, rows), :] = jnp.zeros((rows, LANES), jnp.float32)

    @pl.when(j == nch - 1)
    def _():
        for k in range(EXP_PER_STEP):
            yb(1, k)[...] = jnp.zeros_like(yb(1, k))
        gather(span, 0)

    def expert_step(p):
        cur = (j - nch + 1) * span
        gather(cur + span, 1 - p)
        scatter(cur - span, 1 - p)
        tcol = jnp.transpose(taff_ref[...])
        for k in range(EXP_PER_STEP):
            xs = jnp.concatenate([xg(p, k)[pl.ds(s, cap, stride=nsl), :] for s in range(nsl)], axis=-1)
            xs = xs.astype(jnp.bfloat16)
            gate = _dot(xs, wg_ref[k])
            up = _dot(xs, wu_ref[k])
            hid = (_silu(gate) * up).astype(jnp.bfloat16)
            y = _dot(hid, wd_ref[k]) * tcol[k * cap:(k + 1) * cap, 0:1]
            for s in range(nsl):
                yb(p, k)[pl.ds(s, cap, stride=nsl), :] = y[:, s * LANES:(s + 1) * LANES]

    in_experts = (j >= nch) & (j < nch + nes)
    odd = lax.rem(j - nch, 2) == 1

    @pl.when(in_experts & jnp.logical_not(odd))
    def _():
        expert_step(0)

    @pl.when(in_experts & odd)
    def _():
        expert_step(1)

    @pl.when(j == nch + nes)
    def _():
        scatter(nes * span, (nes - 1) % 2)

    @pl.when(j >= nch + nes)
    def _():
        c = j - nch - nes
        base = pl.multiple_of(c * rows, rows)
        m = jnp.concatenate([acc_ref[pl.ds(base + s, chunk, stride=nsl), :] for s in range(nsl)], axis=-1)
        g2 = mod_ref[:, 5 * d:6 * d]
        x2 = x_ref[...] + g2 * m
        if final:
            x2 = _rms(x2, fg_ref[...])
        o_ref[...] = x2


def _experts(x1, idx, taff, mod_l, mod_row0, lw, final_g, final):
    nb, n_tok, d = x1.shape
    cap = idx.shape[2]
    chunk = MOE_CHUNK if n_tok % MOE_CHUNK == 0 else MOE_CHUNK // 2
    assert n_tok % chunk == 0 and cap % SCATTER_GROUP == 0
    nch = n_tok // chunk
    nes = N_EXPERTS // EXP_PER_STEP
    steps = 2 * nch + nes
    nsl = d // LANES
    span = EXP_PER_STEP * cap
    pad = jnp.zeros((nb, span), jnp.int32)
    idx = jnp.concatenate([pad, (idx * nsl).reshape(nb, nes * span), pad], axis=1).reshape(nb, 1, (nes + 2) * span)
    taff = jnp.broadcast_to(taff.reshape(nb, nes, 1, span), (nb, nes, SUBLANES, span))

    def chunk_in(i, j):
        c = jnp.where(j < nch, j, jnp.where(j < nch + nes, nch - 1, j - nch - nes))
        return (i, c, 0)

    def chunk_out(i, j):
        return (i, jnp.clip(j - nch - nes, 0, nch - 1), 0)

    def slots(i, j):
        return (i, jnp.clip(j - nch, 0, nes - 1), 0, 0)

    def expert(i, j):
        return (jnp.clip(j - nch, 0, nes - 1), 0, 0)

    const = lambda i, j: (0, 0)
    if mod_row0 is None:
        mod_map = lambda i, j: (i, 0, 0)
    else:
        mod_map = lambda i, j: (mod_row0, 0, 0)
    slot_buf = pltpu.VMEM((cap * nsl, LANES), jnp.float32)
    return pl.pallas_call(
        functools.partial(_moe_body, nch=nch, cap=cap, final=final),
        grid=(nb, steps),
        in_specs=[
            pl.BlockSpec((None, 1, (nes + 2) * span), lambda i, j: (i, 0, 0), memory_space=pltpu.SMEM),
            pl.BlockSpec((None, None, SUBLANES, span), slots),
            pl.BlockSpec((None, chunk, d), chunk_in),
            pl.BlockSpec((None, 1, N_MOD * d), mod_map),
            pl.BlockSpec((1, d), const),
            pl.BlockSpec((1, d), const),
            pl.BlockSpec((EXP_PER_STEP, d, EXPERT_FF), expert),
            pl.BlockSpec((EXP_PER_STEP, d, EXPERT_FF), expert),
            pl.BlockSpec((EXP_PER_STEP, EXPERT_FF, d), expert),
        ],
        out_specs=pl.BlockSpec((None, chunk, d), chunk_out),
        out_shape=jax.ShapeDtypeStruct((nb, n_tok, d), jnp.float32),
        scratch_shapes=[pltpu.VMEM((n_tok * nsl, LANES), jnp.float32),
                        pltpu.VMEM((n_tok * nsl, LANES), jnp.float32)] + [slot_buf] * (4 * EXP_PER_STEP),
        compiler_params=_cparams(2),
        name="experts",
    )(idx, taff, x1, mod_l, lw["n2"], final_g, lw["wg"], lw["wu"], lw["wd"])


def _group_samples(x1, idx, taff):
    b, n_tok, d = x1.shape
    cap = idx.shape[2]
    g = max(k for k in range(1, 9) if b % k == 0 and (k * n_tok) % (MOE_CHUNK // 2) == 0)
    off = (jnp.arange(b, dtype=jnp.int32) % g) * n_tok
    idx = idx + off[:, None, None]

    def merge(t):
        t = t.reshape(b // g, g, N_EXPERTS, cap)
        return jnp.swapaxes(t, 1, 2).reshape(b // g, N_EXPERTS, g * cap)

    return x1.reshape(b // g, g * n_tok, d), merge(idx), merge(taff)


def _swap_halves(w):
    half = w.shape[-1] // 2
    return jnp.concatenate([-w[..., half:], w[..., :half]], axis=-1)


def _swa_perm():
    cols = []
    for c in range(SWA_REP):
        for g in range(SWA_KV_HEADS):
            h = g * SWA_REP + c
            cols.extend(range(h * SWA_HEAD_DIM, (h + 1) * SWA_HEAD_DIM))
    return np.asarray(cols, np.int32)


def _layer_weights(w_in, q_g, w_uq, kv_g, w_ukv, conv_w, w_o, n1, n2, router_w, wg, wu, wd):
    bf = jnp.bfloat16
    d = w_in.shape[0]
    offs = np.cumsum([0, MLA_Q_RANK, MLA_KV_RANK, MLA_ROPE, SWA_HEADS * SWA_HEAD_DIM,
                      SWA_KV_HEADS * SWA_HEAD_DIM, SWA_KV_HEADS * SWA_HEAD_DIM, CONV_CH, CONV_CH, CONV_CH])
    part = [w_in[:, offs[i]:offs[i + 1]] for i in range(9)]
    w_cq, w_ckv, w_kr, w_sq, w_sk, w_sv, w_cb, w_cc, w_cu = part
    perm = _swa_perm()

    def per_head_swap(w, nh):
        wh = w.reshape(d, nh, -1)
        return _swap_halves(wh).reshape(d, -1)

    kr_block = jnp.concatenate([w_kr, _swap_halves(w_kr), jnp.zeros((d, LANES - 2 * MLA_ROPE), w_in.dtype)], axis=1)
    win = jnp.concatenate([
        w_cq, w_ckv, kr_block,
        w_sq[:, perm], per_head_swap(w_sq, SWA_HEADS)[:, perm],
        w_sk, per_head_swap(w_sk, SWA_KV_HEADS), w_sv, w_cb, w_cc, w_cu], axis=1).astype(bf)

    qd = MLA_NOPE + MLA_ROPE
    uq = w_uq.reshape(MLA_Q_RANK, MLA_HEADS, qd)
    zpad = jnp.zeros((MLA_Q_RANK, MLA_HEADS, HEAD_PAD - qd), w_uq.dtype)
    main = jnp.concatenate([uq, zpad], axis=-1)
    swp = jnp.concatenate([jnp.zeros_like(uq[..., :MLA_NOPE]), _swap_halves(uq[..., MLA_NOPE:]), zpad], axis=-1)
    wuq = jnp.concatenate([main.reshape(MLA_Q_RANK, -1), swp.reshape(MLA_Q_RANK, -1)], axis=1).astype(bf)

    ukv = w_ukv.reshape(MLA_KV_RANK, MLA_HEADS, MLA_NOPE + MLA_V)
    wk = jnp.concatenate([ukv[..., :MLA_NOPE],
                          jnp.zeros((MLA_KV_RANK, MLA_HEADS, HEAD_PAD - MLA_NOPE), w_ukv.dtype)], axis=-1)
    wk = wk.reshape(MLA_KV_RANK, -1).astype(bf)
    wv = ukv[..., MLA_NOPE:].reshape(MLA_KV_RANK, -1).astype(bf)

    e = np.zeros((LANES, MLA_HEADS * HEAD_PAD), np.float32)
    for h in range(MLA_HEADS):
        for l in range(MLA_ROPE):
            e[l, h * HEAD_PAD + MLA_NOPE + l] = 1.0
            e[MLA_ROPE + l, h * HEAD_PAD + MLA_NOPE + l] = 1.0

    n_mla = MLA_HEADS * MLA_V
    n_swa = SWA_HEADS * SWA_HEAD_DIM
    wo = jnp.concatenate([w_o[:n_mla], w_o[n_mla:n_mla + n_swa][perm], w_o[n_mla + n_swa:]], axis=0).astype(bf)
    return dict(
        g1=n1.reshape(1, -1), win=win, qg=q_g.reshape(1, -1), wuq=wuq, kvg=kv_g.reshape(1, -1),
        wk=wk, wvt=wv.T, e=jnp.asarray(e, bf), cw=conv_w, wo=wo, n2=n2.reshape(1, -1),
        rw=router_w.T.astype(bf), wg=wg.astype(bf), wu=wu.astype(bf), wd=wd.astype(bf))


def _axial_tables(n_tokens, rot_dim):
    rows = n_tokens // GRID_W
    row = jnp.repeat(jnp.arange(rows, dtype=jnp.float32), GRID_W)
    col = jnp.tile(jnp.arange(GRID_W, dtype=jnp.float32), rows)
    n_freq = rot_dim // 4
    inv = ROPE_BASE ** (-jnp.arange(n_freq, dtype=jnp.float32) / n_freq)
    ang = jnp.concatenate([row[:, None] * inv, col[:, None] * inv], axis=-1)
    return jnp.cos(ang), jnp.sin(ang)


def _row_table(n_ctx, n_lat):
    def with_ctx(cos, sin):
        one = jnp.ones((n_ctx, cos.shape[1]), jnp.float32)
        return (jnp.concatenate([cos, one], axis=0), jnp.concatenate([sin, 0.0 * one], axis=0))

    cm, sm = with_ctx(*_axial_tables(n_lat, MLA_ROPE))
    cs, ss = with_ctx(*_axial_tables(n_lat, SWA_HEAD_DIM))
    t = n_ctx + n_lat
    cm2 = jnp.concatenate([cm, cm], axis=1)
    sm2 = jnp.concatenate([sm, sm], axis=1)
    cs2 = jnp.concatenate([cs, cs], axis=1)
    ss2 = jnp.concatenate([ss, ss], axis=1)
    pad = jnp.zeros((t, HEAD_PAD - MLA_NOPE - MLA_ROPE), jnp.float32)
    cq = jnp.concatenate([jnp.ones((t, MLA_NOPE), jnp.float32), cm2, pad], axis=1) * (MLA_SCALE * LOG2E)
    sq = jnp.concatenate([jnp.zeros((t, MLA_NOPE), jnp.float32), sm2, pad], axis=1) * (MLA_SCALE * LOG2E)
    kr = jnp.concatenate([cm2, sm2, jnp.zeros((t, LANES - 2 * MLA_ROPE), jnp.float32)], axis=1)
    tab = jnp.concatenate([
        jnp.tile(cq, (1, MLA_HEADS)), jnp.tile(sq, (1, MLA_HEADS)), kr,
        jnp.tile(cs2, (1, SWA_HEADS)) * (SWA_SCALE * LOG2E), jnp.tile(ss2, (1, SWA_HEADS)) * (SWA_SCALE * LOG2E),
        jnp.tile(cs2, (1, SWA_KV_HEADS)), jnp.tile(ss2, (1, SWA_KV_HEADS))], axis=1)
    return tab


def _strict_upper(n):
    r = lax.broadcasted_iota(jnp.int32, (n, n), 0)
    c = lax.broadcasted_iota(jnp.int32, (n, n), 1)
    return (r < c).astype(jnp.bfloat16)


def kernel(x, c, ctx, c_ctx, ada_w, ada_b, norm1_g, w_in, mla_q_norm_g, mla_w_uq, mla_kv_norm_g, mla_w_ukv,
           swa_sink, conv_w, w_o, norm2_g, router_w, exp_w_gate, exp_w_up, exp_w_down, final_norm_g):
    b, s, d = x.shape
    n_ctx = ctx.shape[1]
    depth = ada_w.shape[0]
    assert d == D_MODEL and s % MLA_TQ == 0 and n_ctx == TILE and b + 1 <= MOD_ROWS
    cap_lat = EC_CAPACITY_FACTOR * s // N_EXPERTS
    cap_ctx = EC_CAPACITY_FACTOR * n_ctx // N_EXPERTS

    cc = jnp.concatenate([c, c_ctx[None, :], jnp.zeros((MOD_ROWS - b - 1, d), c.dtype)], axis=0)
    mod = _modulation(cc, ada_w, ada_b).reshape(depth, MOD_ROWS, 1, N_MOD * d)
    tab = _row_table(n_ctx, s)
    tri_lat = _strict_upper(s)
    tri_ctx = _strict_upper(n_ctx)
    swa_bias = _swa_bias(n_ctx, s)
    sink_slots = jnp.zeros((8,), jnp.float32)
    fg = final_norm_g.reshape(1, d)

    xl, xc = x, ctx
    for li in range(depth):
        last = li == depth - 1
        lw = _layer_weights(w_in[li], mla_q_norm_g[li], mla_w_uq[li], mla_kv_norm_g[li], mla_w_ukv[li],
                            conv_w[li], w_o[li], norm1_g[li], norm2_g[li], router_w[li],
                            exp_w_gate[li], exp_w_up[li], exp_w_down[li])
        mod_l = mod[li]
        sink = sink_slots.at[:SWA_HEADS].set(swa_sink[li] * LOG2E)
        qm, km, vmt, qs, ks, vs, cz = _projections(xl, xc, mod_l, lw, tab)
        a = _mla_attention(qm, km, vmt, s)
        bsw = _swa_attention(sink, qs, ks, vs, swa_bias, s)
        x1, aff = _mixer_out(a, bsw, cz, xl, mod_l, None, lw, 0)
        idx, taff = _routing(aff, tri_lat, cap_lat)
        if not last:
            ac = _mla_attention_ctx(qm, km, vmt, s)
            bc = _swa_attention_ctx(sink, qs, ks, vs, s)
            xc1, affc = _mixer_out(ac, bc, cz, xc, mod_l, b, lw, s)
            idxc, taffc = _routing(affc, tri_ctx, cap_ctx)
            xc = _experts(*_group_samples(xc1, idxc, taffc), mod_l, b, lw, fg, final=False).reshape(xc1.shape)
        xl = _experts(x1, idx, taff, mod_l, None, lw, fg, final=last)
    return xl
```

```python
import functools

import jax
import jax.numpy as jnp
import numpy as np
from jax import lax
from jax.experimental import pallas as pl
from jax.experimental.pallas import tpu as pltpu

D_MODEL = 1024
GRID_W = 64
NORM_EPS = 1e-6
ROPE_BASE = 10000.0

MLA_HEADS = 6
MLA_Q_RANK = 256
MLA_KV_RANK = 128
MLA_NOPE = 64
MLA_ROPE = 32
MLA_V = 64
MLA_SCALE = (MLA_NOPE + MLA_ROPE) ** -0.5

SWA_HEADS = 6
SWA_KV_HEADS = 2
SWA_REP = SWA_HEADS // SWA_KV_HEADS
SWA_HEAD_DIM = 64
SWA_WINDOW = 128
SWA_SCALE = SWA_HEAD_DIM ** -0.5

CONV_CH = 256
N_EXPERTS = 16
EXPERT_FF = 512
EC_CAPACITY_FACTOR = 2
N_MOD = 6

LANES = 128
SUBLANES = 8
TILE = 256
HEAD_PAD = 128
MOD_ROWS = 40
NEG_BIG = -1e30
LOG2E = 1.4426950408889634

_C_CQ = 0
_C_CKV = 256
_C_KR = 384
_C_SQ = 512
_C_SK = 896
_C_SV = 1024
_C_CB = 1152
_C_CC = 1408
_C_CU = 1664
IN_W = 1920

_T_CQ = 0
_T_SQ = 768
_T_KR = 1536
_T_CS = 1664
_T_SS = 2048
_T_CK = 2432
_T_SK = 2560
TAB_W = 2688

_VMEM_LIMIT = 56 * 1024 * 1024


def _cparams(n_grid):
    return pltpu.CompilerParams(dimension_semantics=("arbitrary",) * n_grid,
                                vmem_limit_bytes=_VMEM_LIMIT)


def _silu(v):
    return v * (1.0 / (1.0 + jnp.exp(-v)))


def _rms(v, g):
    return v * lax.rsqrt(jnp.mean(v * v, axis=-1, keepdims=True) + NORM_EPS) * g


def _dot(a, b):
    return jnp.dot(a, b, preferred_element_type=jnp.float32)


def _dot_nt(a, b):
    return lax.dot_general(a, b, (((1,), (1,)), ((), ())), preferred_element_type=jnp.float32)


def _mod_body(c_ref, w_ref, b_ref, o_ref):
    a = _silu(c_ref[...]).astype(jnp.bfloat16)
    o_ref[...] = _dot(a, w_ref[...].astype(jnp.bfloat16)) + b_ref[...]


def _modulation(cc, ada_w, ada_b):
    depth, d, n = ada_w.shape
    bn = 512
    return pl.pallas_call(
        _mod_body,
        grid=(depth, n // bn),
        in_specs=[
            pl.BlockSpec((MOD_ROWS, d), lambda l, i: (0, 0)),
            pl.BlockSpec((None, d, bn), lambda l, i: (l, 0, i)),
            pl.BlockSpec((None, 1, bn), lambda l, i: (l, 0, i)),
        ],
        out_specs=pl.BlockSpec((None, MOD_ROWS, bn), lambda l, i: (l, 0, i)),
        out_shape=jax.ShapeDtypeStruct((depth, MOD_ROWS, n), jnp.float32),
        compiler_params=_cparams(2),
        name="modulation",
    )(cc, ada_w, ada_b.reshape(depth, 1, n))


def _proj_body(x_ref, ctx_ref, mod_ref, g1_ref, win_ref, qg_ref, wuq_ref, kvg_ref, wk_ref, wvt_ref,
               e_ref, tab_ref, qm_ref, km_ref, vmt_ref, qs_ref, ks_ref, vs_ref, cz_ref, xs_ref):
    j = pl.program_id(0)
    is_ctx = j == pl.num_programs(0) - 1

    @pl.when(is_ctx)
    def _():
        xs_ref[...] = ctx_ref[...]

    @pl.when(jnp.logical_not(is_ctx))
    def _():
        xs_ref[...] = x_ref[...]

    d = D_MODEL
    sh1 = mod_ref[:, 0:d]
    sc1 = mod_ref[:, d:2 * d]
    h = (_rms(xs_ref[...], g1_ref[...]) * (1.0 + sc1) + sh1).astype(jnp.bfloat16)

    def proj(lo, hi):
        return _dot(h, win_ref[:, lo:hi])

    def tab(lo, n):
        return tab_ref[:, lo:lo + n]

    cq = _rms(proj(_C_CQ, _C_CKV), qg_ref[...]).astype(jnp.bfloat16)
    uq = _dot(cq, wuq_ref[...])
    nq = MLA_HEADS * HEAD_PAD
    qm_ref[...] = (uq[:, :nq] * tab(_T_CQ, nq) + uq[:, nq:] * tab(_T_SQ, nq)).astype(jnp.bfloat16)

    ckv = _rms(proj(_C_CKV, _C_KR), kvg_ref[...]).astype(jnp.bfloat16)
    krp = (proj(_C_KR, _C_SQ) * tab(_T_KR, LANES)).astype(jnp.bfloat16)
    km_ref[...] = (_dot(ckv, wk_ref[...]) + _dot(krp, e_ref[...])).astype(jnp.bfloat16)
    vmt_ref[...] = _dot_nt(wvt_ref[...], ckv).astype(jnp.bfloat16)

    nsq = SWA_HEADS * SWA_HEAD_DIM
    lane = lax.broadcasted_iota(jnp.int32, (h.shape[0], LANES), 1)
    first_half = (lane & (SWA_HEAD_DIM // 2)) == 0

    def rotary(x, c_off, s_off):
        half = SWA_HEAD_DIM // 2
        swapped = jnp.where(first_half, pltpu.roll(x, LANES - half, 1), pltpu.roll(x, half, 1))
        return (x * tab(c_off, LANES) + swapped * tab(s_off, LANES)).astype(jnp.bfloat16)

    uq = proj(_C_SQ, _C_SK)
    for c in range(nsq // LANES):
        qs_ref[:, c * LANES:(c + 1) * LANES] = rotary(uq[:, c * LANES:(c + 1) * LANES],
                                                       _T_CS + c * LANES, _T_SS + c * LANES)
    ks_ref[...] = rotary(proj(_C_SK, _C_SV), _T_CK, _T_SK)
    vs_ref[...] = proj(_C_SV, _C_CB).astype(jnp.bfloat16)

    cz_ref[:, 0:CONV_CH] = proj(_C_CB, _C_CC).astype(jnp.bfloat16)
    cz_ref[:, CONV_CH:2 * CONV_CH] = (proj(_C_CC, _C_CU) * proj(_C_CU, IN_W)).astype(jnp.bfloat16)


def _projections(x, ctx, mod_l, lw, tab):
    b, s, d = x.shape
    n_ctx = ctx.shape[1]
    t = n_ctx + s
    nl = s // TILE
    const = lambda j, i: (0, 0)
    row_outs = [MLA_HEADS * HEAD_PAD, MLA_HEADS * HEAD_PAD, None, SWA_HEADS * SWA_HEAD_DIM, LANES, LANES,
                2 * CONV_CH]
    nv = MLA_HEADS * MLA_V
    out_specs = [pl.BlockSpec((None, nv, TILE), lambda j, i: (i, 0, j)) if w is None
                 else pl.BlockSpec((None, TILE, w), lambda j, i: (i, j, 0)) for w in row_outs]
    out_shape = [jax.ShapeDtypeStruct((b, nv, t) if w is None else (b, t, w), jnp.bfloat16) for w in row_outs]
    return pl.pallas_call(
        _proj_body,
        grid=(nl + 1, b),
        in_specs=[
            pl.BlockSpec((None, TILE, d), lambda j, i: (jnp.where(j == nl, 0, i), jnp.minimum(j, nl - 1), 0)),
            pl.BlockSpec((None, TILE, d), lambda j, i: (jnp.where(j == nl, i, 0), 0, 0)),
            pl.BlockSpec((None, 1, N_MOD * d), lambda j, i: (jnp.where(j == nl, b, i), 0, 0)),
            pl.BlockSpec((1, d), const),
            pl.BlockSpec((d, IN_W), const),
            pl.BlockSpec((1, MLA_Q_RANK), const),
            pl.BlockSpec((MLA_Q_RANK, 2 * MLA_HEADS * HEAD_PAD), const),
            pl.BlockSpec((1, MLA_KV_RANK), const),
            pl.BlockSpec((MLA_KV_RANK, MLA_HEADS * HEAD_PAD), const),
            pl.BlockSpec((nv, MLA_KV_RANK), const),
            pl.BlockSpec((LANES, MLA_HEADS * HEAD_PAD), const),
            pl.BlockSpec((TILE, TAB_W), lambda j, i: (j, 0)),
        ],
        out_specs=out_specs,
        out_shape=out_shape,
        scratch_shapes=[pltpu.VMEM((TILE, d), jnp.float32)],
        compiler_params=_cparams(2),
        name="projections",
    )(x, ctx, mod_l, lw["g1"], lw["win"], lw["qg"], lw["wuq"], lw["kvg"], lw["wk"], lw["wvt"], lw["e"], tab)


MLA_TQ = 2048


def _mla_pair(q_ref, k_ref, vt_ref, o_ref, c):
    tq = q_ref.shape[0]
    res = []
    for hh in range(2):
        lo = (2 * c + hh) * HEAD_PAD
        st = _dot_nt(k_ref[:, lo:lo + HEAD_PAD], q_ref[:, lo:lo + HEAD_PAD])
        m = jnp.max(st, axis=0, keepdims=True)
        p = jnp.exp2(st - m)
        l = jnp.sum(p, axis=0, keepdims=True)
        ot = _dot(vt_ref[c * LANES:(c + 1) * LANES, :], p.astype(jnp.bfloat16))
        res.append(ot / l)
    row = lax.broadcasted_iota(jnp.int32, (LANES, tq), 0)
    o_ref[:, c * LANES:(c + 1) * LANES] = jnp.transpose(
        jnp.where(row < MLA_V, res[0], res[1])).astype(jnp.bfloat16)


def _mla_main_body(q_ref, k_ref, vt_ref, o_ref):
    for c in range(MLA_HEADS // 2):
        _mla_pair(q_ref, k_ref, vt_ref, o_ref, c)


def _mla_ctx_body(q_ref, k_ref, vt_ref, o_ref):
    for c in range(MLA_HEADS // 2):
        _mla_pair(q_ref, k_ref, vt_ref, o_ref, c)


def _mla_attention(qm, km, vmt, n_lat):
    b, t, _ = qm.shape
    return pl.pallas_call(
        _mla_main_body,
        grid=(b, n_lat // MLA_TQ),
        in_specs=[
            pl.BlockSpec((None, MLA_TQ, qm.shape[2]), lambda i, j: (i, j, 0)),
            pl.BlockSpec((None, t, km.shape[2]), lambda i, j: (i, 0, 0)),
            pl.BlockSpec((None, vmt.shape[1], t), lambda i, j: (i, 0, 0)),
        ],
        out_specs=pl.BlockSpec((None, MLA_TQ, vmt.shape[1]), lambda i, j: (i, j, 0)),
        out_shape=jax.ShapeDtypeStruct((b, n_lat, vmt.shape[1]), jnp.bfloat16),
        compiler_params=_cparams(2),
        name="mla_attention",
    )(qm, km, vmt)


def _mla_attention_ctx(qm, km, vmt, n_lat):
    b, t, _ = qm.shape
    n_ctx = t - n_lat
    blk = n_lat // n_ctx
    return pl.pallas_call(
        _mla_ctx_body,
        grid=(b,),
        in_specs=[
            pl.BlockSpec((None, n_ctx, qm.shape[2]), lambda i: (i, blk, 0)),
            pl.BlockSpec((None, n_ctx, km.shape[2]), lambda i: (i, blk, 0)),
            pl.BlockSpec((None, vmt.shape[1], n_ctx), lambda i: (i, 0, blk)),
        ],
        out_specs=pl.BlockSpec((None, n_ctx, vmt.shape[1]), lambda i: (i, 0, 0)),
        out_shape=jax.ShapeDtypeStruct((b, n_ctx, vmt.shape[1]), jnp.bfloat16),
        compiler_params=_cparams(1),
        name="mla_attention_ctx",
    )(qm, km, vmt)


def _swa_attend(sink_ref, q_ref, kcat, vcat, valid, o_ref):
    rows = q_ref.shape[0]
    lane = lax.broadcasted_iota(jnp.int32, (rows, LANES), 1)
    lo_half = lane < SWA_HEAD_DIM
    for c in range(SWA_REP):
        q2 = q_ref[:, c * LANES:(c + 1) * LANES]
        res = []
        for g in range(SWA_KV_HEADS):
            keep = lo_half if g == 0 else jnp.logical_not(lo_half)
            qg = jnp.where(keep, q2, jnp.zeros_like(q2))
            s = _dot_nt(qg, kcat)
            if valid is not None:
                s = jnp.where(valid, s, NEG_BIG)
            sk = sink_ref[g * SWA_REP + c]
            m = jnp.maximum(jnp.max(s, axis=-1, keepdims=True), sk)
            p = jnp.exp2(s - m)
            l = jnp.sum(p, axis=-1, keepdims=True) + jnp.exp2(sk - m)
            res.append(_dot(p.astype(jnp.bfloat16), vcat) / l)
        o_ref[:, c * LANES:(c + 1) * LANES] = jnp.where(lo_half, res[0], res[1]).astype(jnp.bfloat16)


SWA_BAND = TILE + 2 * SWA_WINDOW


def _swa_band_start(j, n_lat):
    return jnp.clip(j * TILE - SWA_WINDOW, 0, n_lat - SWA_BAND)


def _swa_main_body(sink_ref, q_ref, k_ref, v_ref, bias_ref, o_ref, *, n_ctx, n_lat):
    ks = pl.multiple_of(_swa_band_start(pl.program_id(1), n_lat), SWA_WINDOW)
    kcat = jnp.concatenate([k_ref[n_lat:n_lat + n_ctx, :], k_ref[pl.ds(ks, SWA_BAND), :]], axis=0)
    vcat = jnp.concatenate([v_ref[n_lat:n_lat + n_ctx, :], v_ref[pl.ds(ks, SWA_BAND), :]], axis=0)
    vt = jnp.transpose(vcat.astype(jnp.float32)).astype(jnp.bfloat16)
    bias = jnp.concatenate([bias_ref[...]] * SWA_REP, axis=1)
    lane = lax.broadcasted_iota(jnp.int32, (TILE, LANES), 1)
    lo_half = lane < SWA_HEAD_DIM
    qcol = lax.broadcasted_iota(jnp.int32, (1, SWA_REP * TILE), 1)
    res = []
    for g in range(SWA_KV_HEADS):
        keep = lo_half if g == 0 else jnp.logical_not(lo_half)
        qg = jnp.concatenate([jnp.where(keep, q_ref[:, c * LANES:(c + 1) * LANES], 0.0).astype(jnp.bfloat16)
                              for c in range(SWA_REP)], axis=0)
        sk = jnp.full((1, SWA_REP * TILE), sink_ref[g * SWA_REP + SWA_REP - 1], jnp.float32)
        for c in range(SWA_REP - 2, -1, -1):
            sk = jnp.where(qcol < (c + 1) * TILE, sink_ref[g * SWA_REP + c], sk)
        st = _dot_nt(kcat, qg) + bias
        m = jnp.maximum(jnp.max(st, axis=0, keepdims=True), sk)
        p = jnp.exp2(st - m)
        l = jnp.sum(p, axis=0, keepdims=True) + jnp.exp2(sk - m)
        res.append(_dot(vt, p.astype(jnp.bfloat16)) / l)
    row = lax.broadcasted_iota(jnp.int32, (LANES, TILE), 0)
    for c in range(SWA_REP):
        pair = jnp.where(row < SWA_HEAD_DIM, res[0][:, c * TILE:(c + 1) * TILE], res[1][:, c * TILE:(c + 1) * TILE])
        o_ref[:, c * LANES:(c + 1) * LANES] = jnp.transpose(pair).astype(jnp.bfloat16)


def _swa_bias(n_ctx, n_lat):
    nt = n_lat // TILE
    r = lax.broadcasted_iota(jnp.int32, (n_ctx + SWA_BAND, TILE), 0)
    q = lax.broadcasted_iota(jnp.int32, (n_ctx + SWA_BAND, TILE), 1)
    out = []
    for j in (0, 1, nt - 1):
        kpos = _swa_band_start(j, n_lat) + r - n_ctx
        valid = (r < n_ctx) | (jnp.abs(j * TILE + q - kpos) <= SWA_WINDOW)
        out.append(jnp.where(valid, 0.0, NEG_BIG).astype(jnp.float32))
    return jnp.stack(out)


def _swa_ctx_body(sink_ref, q_ref, k_ref, v_ref, o_ref):
    _swa_attend(sink_ref, q_ref, k_ref[...], v_ref[...], None, o_ref)


def _swa_attention(sink, qs, ks, vs, bias, n_lat):
    b, t, w = qs.shape
    nt = n_lat // TILE
    assert nt >= 3
    return pl.pallas_call(
        functools.partial(_swa_main_body, n_ctx=t - n_lat, n_lat=n_lat),
        grid=(b, nt),
        in_specs=[
            pl.BlockSpec(memory_space=pltpu.SMEM),
            pl.BlockSpec((None, TILE, w), lambda i, j: (i, j, 0)),
            pl.BlockSpec((None, t, LANES), lambda i, j: (i, 0, 0)),
            pl.BlockSpec((None, t, LANES), lambda i, j: (i, 0, 0)),
            pl.BlockSpec((None,) + bias.shape[1:], lambda i, j: (jnp.where(j == 0, 0, jnp.where(j == nt - 1, 2, 1)), 0, 0)),
        ],
        out_specs=pl.BlockSpec((None, TILE, w), lambda i, j: (i, j, 0)),
        out_shape=jax.ShapeDtypeStruct((b, n_lat, w), jnp.bfloat16),
        compiler_params=_cparams(2),
        name="swa_attention",
    )(sink, qs, ks, vs, bias)


def _swa_attention_ctx(sink, qs, ks, vs, n_lat):
    b, t, w = qs.shape
    n_ctx = t - n_lat
    blk = n_lat // n_ctx
    return pl.pallas_call(
        _swa_ctx_body,
        grid=(b,),
        in_specs=[
            pl.BlockSpec(memory_space=pltpu.SMEM),
            pl.BlockSpec((None, n_ctx, w), lambda i: (i, blk, 0)),
            pl.BlockSpec((None, n_ctx, LANES), lambda i: (i, blk, 0)),
            pl.BlockSpec((None, n_ctx, LANES), lambda i: (i, blk, 0)),
        ],
        out_specs=pl.BlockSpec((None, n_ctx, w), lambda i: (i, 0, 0)),
        out_shape=jax.ShapeDtypeStruct((b, n_ctx, w), jnp.bfloat16),
        compiler_params=_cparams(1),
        name="swa_attention_ctx",
    )(sink, qs, ks, vs)


HALO = 16


def _mix_body(a_ref, b_ref, cz_ref, hp_ref, hn_ref, x_ref, mod_ref, cw_ref, wo_ref, n2_ref, rw_ref,
              x1_ref, aff_ref):
    j = pl.program_id(1)
    d = D_MODEL
    first = j == 0
    last = j == pl.num_programs(1) - 1
    z = cz_ref[:, CONV_CH:2 * CONV_CH].astype(jnp.float32)
    zp = hp_ref[HALO - 1:HALO, CONV_CH:2 * CONV_CH].astype(jnp.float32)
    zn = hn_ref[0:1, CONV_CH:2 * CONV_CH].astype(jnp.float32)
    zp = jnp.where(first, jnp.zeros_like(zp), zp)
    zn = jnp.where(last, jnp.zeros_like(zn), zn)
    row = lax.broadcasted_iota(jnp.int32, z.shape, 0)
    z_dn = jnp.where(row == 0, zp, pltpu.roll(z, 1, 0))
    rows = z.shape[0]
    z_up = jnp.where(row == rows - 1, zn, pltpu.roll(z, rows - 1, 0))
    y = z_dn * cw_ref[0:1, :] + z * cw_ref[1:2, :] + z_up * cw_ref[2:3, :]
    cv = (cz_ref[:, 0:CONV_CH].astype(jnp.float32) * y).astype(jnp.bfloat16)
    mix = jnp.concatenate([a_ref[...], b_ref[...], cv], axis=-1)
    g1 = mod_ref[:, 2 * d:3 * d]
    sh2 = mod_ref[:, 3 * d:4 * d]
    sc2 = mod_ref[:, 4 * d:5 * d]
    x1 = x_ref[...] + g1 * _dot(mix, wo_ref[...])
    x1_ref[...] = x1
    h2 = (_rms(x1, n2_ref[...]) * (1.0 + sc2) + sh2).astype(jnp.bfloat16)
    lg = _dot_nt(rw_ref[...], h2)
    ex = jnp.exp(lg - jnp.max(lg, axis=0, keepdims=True))
    aff_ref[...] = ex / jnp.sum(ex, axis=0, keepdims=True)


MIX_TILE = 512


def _mixer_out(a, bsw, cz, x, mod_l, mod_row0, lw, frame_row0):
    b, n, d = x.shape
    t = cz.shape[1]
    tm = min(MIX_TILE, n)
    assert n % tm == 0 and frame_row0 % tm == 0
    tile0 = frame_row0 // tm
    hb = tm // HALO
    here = lambda i, j: (i, j, 0)
    const = lambda i, j: (0, 0)
    if mod_row0 is None:
        mod_map = lambda i, j: (i, 0, 0)
    else:
        mod_map = lambda i, j: (mod_row0, 0, 0)
    return pl.pallas_call(
        _mix_body,
        grid=(b, n // tm),
        in_specs=[
            pl.BlockSpec((None, tm, a.shape[2]), here),
            pl.BlockSpec((None, tm, bsw.shape[2]), here),
            pl.BlockSpec((None, tm, cz.shape[2]), lambda i, j: (i, j + tile0, 0)),
            pl.BlockSpec((None, HALO, cz.shape[2]),
                         lambda i, j: (i, jnp.maximum((j + tile0) * hb - 1, 0), 0)),
            pl.BlockSpec((None, HALO, cz.shape[2]),
                         lambda i, j: (i, jnp.minimum((j + tile0 + 1) * hb, t // HALO - 1), 0)),
            pl.BlockSpec((None, tm, d), here),
            pl.BlockSpec((None, 1, N_MOD * d), mod_map),
            pl.BlockSpec((3, CONV_CH), const),
            pl.BlockSpec((d, d), const),
            pl.BlockSpec((1, d), const),
            pl.BlockSpec((N_EXPERTS, d), const),
        ],
        out_specs=[pl.BlockSpec((None, tm, d), here),
                   pl.BlockSpec((None, N_EXPERTS, tm), lambda i, j: (i, 0, j))],
        out_shape=[jax.ShapeDtypeStruct((b, n, d), jnp.float32),
                   jax.ShapeDtypeStruct((b, N_EXPERTS, n), jnp.float32)],
        compiler_params=_cparams(2),
        name="mixer_out",
    )(a, bsw, cz, cz, cz, x, mod_l, lw["cw"], lw["wo"], lw["n2"], lw["rw"])


ROUTE_ROWS = 128
ROUTE_SPLIT = 64


def _route_body(aff_ref, tri_ref, idx_ref, taff_ref, pos_ref, parts_ref, *, cap):
    rows, n = aff_ref.shape
    aff = aff_ref[...]
    bits = pltpu.bitcast(aff, jnp.int32)

    def count(mask):
        return jnp.sum(jnp.where(mask, 1.0, 0.0), axis=1, keepdims=True)

    def ones(mask):
        return jnp.where(mask, 1.0, 0.0).astype(jnp.bfloat16)

    def search(i, thr):
        cand = thr | (1 << (30 - i))
        return jnp.where(count(bits >= cand) >= cap, cand, thr)

    thr = lax.fori_loop(0, 31, search, jnp.zeros((rows, 1), jnp.int32))
    gt = bits > thr
    eq = bits == thr
    need = cap - count(gt)
    peq = _dot(ones(eq), tri_ref[...])
    sel = gt | (eq & (peq < need))
    pos = _dot(ones(sel), tri_ref[...])
    pos_ref[...] = jnp.where(sel, pos, -1.0)

    a1 = aff.astype(jnp.bfloat16).astype(jnp.float32)
    a2 = (aff - a1).astype(jnp.bfloat16).astype(jnp.float32)
    parts_ref[0] = a1
    parts_ref[1] = a2
    parts_ref[2] = aff - a1 - a2
    tok = lax.broadcasted_iota(jnp.int32, (1, n), 1)
    tok_hi = (tok // ROUTE_SPLIT).astype(jnp.float32)
    tok_lo = (tok % ROUTE_SPLIT).astype(jnp.float32)
    sub = lax.broadcasted_iota(jnp.int32, (SUBLANES, n), 0)
    slot = lax.broadcasted_iota(jnp.int32, (cap, 1), 0).astype(jnp.float32)

    def per_row(r, carry):
        onehot = jnp.where(pos_ref[pl.ds(r, 1), :] == slot, 1.0, 0.0).astype(jnp.bfloat16)
        vals = jnp.where(sub == 0, tok_hi, jnp.where(sub == 1, tok_lo, 0.0))
        for k in range(3):
            vals = jnp.where(sub == 2 + k, parts_ref[k, pl.ds(r, 1), :], vals)
        got = _dot_nt(vals.astype(jnp.bfloat16), onehot)
        idx_ref[pl.ds(r, 1), :] = (got[0:1, :] * ROUTE_SPLIT + got[1:2, :]).astype(jnp.int32)
        taff_ref[pl.ds(r, 1), :] = got[2:3, :] + got[3:4, :] + got[4:5, :]
        return carry

    lax.fori_loop(0, rows, per_row, 0)


def _routing(aff, tri, cap):
    b, e, n = aff.shape
    rows = b * e
    rr = min(ROUTE_ROWS, rows)
    idx, taff = pl.pallas_call(
        functools.partial(_route_body, cap=cap),
        grid=(rows // rr,),
        in_specs=[pl.BlockSpec((rr, n), lambda i: (i, 0)),
                  pl.BlockSpec((n, n), lambda i: (0, 0))],
        out_specs=[pl.BlockSpec((rr, cap), lambda i: (i, 0)),
                   pl.BlockSpec((rr, cap), lambda i: (i, 0))],
        out_shape=[jax.ShapeDtypeStruct((rows, cap), jnp.int32),
                   jax.ShapeDtypeStruct((rows, cap), jnp.float32)],
        scratch_shapes=[pltpu.VMEM((rr, n), jnp.float32),
                        pltpu.VMEM((3, rr, n), jnp.float32)],
        compiler_params=_cparams(1),
        name="routing",
    )(aff.reshape(rows, n), tri)
    return idx.reshape(b, e, cap), taff.reshape(b, e, cap)


EXP_PER_STEP = 2
MOE_CHUNK = 1024
SCATTER_GROUP = 16


def _moe_body(idx_ref, taff_ref, x_ref, mod_ref, n2_ref, fg_ref, wg_ref, wu_ref, wd_ref, o_ref,
              h_ref, acc_ref, *bufs, nch, cap, final):
    j = pl.program_id(1)
    d = D_MODEL
    nsl = d // LANES
    chunk = x_ref.shape[0]
    rows = chunk * nsl
    nes = N_EXPERTS // EXP_PER_STEP
    xg_refs = bufs[:EXP_PER_STEP]
    yb_refs = bufs[EXP_PER_STEP:]

    @pl.when(j < nch)
    def _():
        sh2 = mod_ref[:, 3 * d:4 * d]
        sc2 = mod_ref[:, 4 * d:5 * d]
        h2 = _rms(x_ref[...], n2_ref[...]) * (1.0 + sc2) + sh2
        base = pl.multiple_of(j * rows, rows)
        for s in range(nsl):
            h_ref[pl.ds(base + s, chunk, stride=nsl), :] = h2[:, s * LANES:(s + 1) * LANES]
        acc_ref[pl.ds(base, rows), :] = jnp.zeros((rows, LANES), jnp.float32)

    @pl.when((j >= nch) & (j < nch + nes))
    def _():
        tcol = jnp.transpose(taff_ref[...])
        for k in range(EXP_PER_STEP):
            xg_ref, yb_ref = xg_refs[k], yb_refs[k]
            tok = [pl.multiple_of(idx_ref[0, k * cap + jj], nsl) for jj in range(cap)]
            for jj in range(cap):
                xg_ref[jj * nsl:(jj + 1) * nsl, :] = h_ref[pl.ds(tok[jj], nsl), :]
            xs = jnp.concatenate([xg_ref[pl.ds(s, cap, stride=nsl), :] for s in range(nsl)], axis=-1)
            xs = xs.astype(jnp.bfloat16)
            gate = _dot(xs, wg_ref[k])
            up = _dot(xs, wu_ref[k])
            hid = (_silu(gate) * up).astype(jnp.bfloat16)
            y = _dot(hid, wd_ref[k]) * tcol[k * cap:(k + 1) * cap, 0:1]
            for s in range(nsl):
                yb_ref[pl.ds(s, cap, stride=nsl), :] = y[:, s * LANES:(s + 1) * LANES]
            for g0 in range(0, cap, SCATTER_GROUP):
                grp = range(g0, g0 + SCATTER_GROUP)
                vals = [acc_ref[pl.ds(tok[jj], nsl), :] + yb_ref[jj * nsl:(jj + 1) * nsl, :] for jj in grp]
                for jj, v in zip(grp, vals):
                    acc_ref[pl.ds(tok[jj], nsl), :] = v

    @pl.when(j >= nch + nes)
    def _():
        c = j - nch - nes
        base = pl.multiple_of(c * rows, rows)
        m = jnp.concatenate([acc_ref[pl.ds(base + s, chunk, stride=nsl), :] for s in range(nsl)], axis=-1)
        g2 = mod_ref[:, 5 * d:6 * d]
        x2 = x_ref[...] + g2 * m
        if final:
            x2 = _rms(x2, fg_ref[...])
        o_ref[...] = x2


def _experts(x1, idx, taff, mod_l, mod_row0, lw, final_g, final):
    nb, n_tok, d = x1.shape
    cap = idx.shape[2]
    chunk = MOE_CHUNK if n_tok % MOE_CHUNK == 0 else MOE_CHUNK // 2
    assert n_tok % chunk == 0 and cap % SCATTER_GROUP == 0
    nch = n_tok // chunk
    nes = N_EXPERTS // EXP_PER_STEP
    steps = 2 * nch + nes
    nsl = d // LANES
    idx = (idx * nsl).reshape(nb, nes, 1, EXP_PER_STEP * cap)
    taff = jnp.broadcast_to(taff.reshape(nb, nes, 1, EXP_PER_STEP * cap), (nb, nes, SUBLANES, EXP_PER_STEP * cap))

    def chunk_in(i, j):
        c = jnp.where(j < nch, j, jnp.where(j < nch + nes, nch - 1, j - nch - nes))
        return (i, c, 0)

    def chunk_out(i, j):
        return (i, jnp.clip(j - nch - nes, 0, nch - 1), 0)

    def slots(i, j):
        return (i, jnp.clip(j - nch, 0, nes - 1), 0, 0)

    def expert(i, j):
        return (jnp.clip(j - nch, 0, nes - 1), 0, 0)

    const = lambda i, j: (0, 0)
    if mod_row0 is None:
        mod_map = lambda i, j: (i, 0, 0)
    else:
        mod_map = lambda i, j: (mod_row0, 0, 0)
    slot_buf = pltpu.VMEM((cap * nsl, LANES), jnp.float32)
    return pl.pallas_call(
        functools.partial(_moe_body, nch=nch, cap=cap, final=final),
        grid=(nb, steps),
        in_specs=[
            pl.BlockSpec((None, None, 1, EXP_PER_STEP * cap), slots, memory_space=pltpu.SMEM),
            pl.BlockSpec((None, None, SUBLANES, EXP_PER_STEP * cap), slots),
            pl.BlockSpec((None, chunk, d), chunk_in),
            pl.BlockSpec((None, 1, N_MOD * d), mod_map),
            pl.BlockSpec((1, d), const),
            pl.BlockSpec((1, d), const),
            pl.BlockSpec((EXP_PER_STEP, d, EXPERT_FF), expert),
            pl.BlockSpec((EXP_PER_STEP, d, EXPERT_FF), expert),
            pl.BlockSpec((EXP_PER_STEP, EXPERT_FF, d), expert),
        ],
        out_specs=pl.BlockSpec((None, chunk, d), chunk_out),
        out_shape=jax.ShapeDtypeStruct((nb, n_tok, d), jnp.float32),
        scratch_shapes=[pltpu.VMEM((n_tok * nsl, LANES), jnp.float32),
                        pltpu.VMEM((n_tok * nsl, LANES), jnp.float32)] + [slot_buf] * (2 * EXP_PER_STEP),
        compiler_params=_cparams(2),
        name="experts",
    )(idx, taff, x1, mod_l, lw["n2"], final_g, lw["wg"], lw["wu"], lw["wd"])


def _group_samples(x1, idx, taff):
    b, n_tok, d = x1.shape
    cap = idx.shape[2]
    g = max(k for k in range(1, 9) if b % k == 0 and (k * n_tok) % (MOE_CHUNK // 2) == 0)
    off = (jnp.arange(b, dtype=jnp.int32) % g) * n_tok
    idx = idx + off[:, None, None]

    def merge(t):
        t = t.reshape(b // g, g, N_EXPERTS, cap)
        return jnp.swapaxes(t, 1, 2).reshape(b // g, N_EXPERTS, g * cap)

    return x1.reshape(b // g, g * n_tok, d), merge(idx), merge(taff)


def _swap_halves(w):
    half = w.shape[-1] // 2
    return jnp.concatenate([-w[..., half:], w[..., :half]], axis=-1)


def _swa_perm():
    cols = []
    for c in range(SWA_REP):
        for g in range(SWA_KV_HEADS):
            h = g * SWA_REP + c
            cols.extend(range(h * SWA_HEAD_DIM, (h + 1) * SWA_HEAD_DIM))
    return np.asarray(cols, np.int32)


def _layer_weights(w_in, q_g, w_uq, kv_g, w_ukv, conv_w, w_o, n1, n2, router_w, wg, wu, wd):
    bf = jnp.bfloat16
    d = w_in.shape[0]
    offs = np.cumsum([0, MLA_Q_RANK, MLA_KV_RANK, MLA_ROPE, SWA_HEADS * SWA_HEAD_DIM,
                      SWA_KV_HEADS * SWA_HEAD_DIM, SWA_KV_HEADS * SWA_HEAD_DIM, CONV_CH, CONV_CH, CONV_CH])
    part = [w_in[:, offs[i]:offs[i + 1]] for i in range(9)]
    w_cq, w_ckv, w_kr, w_sq, w_sk, w_sv, w_cb, w_cc, w_cu = part
    perm = _swa_perm()

    kr_block = jnp.concatenate([w_kr, _swap_halves(w_kr), jnp.zeros((d, LANES - 2 * MLA_ROPE), w_in.dtype)], axis=1)
    win = jnp.concatenate([
        w_cq, w_ckv, kr_block,
        w_sq[:, perm], w_sk, w_sv, w_cb, w_cc, w_cu], axis=1).astype(bf)

    qd = MLA_NOPE + MLA_ROPE
    uq = w_uq.reshape(MLA_Q_RANK, MLA_HEADS, qd)
    zpad = jnp.zeros((MLA_Q_RANK, MLA_HEADS, HEAD_PAD - qd), w_uq.dtype)
    main = jnp.concatenate([uq, zpad], axis=-1)
    swp = jnp.concatenate([jnp.zeros_like(uq[..., :MLA_NOPE]), _swap_halves(uq[..., MLA_NOPE:]), zpad], axis=-1)
    wuq = jnp.concatenate([main.reshape(MLA_Q_RANK, -1), swp.reshape(MLA_Q_RANK, -1)], axis=1).astype(bf)

    ukv = w_ukv.reshape(MLA_KV_RANK, MLA_HEADS, MLA_NOPE + MLA_V)
    wk = jnp.concatenate([ukv[..., :MLA_NOPE],
                          jnp.zeros((MLA_KV_RANK, MLA_HEADS, HEAD_PAD - MLA_NOPE), w_ukv.dtype)], axis=-1)
    wk = wk.reshape(MLA_KV_RANK, -1).astype(bf)
    wv = ukv[..., MLA_NOPE:].reshape(MLA_KV_RANK, -1).astype(bf)

    e = np.zeros((LANES, MLA_HEADS * HEAD_PAD), np.float32)
    for h in range(MLA_HEADS):
        for l in range(MLA_ROPE):
            e[l, h * HEAD_PAD + MLA_NOPE + l] = 1.0
            e[MLA_ROPE + l, h * HEAD_PAD + MLA_NOPE + l] = 1.0

    n_mla = MLA_HEADS * MLA_V
    n_swa = SWA_HEADS * SWA_HEAD_DIM
    wo = jnp.concatenate([w_o[:n_mla], w_o[n_mla:n_mla + n_swa][perm], w_o[n_mla + n_swa:]], axis=0).astype(bf)
    return dict(
        g1=n1.reshape(1, -1), win=win, qg=q_g.reshape(1, -1), wuq=wuq, kvg=kv_g.reshape(1, -1),
        wk=wk, wvt=wv.T, e=jnp.asarray(e, bf), cw=conv_w, wo=wo, n2=n2.reshape(1, -1),
        rw=router_w.T.astype(bf), wg=wg.astype(bf), wu=wu.astype(bf), wd=wd.astype(bf))


def _axial_tables(n_tokens, rot_dim):
    rows = n_tokens // GRID_W
    row = jnp.repeat(jnp.arange(rows, dtype=jnp.float32), GRID_W)
    col = jnp.tile(jnp.arange(GRID_W, dtype=jnp.float32), rows)
    n_freq = rot_dim // 4
    inv = ROPE_BASE ** (-jnp.arange(n_freq, dtype=jnp.float32) / n_freq)
    ang = jnp.concatenate([row[:, None] * inv, col[:, None] * inv], axis=-1)
    return jnp.cos(ang), jnp.sin(ang)


def _row_table(n_ctx, n_lat):
    def with_ctx(cos, sin):
        one = jnp.ones((n_ctx, cos.shape[1]), jnp.float32)
        return (jnp.concatenate([cos, one], axis=0), jnp.concatenate([sin, 0.0 * one], axis=0))

    cm, sm = with_ctx(*_axial_tables(n_lat, MLA_ROPE))
    cs, ss = with_ctx(*_axial_tables(n_lat, SWA_HEAD_DIM))
    t = n_ctx + n_lat
    cm2 = jnp.concatenate([cm, cm], axis=1)
    sm2 = jnp.concatenate([sm, sm], axis=1)
    cs2 = jnp.concatenate([cs, cs], axis=1)
    ss2 = jnp.concatenate([-ss, ss], axis=1)
    pad = jnp.zeros((t, HEAD_PAD - MLA_NOPE - MLA_ROPE), jnp.float32)
    cq = jnp.concatenate([jnp.ones((t, MLA_NOPE), jnp.float32), cm2, pad], axis=1) * (MLA_SCALE * LOG2E)
    sq = jnp.concatenate([jnp.zeros((t, MLA_NOPE), jnp.float32), sm2, pad], axis=1) * (MLA_SCALE * LOG2E)
    kr = jnp.concatenate([cm2, sm2, jnp.zeros((t, LANES - 2 * MLA_ROPE), jnp.float32)], axis=1)
    tab = jnp.concatenate([
        jnp.tile(cq, (1, MLA_HEADS)), jnp.tile(sq, (1, MLA_HEADS)), kr,
        jnp.tile(cs2, (1, SWA_HEADS)) * (SWA_SCALE * LOG2E), jnp.tile(ss2, (1, SWA_HEADS)) * (SWA_SCALE * LOG2E),
        jnp.tile(cs2, (1, SWA_KV_HEADS)), jnp.tile(ss2, (1, SWA_KV_HEADS))], axis=1)
    return tab


def _strict_upper(n):
    r = lax.broadcasted_iota(jnp.int32, (n, n), 0)
    c = lax.broadcasted_iota(jnp.int32, (n, n), 1)
    return (r < c).astype(jnp.bfloat16)


def kernel(x, c, ctx, c_ctx, ada_w, ada_b, norm1_g, w_in, mla_q_norm_g, mla_w_uq, mla_kv_norm_g, mla_w_ukv,
           swa_sink, conv_w, w_o, norm2_g, router_w, exp_w_gate, exp_w_up, exp_w_down, final_norm_g):
    b, s, d = x.shape
    n_ctx = ctx.shape[1]
    depth = ada_w.shape[0]
    assert d == D_MODEL and s % MLA_TQ == 0 and n_ctx == TILE and b + 1 <= MOD_ROWS
    cap_lat = EC_CAPACITY_FACTOR * s // N_EXPERTS
    cap_ctx = EC_CAPACITY_FACTOR * n_ctx // N_EXPERTS

    cc = jnp.concatenate([c, c_ctx[None, :], jnp.zeros((MOD_ROWS - b - 1, d), c.dtype)], axis=0)
    mod = _modulation(cc, ada_w, ada_b).reshape(depth, MOD_ROWS, 1, N_MOD * d)
    tab = _row_table(n_ctx, s)
    tri_lat = _strict_upper(s)
    tri_ctx = _strict_upper(n_ctx)
    swa_bias = _swa_bias(n_ctx, s)
    sink_slots = jnp.zeros((8,), jnp.float32)
    fg = final_norm_g.reshape(1, d)

    xl, xc = x, ctx
    for li in range(depth):
        last = li == depth - 1
        lw = _layer_weights(w_in[li], mla_q_norm_g[li], mla_w_uq[li], mla_kv_norm_g[li], mla_w_ukv[li],
                            conv_w[li], w_o[li], norm1_g[li], norm2_g[li], router_w[li],
                            exp_w_gate[li], exp_w_up[li], exp_w_down[li])
        mod_l = mod[li]
        sink = sink_slots.at[:SWA_HEADS].set(swa_sink[li] * LOG2E)
        qm, km, vmt, qs, ks, vs, cz = _projections(xl, xc, mod_l, lw, tab)
        a = _mla_attention(qm, km, vmt, s)
        bsw = _swa_attention(sink, qs, ks, vs, swa_bias, s)
        x1, aff = _mixer_out(a, bsw, cz, xl, mod_l, None, lw, 0)
        idx, taff = _routing(aff, tri_lat, cap_lat)
        if not last:
            ac = _mla_attention_ctx(qm, km, vmt, s)
            bc = _swa_attention_ctx(sink, qs, ks, vs, s)
            xc1, affc = _mixer_out(ac, bc, cz, xc, mod_l, b, lw, s)
            idxc, taffc = _routing(affc, tri_ctx, cap_ctx)
            xc = _experts(*_group_samples(xc1, idxc, taffc), mod_l, b, lw, fg, final=False).reshape(xc1.shape)
        xl = _experts(x1, idx, taff, mod_l, None, lw, fg, final=last)
    return xl
```

```python
import functools

import jax
import jax.numpy as jnp
import numpy as np
from jax import lax
from jax.experimental import pallas as pl
from jax.experimental.pallas import tpu as pltpu

D_MODEL = 1024
GRID_W = 64
NORM_EPS = 1e-6
ROPE_BASE = 10000.0

MLA_HEADS = 6
MLA_Q_RANK = 256
MLA_KV_RANK = 128
MLA_NOPE = 64
MLA_ROPE = 32
MLA_V = 64
MLA_SCALE = (MLA_NOPE + MLA_ROPE) ** -0.5

SWA_HEADS = 6
SWA_KV_HEADS = 2
SWA_REP = SWA_HEADS // SWA_KV_HEADS
SWA_HEAD_DIM = 64
SWA_WINDOW = 128
SWA_SCALE = SWA_HEAD_DIM ** -0.5

CONV_CH = 256
N_EXPERTS = 16
EXPERT_FF = 512
EC_CAPACITY_FACTOR = 2
N_MOD = 6

LANES = 128
SUBLANES = 8
TILE = 256
HEAD_PAD = 128
MOD_ROWS = 40
NEG_BIG = -1e30
LOG2E = 1.4426950408889634

_C_CQ = 0
_C_CKV = 256
_C_KR = 384
_C_SQ = 512
_C_SK = 896
_C_SV = 1024
_C_CB = 1152
_C_CC = 1408
_C_CU = 1664
IN_W = 1920

_T_CQ = 0
_T_SQ = 768
_T_KR = 1536
_T_CS = 1664
_T_SS = 2048
_T_CK = 2432
_T_SK = 2560
TAB_W = 2688

_VMEM_LIMIT = 56 * 1024 * 1024


def _cparams(n_grid):
    return pltpu.CompilerParams(dimension_semantics=("arbitrary",) * n_grid,
                                vmem_limit_bytes=_VMEM_LIMIT)


def _silu(v):
    return v * (1.0 / (1.0 + jnp.exp(-v)))


def _rms(v, g):
    return v * lax.rsqrt(jnp.mean(v * v, axis=-1, keepdims=True) + NORM_EPS) * g


def _dot(a, b):
    return jnp.dot(a, b, preferred_element_type=jnp.float32)


def _dot_nt(a, b):
    return lax.dot_general(a, b, (((1,), (1,)), ((), ())), preferred_element_type=jnp.float32)


def _mod_body(c_ref, w_ref, b_ref, o_ref):
    a = _silu(c_ref[...]).astype(jnp.bfloat16)
    o_ref[...] = _dot(a, w_ref[...].astype(jnp.bfloat16)) + b_ref[...]


def _modulation(cc, ada_w, ada_b):
    depth, d, n = ada_w.shape
    bn = 512
    return pl.pallas_call(
        _mod_body,
        grid=(depth, n // bn),
        in_specs=[
            pl.BlockSpec((MOD_ROWS, d), lambda l, i: (0, 0)),
            pl.BlockSpec((None, d, bn), lambda l, i: (l, 0, i)),
            pl.BlockSpec((None, 1, bn), lambda l, i: (l, 0, i)),
        ],
        out_specs=pl.BlockSpec((None, MOD_ROWS, bn), lambda l, i: (l, 0, i)),
        out_shape=jax.ShapeDtypeStruct((depth, MOD_ROWS, n), jnp.float32),
        compiler_params=_cparams(2),
        name="modulation",
    )(cc, ada_w, ada_b.reshape(depth, 1, n))


def _proj_body(x_ref, ctx_ref, mod_ref, g1_ref, win_ref, qg_ref, wuq_ref, kvg_ref, wk_ref, wvt_ref,
               e_ref, tab_ref, qm_ref, km_ref, vmt_ref, qs_ref, ks_ref, vs_ref, cz_ref, xs_ref):
    j = pl.program_id(0)
    is_ctx = j == pl.num_programs(0) - 1

    @pl.when(is_ctx)
    def _():
        xs_ref[...] = ctx_ref[...]

    @pl.when(jnp.logical_not(is_ctx))
    def _():
        xs_ref[...] = x_ref[...]

    d = D_MODEL
    sh1 = mod_ref[:, 0:d]
    sc1 = mod_ref[:, d:2 * d]
    h = (_rms(xs_ref[...], g1_ref[...]) * (1.0 + sc1) + sh1).astype(jnp.bfloat16)

    def proj(lo, hi):
        return _dot(h, win_ref[:, lo:hi])

    def tab(lo, n):
        return tab_ref[:, lo:lo + n]

    cq = _rms(proj(_C_CQ, _C_CKV), qg_ref[...]).astype(jnp.bfloat16)
    uq = _dot(cq, wuq_ref[...])
    nq = MLA_HEADS * HEAD_PAD
    qm_ref[...] = (uq[:, :nq] * tab(_T_CQ, nq) + uq[:, nq:] * tab(_T_SQ, nq)).astype(jnp.bfloat16)

    ckv = _rms(proj(_C_CKV, _C_KR), kvg_ref[...]).astype(jnp.bfloat16)
    krp = (proj(_C_KR, _C_SQ) * tab(_T_KR, LANES)).astype(jnp.bfloat16)
    km_ref[...] = (_dot(ckv, wk_ref[...]) + _dot(krp, e_ref[...])).astype(jnp.bfloat16)
    vmt_ref[...] = _dot_nt(wvt_ref[...], ckv).astype(jnp.bfloat16)

    nsq = SWA_HEADS * SWA_HEAD_DIM
    lane = lax.broadcasted_iota(jnp.int32, (h.shape[0], LANES), 1)
    first_half = (lane & (SWA_HEAD_DIM // 2)) == 0

    def rotary(x, c_off, s_off):
        half = SWA_HEAD_DIM // 2
        swapped = jnp.where(first_half, pltpu.roll(x, LANES - half, 1), pltpu.roll(x, half, 1))
        return (x * tab(c_off, LANES) + swapped * tab(s_off, LANES)).astype(jnp.bfloat16)

    uq = proj(_C_SQ, _C_SK)
    for c in range(nsq // LANES):
        qs_ref[:, c * LANES:(c + 1) * LANES] = rotary(uq[:, c * LANES:(c + 1) * LANES],
                                                       _T_CS + c * LANES, _T_SS + c * LANES)
    ks_ref[...] = rotary(proj(_C_SK, _C_SV), _T_CK, _T_SK)
    vs_ref[...] = proj(_C_SV, _C_CB).astype(jnp.bfloat16)

    cz_ref[:, 0:CONV_CH] = proj(_C_CB, _C_CC).astype(jnp.bfloat16)
    cz_ref[:, CONV_CH:2 * CONV_CH] = (proj(_C_CC, _C_CU) * proj(_C_CU, IN_W)).astype(jnp.bfloat16)


def _projections(x, ctx, mod_l, lw, tab):
    b, s, d = x.shape
    n_ctx = ctx.shape[1]
    t = n_ctx + s
    nl = s // TILE
    const = lambda j, i: (0, 0)
    row_outs = [MLA_HEADS * HEAD_PAD, MLA_HEADS * HEAD_PAD, None, SWA_HEADS * SWA_HEAD_DIM, LANES, LANES,
                2 * CONV_CH]
    nv = MLA_HEADS * MLA_V
    out_specs = [pl.BlockSpec((None, nv, TILE), lambda j, i: (i, 0, j)) if w is None
                 else pl.BlockSpec((None, TILE, w), lambda j, i: (i, j, 0)) for w in row_outs]
    out_shape = [jax.ShapeDtypeStruct((b, nv, t) if w is None else (b, t, w), jnp.bfloat16) for w in row_outs]
    return pl.pallas_call(
        _proj_body,
        grid=(nl + 1, b),
        in_specs=[
            pl.BlockSpec((None, TILE, d), lambda j, i: (jnp.where(j == nl, 0, i), jnp.minimum(j, nl - 1), 0)),
            pl.BlockSpec((None, TILE, d), lambda j, i: (jnp.where(j == nl, i, 0), 0, 0)),
            pl.BlockSpec((None, 1, N_MOD * d), lambda j, i: (jnp.where(j == nl, b, i), 0, 0)),
            pl.BlockSpec((1, d), const),
            pl.BlockSpec((d, IN_W), const),
            pl.BlockSpec((1, MLA_Q_RANK), const),
            pl.BlockSpec((MLA_Q_RANK, 2 * MLA_HEADS * HEAD_PAD), const),
            pl.BlockSpec((1, MLA_KV_RANK), const),
            pl.BlockSpec((MLA_KV_RANK, MLA_HEADS * HEAD_PAD), const),
            pl.BlockSpec((nv, MLA_KV_RANK), const),
            pl.BlockSpec((LANES, MLA_HEADS * HEAD_PAD), const),
            pl.BlockSpec((TILE, TAB_W), lambda j, i: (j, 0)),
        ],
        out_specs=out_specs,
        out_shape=out_shape,
        scratch_shapes=[pltpu.VMEM((TILE, d), jnp.float32)],
        compiler_params=_cparams(2),
        name="projections",
    )(x, ctx, mod_l, lw["g1"], lw["win"], lw["qg"], lw["wuq"], lw["kvg"], lw["wk"], lw["wvt"], lw["e"], tab)


MLA_TQ = 2048


def _mla_pair(q_ref, k_ref, vt_ref, o_ref, c):
    tq = q_ref.shape[0]
    res = []
    for hh in range(2):
        lo = (2 * c + hh) * HEAD_PAD
        st = _dot_nt(k_ref[:, lo:lo + HEAD_PAD], q_ref[:, lo:lo + HEAD_PAD])
        m = jnp.max(st, axis=0, keepdims=True)
        p = jnp.exp2(st - m)
        l = jnp.sum(p, axis=0, keepdims=True)
        ot = _dot(vt_ref[c * LANES:(c + 1) * LANES, :], p.astype(jnp.bfloat16))
        res.append(ot / l)
    row = lax.broadcasted_iota(jnp.int32, (LANES, tq), 0)
    o_ref[:, c * LANES:(c + 1) * LANES] = jnp.transpose(
        jnp.where(row < MLA_V, res[0], res[1])).astype(jnp.bfloat16)


def _mla_main_body(q_ref, k_ref, vt_ref, o_ref):
    for c in range(MLA_HEADS // 2):
        _mla_pair(q_ref, k_ref, vt_ref, o_ref, c)


def _mla_ctx_body(q_ref, k_ref, vt_ref, o_ref):
    for c in range(MLA_HEADS // 2):
        _mla_pair(q_ref, k_ref, vt_ref, o_ref, c)


def _mla_attention(qm, km, vmt, n_lat):
    b, t, _ = qm.shape
    return pl.pallas_call(
        _mla_main_body,
        grid=(b, n_lat // MLA_TQ),
        in_specs=[
            pl.BlockSpec((None, MLA_TQ, qm.shape[2]), lambda i, j: (i, j, 0)),
            pl.BlockSpec((None, t, km.shape[2]), lambda i, j: (i, 0, 0)),
            pl.BlockSpec((None, vmt.shape[1], t), lambda i, j: (i, 0, 0)),
        ],
        out_specs=pl.BlockSpec((None, MLA_TQ, vmt.shape[1]), lambda i, j: (i, j, 0)),
        out_shape=jax.ShapeDtypeStruct((b, n_lat, vmt.shape[1]), jnp.bfloat16),
        compiler_params=_cparams(2),
        name="mla_attention",
    )(qm, km, vmt)


def _mla_attention_ctx(qm, km, vmt, n_lat):
    b, t, _ = qm.shape
    n_ctx = t - n_lat
    blk = n_lat // n_ctx
    return pl.pallas_call(
        _mla_ctx_body,
        grid=(b,),
        in_specs=[
            pl.BlockSpec((None, n_ctx, qm.shape[2]), lambda i: (i, blk, 0)),
            pl.BlockSpec((None, n_ctx, km.shape[2]), lambda i: (i, blk, 0)),
            pl.BlockSpec((None, vmt.shape[1], n_ctx), lambda i: (i, 0, blk)),
        ],
        out_specs=pl.BlockSpec((None, n_ctx, vmt.shape[1]), lambda i: (i, 0, 0)),
        out_shape=jax.ShapeDtypeStruct((b, n_ctx, vmt.shape[1]), jnp.bfloat16),
        compiler_params=_cparams(1),
        name="mla_attention_ctx",
    )(qm, km, vmt)


def _swa_attend(sink_ref, q_ref, kcat, vcat, valid, o_ref):
    rows = q_ref.shape[0]
    lane = lax.broadcasted_iota(jnp.int32, (rows, LANES), 1)
    lo_half = lane < SWA_HEAD_DIM
    for c in range(SWA_REP):
        q2 = q_ref[:, c * LANES:(c + 1) * LANES]
        res = []
        for g in range(SWA_KV_HEADS):
            keep = lo_half if g == 0 else jnp.logical_not(lo_half)
            qg = jnp.where(keep, q2, jnp.zeros_like(q2))
            s = _dot_nt(qg, kcat)
            if valid is not None:
                s = jnp.where(valid, s, NEG_BIG)
            sk = sink_ref[g * SWA_REP + c]
            m = jnp.maximum(jnp.max(s, axis=-1, keepdims=True), sk)
            p = jnp.exp2(s - m)
            l = jnp.sum(p, axis=-1, keepdims=True) + jnp.exp2(sk - m)
            res.append(_dot(p.astype(jnp.bfloat16), vcat) / l)
        o_ref[:, c * LANES:(c + 1) * LANES] = jnp.where(lo_half, res[0], res[1]).astype(jnp.bfloat16)


SWA_BAND = TILE + 2 * SWA_WINDOW


def _swa_band_start(j, n_lat):
    return jnp.clip(j * TILE - SWA_WINDOW, 0, n_lat - SWA_BAND)


def _swa_main_body(sink_ref, q_ref, k_ref, v_ref, bias_ref, o_ref, *, n_ctx, n_lat):
    ks = pl.multiple_of(_swa_band_start(pl.program_id(1), n_lat), SWA_WINDOW)
    kcat = jnp.concatenate([k_ref[n_lat:n_lat + n_ctx, :], k_ref[pl.ds(ks, SWA_BAND), :]], axis=0)
    vcat = jnp.concatenate([v_ref[n_lat:n_lat + n_ctx, :], v_ref[pl.ds(ks, SWA_BAND), :]], axis=0)
    vt = jnp.transpose(vcat.astype(jnp.float32)).astype(jnp.bfloat16)
    bias = jnp.concatenate([bias_ref[...]] * SWA_REP, axis=1)
    lane = lax.broadcasted_iota(jnp.int32, (TILE, LANES), 1)
    lo_half = lane < SWA_HEAD_DIM
    qcol = lax.broadcasted_iota(jnp.int32, (1, SWA_REP * TILE), 1)
    res = []
    for g in range(SWA_KV_HEADS):
        keep = lo_half if g == 0 else jnp.logical_not(lo_half)
        qg = jnp.concatenate([jnp.where(keep, q_ref[:, c * LANES:(c + 1) * LANES], 0.0).astype(jnp.bfloat16)
                              for c in range(SWA_REP)], axis=0)
        sk = jnp.full((1, SWA_REP * TILE), sink_ref[g * SWA_REP + SWA_REP - 1], jnp.float32)
        for c in range(SWA_REP - 2, -1, -1):
            sk = jnp.where(qcol < (c + 1) * TILE, sink_ref[g * SWA_REP + c], sk)
        st = _dot_nt(kcat, qg) + bias
        m = jnp.maximum(jnp.max(st, axis=0, keepdims=True), sk)
        p = jnp.exp2(st - m)
        l = jnp.sum(p, axis=0, keepdims=True) + jnp.exp2(sk - m)
        res.append(_dot(vt, p.astype(jnp.bfloat16)) / l)
    row = lax.broadcasted_iota(jnp.int32, (LANES, TILE), 0)
    for c in range(SWA_REP):
        pair = jnp.where(row < SWA_HEAD_DIM, res[0][:, c * TILE:(c + 1) * TILE], res[1][:, c * TILE:(c + 1) * TILE])
        o_ref[:, c * LANES:(c + 1) * LANES] = jnp.transpose(pair).astype(jnp.bfloat16)


def _swa_bias(n_ctx, n_lat):
    nt = n_lat // TILE
    r = lax.broadcasted_iota(jnp.int32, (n_ctx + SWA_BAND, TILE), 0)
    q = lax.broadcasted_iota(jnp.int32, (n_ctx + SWA_BAND, TILE), 1)
    out = []
    for j in (0, 1, nt - 1):
        kpos = _swa_band_start(j, n_lat) + r - n_ctx
        valid = (r < n_ctx) | (jnp.abs(j * TILE + q - kpos) <= SWA_WINDOW)
        out.append(jnp.where(valid, 0.0, NEG_BIG).astype(jnp.float32))
    return jnp.stack(out)


def _swa_ctx_body(sink_ref, q_ref, k_ref, v_ref, o_ref):
    _swa_attend(sink_ref, q_ref, k_ref[...], v_ref[...], None, o_ref)


def _swa_attention(sink, qs, ks, vs, bias, n_lat):
    b, t, w = qs.shape
    nt = n_lat // TILE
    assert nt >= 3
    return pl.pallas_call(
        functools.partial(_swa_main_body, n_ctx=t - n_lat, n_lat=n_lat),
        grid=(b, nt),
        in_specs=[
            pl.BlockSpec(memory_space=pltpu.SMEM),
            pl.BlockSpec((None, TILE, w), lambda i, j: (i, j, 0)),
            pl.BlockSpec((None, t, LANES), lambda i, j: (i, 0, 0)),
            pl.BlockSpec((None, t, LANES), lambda i, j: (i, 0, 0)),
            pl.BlockSpec((None,) + bias.shape[1:], lambda i, j: (jnp.where(j == 0, 0, jnp.where(j == nt - 1, 2, 1)), 0, 0)),
        ],
        out_specs=pl.BlockSpec((None, TILE, w), lambda i, j: (i, j, 0)),
        out_shape=jax.ShapeDtypeStruct((b, n_lat, w), jnp.bfloat16),
        compiler_params=_cparams(2),
        name="swa_attention",
    )(sink, qs, ks, vs, bias)


def _swa_attention_ctx(sink, qs, ks, vs, n_lat):
    b, t, w = qs.shape
    n_ctx = t - n_lat
    blk = n_lat // n_ctx
    return pl.pallas_call(
        _swa_ctx_body,
        grid=(b,),
        in_specs=[
            pl.BlockSpec(memory_space=pltpu.SMEM),
            pl.BlockSpec((None, n_ctx, w), lambda i: (i, blk, 0)),
            pl.BlockSpec((None, n_ctx, LANES), lambda i: (i, blk, 0)),
            pl.BlockSpec((None, n_ctx, LANES), lambda i: (i, blk, 0)),
        ],
        out_specs=pl.BlockSpec((None, n_ctx, w), lambda i: (i, 0, 0)),
        out_shape=jax.ShapeDtypeStruct((b, n_ctx, w), jnp.bfloat16),
        compiler_params=_cparams(1),
        name="swa_attention_ctx",
    )(sink, qs, ks, vs)


HALO = 16


def _mix_body(a_ref, b_ref, cz_ref, hp_ref, hn_ref, x_ref, mod_ref, cw_ref, wo_ref, n2_ref, rw_ref,
              x1_ref, aff_ref):
    j = pl.program_id(1)
    d = D_MODEL
    first = j == 0
    last = j == pl.num_programs(1) - 1
    z = cz_ref[:, CONV_CH:2 * CONV_CH].astype(jnp.float32)
    zp = hp_ref[HALO - 1:HALO, CONV_CH:2 * CONV_CH].astype(jnp.float32)
    zn = hn_ref[0:1, CONV_CH:2 * CONV_CH].astype(jnp.float32)
    zp = jnp.where(first, jnp.zeros_like(zp), zp)
    zn = jnp.where(last, jnp.zeros_like(zn), zn)
    row = lax.broadcasted_iota(jnp.int32, z.shape, 0)
    z_dn = jnp.where(row == 0, zp, pltpu.roll(z, 1, 0))
    rows = z.shape[0]
    z_up = jnp.where(row == rows - 1, zn, pltpu.roll(z, rows - 1, 0))
    y = z_dn * cw_ref[0:1, :] + z * cw_ref[1:2, :] + z_up * cw_ref[2:3, :]
    cv = (cz_ref[:, 0:CONV_CH].astype(jnp.float32) * y).astype(jnp.bfloat16)
    mix = jnp.concatenate([a_ref[...], b_ref[...], cv], axis=-1)
    g1 = mod_ref[:, 2 * d:3 * d]
    sh2 = mod_ref[:, 3 * d:4 * d]
    sc2 = mod_ref[:, 4 * d:5 * d]
    x1 = x_ref[...] + g1 * _dot(mix, wo_ref[...])
    x1_ref[...] = x1
    h2 = (_rms(x1, n2_ref[...]) * (1.0 + sc2) + sh2).astype(jnp.bfloat16)
    lg = _dot_nt(rw_ref[...], h2)
    ex = jnp.exp(lg - jnp.max(lg, axis=0, keepdims=True))
    aff_ref[...] = ex / jnp.sum(ex, axis=0, keepdims=True)


MIX_TILE = 512


def _mixer_out(a, bsw, cz, x, mod_l, mod_row0, lw, frame_row0):
    b, n, d = x.shape
    t = cz.shape[1]
    tm = min(MIX_TILE, n)
    assert n % tm == 0 and frame_row0 % tm == 0
    tile0 = frame_row0 // tm
    hb = tm // HALO
    here = lambda i, j: (i, j, 0)
    const = lambda i, j: (0, 0)
    if mod_row0 is None:
        mod_map = lambda i, j: (i, 0, 0)
    else:
        mod_map = lambda i, j: (mod_row0, 0, 0)
    return pl.pallas_call(
        _mix_body,
        grid=(b, n // tm),
        in_specs=[
            pl.BlockSpec((None, tm, a.shape[2]), here),
            pl.BlockSpec((None, tm, bsw.shape[2]), here),
            pl.BlockSpec((None, tm, cz.shape[2]), lambda i, j: (i, j + tile0, 0)),
            pl.BlockSpec((None, HALO, cz.shape[2]),
                         lambda i, j: (i, jnp.maximum((j + tile0) * hb - 1, 0), 0)),
            pl.BlockSpec((None, HALO, cz.shape[2]),
                         lambda i, j: (i, jnp.minimum((j + tile0 + 1) * hb, t // HALO - 1), 0)),
            pl.BlockSpec((None, tm, d), here),
            pl.BlockSpec((None, 1, N_MOD * d), mod_map),
            pl.BlockSpec((3, CONV_CH), const),
            pl.BlockSpec((d, d), const),
            pl.BlockSpec((1, d), const),
            pl.BlockSpec((N_EXPERTS, d), const),
        ],
        out_specs=[pl.BlockSpec((None, tm, d), here),
                   pl.BlockSpec((None, N_EXPERTS, tm), lambda i, j: (i, 0, j))],
        out_shape=[jax.ShapeDtypeStruct((b, n, d), jnp.float32),
                   jax.ShapeDtypeStruct((b, N_EXPERTS, n), jnp.float32)],
        compiler_params=_cparams(2),
        name="mixer_out",
    )(a, bsw, cz, cz, cz, x, mod_l, lw["cw"], lw["wo"], lw["n2"], lw["rw"])


ROUTE_ROWS = 128
ROUTE_SPLIT = 64
ROUTE_UNROLL = 2


def _route_body(aff_ref, tri_ref, idx_ref, taff_ref, pos_ref, parts_ref, *, cap):
    rows, n = aff_ref.shape
    aff = aff_ref[...]
    bits = pltpu.bitcast(aff, jnp.int32)

    def count(mask):
        return jnp.sum(jnp.where(mask, 1.0, 0.0), axis=1, keepdims=True)

    def ones(mask):
        return jnp.where(mask, 1.0, 0.0).astype(jnp.bfloat16)

    def search(i, thr):
        cand = thr | (1 << (30 - i))
        return jnp.where(count(bits >= cand) >= cap, cand, thr)

    thr = lax.fori_loop(0, 31, search, jnp.zeros((rows, 1), jnp.int32))
    gt = bits > thr
    eq = bits == thr
    need = cap - count(gt)
    peq = _dot(ones(eq), tri_ref[...])
    sel = gt | (eq & (peq < need))
    pos = _dot(ones(sel), tri_ref[...])
    pos_ref[...] = jnp.where(sel, pos, -1.0)

    a1 = aff.astype(jnp.bfloat16).astype(jnp.float32)
    a2 = (aff - a1).astype(jnp.bfloat16).astype(jnp.float32)
    parts_ref[0] = a1
    parts_ref[1] = a2
    parts_ref[2] = aff - a1 - a2
    tok = lax.broadcasted_iota(jnp.int32, (1, n), 1)
    tok_hi = (tok // ROUTE_SPLIT).astype(jnp.float32)
    tok_lo = (tok % ROUTE_SPLIT).astype(jnp.float32)
    sub = lax.broadcasted_iota(jnp.int32, (SUBLANES, n), 0)
    slot = lax.broadcasted_iota(jnp.int32, (cap, 1), 0).astype(jnp.float32)

    def per_row(r, carry):
        onehot = jnp.where(pos_ref[pl.ds(r, 1), :] == slot, 1.0, 0.0).astype(jnp.bfloat16)
        vals = jnp.where(sub == 0, tok_hi, jnp.where(sub == 1, tok_lo, 0.0))
        for k in range(3):
            vals = jnp.where(sub == 2 + k, parts_ref[k, pl.ds(r, 1), :], vals)
        got = _dot_nt(vals.astype(jnp.bfloat16), onehot)
        idx_ref[pl.ds(r, 1), :] = (got[0:1, :] * ROUTE_SPLIT + got[1:2, :]).astype(jnp.int32)
        taff_ref[pl.ds(r, 1), :] = got[2:3, :] + got[3:4, :] + got[4:5, :]
        return carry

    lax.fori_loop(0, rows, per_row, 0, unroll=ROUTE_UNROLL)


def _routing(aff, tri, cap):
    b, e, n = aff.shape
    rows = b * e
    rr = min(ROUTE_ROWS, rows)
    idx, taff = pl.pallas_call(
        functools.partial(_route_body, cap=cap),
        grid=(rows // rr,),
        in_specs=[pl.BlockSpec((rr, n), lambda i: (i, 0)),
                  pl.BlockSpec((n, n), lambda i: (0, 0))],
        out_specs=[pl.BlockSpec((rr, cap), lambda i: (i, 0)),
                   pl.BlockSpec((rr, cap), lambda i: (i, 0))],
        out_shape=[jax.ShapeDtypeStruct((rows, cap), jnp.int32),
                   jax.ShapeDtypeStruct((rows, cap), jnp.float32)],
        scratch_shapes=[pltpu.VMEM((rr, n), jnp.float32),
                        pltpu.VMEM((3, rr, n), jnp.float32)],
        compiler_params=_cparams(1),
        name="routing",
    )(aff.reshape(rows, n), tri)
    return idx.reshape(b, e, cap), taff.reshape(b, e, cap)


EXP_PER_STEP = 2
MOE_CHUNK = 1024
SCATTER_GROUP = 16


def _moe_body(idx_ref, taff_ref, x_ref, mod_ref, n2_ref, fg_ref, wg_ref, wu_ref, wd_ref, o_ref,
              h_ref, acc_ref, *bufs, nch, cap, final):
    j = pl.program_id(1)
    d = D_MODEL
    nsl = d // LANES
    chunk = x_ref.shape[0]
    rows = chunk * nsl
    nes = N_EXPERTS // EXP_PER_STEP
    xg_refs = bufs[:EXP_PER_STEP]
    yb_refs = bufs[EXP_PER_STEP:]

    @pl.when(j < nch)
    def _():
        sh2 = mod_ref[:, 3 * d:4 * d]
        sc2 = mod_ref[:, 4 * d:5 * d]
        h2 = _rms(x_ref[...], n2_ref[...]) * (1.0 + sc2) + sh2
        base = pl.multiple_of(j * rows, rows)
        for s in range(nsl):
            h_ref[pl.ds(base + s, chunk, stride=nsl), :] = h2[:, s * LANES:(s + 1) * LANES]
        acc_ref[pl.ds(base, rows), :] = jnp.zeros((rows, LANES), jnp.float32)

    @pl.when((j >= nch) & (j < nch + nes))
    def _():
        tcol = jnp.transpose(taff_ref[...])
        for k in range(EXP_PER_STEP):
            xg_ref, yb_ref = xg_refs[k], yb_refs[k]
            tok = [pl.multiple_of(idx_ref[0, k * cap + jj], nsl) for jj in range(cap)]
            for jj in range(cap):
                xg_ref[jj * nsl:(jj + 1) * nsl, :] = h_ref[pl.ds(tok[jj], nsl), :]
            xs = jnp.concatenate([xg_ref[pl.ds(s, cap, stride=nsl), :] for s in range(nsl)], axis=-1)
            xs = xs.astype(jnp.bfloat16)
            gate = _dot(xs, wg_ref[k])
            up = _dot(xs, wu_ref[k])
            hid = (_silu(gate) * up).astype(jnp.bfloat16)
            y = _dot(hid, wd_ref[k]) * tcol[k * cap:(k + 1) * cap, 0:1]
            for s in range(nsl):
                yb_ref[pl.ds(s, cap, stride=nsl), :] = y[:, s * LANES:(s + 1) * LANES]
            for g0 in range(0, cap, SCATTER_GROUP):
                grp = range(g0, g0 + SCATTER_GROUP)
                vals = [acc_ref[pl.ds(tok[jj], nsl), :] + yb_ref[jj * nsl:(jj + 1) * nsl, :] for jj in grp]
                for jj, v in zip(grp, vals):
                    acc_ref[pl.ds(tok[jj], nsl), :] = v

    @pl.when(j >= nch + nes)
    def _():
        c = j - nch - nes
        base = pl.multiple_of(c * rows, rows)
        m = jnp.concatenate([acc_ref[pl.ds(base + s, chunk, stride=nsl), :] for s in range(nsl)], axis=-1)
        g2 = mod_ref[:, 5 * d:6 * d]
        x2 = x_ref[...] + g2 * m
        if final:
            x2 = _rms(x2, fg_ref[...])
        o_ref[...] = x2


def _experts(x1, idx, taff, mod_l, mod_row0, lw, final_g, final):
    nb, n_tok, d = x1.shape
    cap = idx.shape[2]
    chunk = MOE_CHUNK if n_tok % MOE_CHUNK == 0 else MOE_CHUNK // 2
    assert n_tok % chunk == 0 and cap % SCATTER_GROUP == 0
    nch = n_tok // chunk
    nes = N_EXPERTS // EXP_PER_STEP
    steps = 2 * nch + nes
    nsl = d // LANES
    idx = (idx * nsl).reshape(nb, nes, 1, EXP_PER_STEP * cap)
    taff = jnp.broadcast_to(taff.reshape(nb, nes, 1, EXP_PER_STEP * cap), (nb, nes, SUBLANES, EXP_PER_STEP * cap))

    def chunk_in(i, j):
        c = jnp.where(j < nch, j, jnp.where(j < nch + nes, nch - 1, j - nch - nes))
        return (i, c, 0)

    def chunk_out(i, j):
        return (i, jnp.clip(j - nch - nes, 0, nch - 1), 0)

    def slots(i, j):
        return (i, jnp.clip(j - nch, 0, nes - 1), 0, 0)

    def expert(i, j):
        return (jnp.clip(j - nch, 0, nes - 1), 0, 0)

    const = lambda i, j: (0, 0)
    if mod_row0 is None:
        mod_map = lambda i, j: (i, 0, 0)
    else:
        mod_map = lambda i, j: (mod_row0, 0, 0)
    slot_buf = pltpu.VMEM((cap * nsl, LANES), jnp.float32)
    return pl.pallas_call(
        functools.partial(_moe_body, nch=nch, cap=cap, final=final),
        grid=(nb, steps),
        in_specs=[
            pl.BlockSpec((None, None, 1, EXP_PER_STEP * cap), slots, memory_space=pltpu.SMEM),
            pl.BlockSpec((None, None, SUBLANES, EXP_PER_STEP * cap), slots),
            pl.BlockSpec((None, chunk, d), chunk_in),
            pl.BlockSpec((None, 1, N_MOD * d), mod_map),
            pl.BlockSpec((1, d), const),
            pl.BlockSpec((1, d), const),
            pl.BlockSpec((EXP_PER_STEP, d, EXPERT_FF), expert),
            pl.BlockSpec((EXP_PER_STEP, d, EXPERT_FF), expert),
            pl.BlockSpec((EXP_PER_STEP, EXPERT_FF, d), expert),
        ],
        out_specs=pl.BlockSpec((None, chunk, d), chunk_out),
        out_shape=jax.ShapeDtypeStruct((nb, n_tok, d), jnp.float32),
        scratch_shapes=[pltpu.VMEM((n_tok * nsl, LANES), jnp.float32),
                        pltpu.VMEM((n_tok * nsl, LANES), jnp.float32)] + [slot_buf] * (2 * EXP_PER_STEP),
        compiler_params=_cparams(2),
        name="experts",
    )(idx, taff, x1, mod_l, lw["n2"], final_g, lw["wg"], lw["wu"], lw["wd"])


def _group_samples(x1, idx, taff):
    b, n_tok, d = x1.shape
    cap = idx.shape[2]
    g = max(k for k in range(1, 9) if b % k == 0 and (k * n_tok) % (MOE_CHUNK // 2) == 0)
    off = (jnp.arange(b, dtype=jnp.int32) % g) * n_tok
    idx = idx + off[:, None, None]

    def merge(t):
        t = t.reshape(b // g, g, N_EXPERTS, cap)
        return jnp.swapaxes(t, 1, 2).reshape(b // g, N_EXPERTS, g * cap)

    return x1.reshape(b // g, g * n_tok, d), merge(idx), merge(taff)


def _swap_halves(w):
    half = w.shape[-1] // 2
    return jnp.concatenate([-w[..., half:], w[..., :half]], axis=-1)


def _swa_perm():
    cols = []
    for c in range(SWA_REP):
        for g in range(SWA_KV_HEADS):
            h = g * SWA_REP + c
            cols.extend(range(h * SWA_HEAD_DIM, (h + 1) * SWA_HEAD_DIM))
    return np.asarray(cols, np.int32)


def _layer_weights(w_in, q_g, w_uq, kv_g, w_ukv, conv_w, w_o, n1, n2, router_w, wg, wu, wd):
    bf = jnp.bfloat16
    d = w_in.shape[0]
    offs = np.cumsum([0, MLA_Q_RANK, MLA_KV_RANK, MLA_ROPE, SWA_HEADS * SWA_HEAD_DIM,
                      SWA_KV_HEADS * SWA_HEAD_DIM, SWA_KV_HEADS * SWA_HEAD_DIM, CONV_CH, CONV_CH, CONV_CH])
    part = [w_in[:, offs[i]:offs[i + 1]] for i in range(9)]
    w_cq, w_ckv, w_kr, w_sq, w_sk, w_sv, w_cb, w_cc, w_cu = part
    perm = _swa_perm()

    kr_block = jnp.concatenate([w_kr, _swap_halves(w_kr), jnp.zeros((d, LANES - 2 * MLA_ROPE), w_in.dtype)], axis=1)
    win = jnp.concatenate([
        w_cq, w_ckv, kr_block,
        w_sq[:, perm], w_sk, w_sv, w_cb, w_cc, w_cu], axis=1).astype(bf)

    qd = MLA_NOPE + MLA_ROPE
    uq = w_uq.reshape(MLA_Q_RANK, MLA_HEADS, qd)
    zpad = jnp.zeros((MLA_Q_RANK, MLA_HEADS, HEAD_PAD - qd), w_uq.dtype)
    main = jnp.concatenate([uq, zpad], axis=-1)
    swp = jnp.concatenate([jnp.zeros_like(uq[..., :MLA_NOPE]), _swap_halves(uq[..., MLA_NOPE:]), zpad], axis=-1)
    wuq = jnp.concatenate([main.reshape(MLA_Q_RANK, -1), swp.reshape(MLA_Q_RANK, -1)], axis=1).astype(bf)

    ukv = w_ukv.reshape(MLA_KV_RANK, MLA_HEADS, MLA_NOPE + MLA_V)
    wk = jnp.concatenate([ukv[..., :MLA_NOPE],
                          jnp.zeros((MLA_KV_RANK, MLA_HEADS, HEAD_PAD - MLA_NOPE), w_ukv.dtype)], axis=-1)
    wk = wk.reshape(MLA_KV_RANK, -1).astype(bf)
    wv = ukv[..., MLA_NOPE:].reshape(MLA_KV_RANK, -1).astype(bf)

    e = np.zeros((LANES, MLA_HEADS * HEAD_PAD), np.float32)
    for h in range(MLA_HEADS):
        for l in range(MLA_ROPE):
            e[l, h * HEAD_PAD + MLA_NOPE + l] = 1.0
            e[MLA_ROPE + l, h * HEAD_PAD + MLA_NOPE + l] = 1.0

    n_mla = MLA_HEADS * MLA_V
    n_swa = SWA_HEADS * SWA_HEAD_DIM
    wo = jnp.concatenate([w_o[:n_mla], w_o[n_mla:n_mla + n_swa][perm], w_o[n_mla + n_swa:]], axis=0).astype(bf)
    return dict(
        g1=n1.reshape(1, -1), win=win, qg=q_g.reshape(1, -1), wuq=wuq, kvg=kv_g.reshape(1, -1),
        wk=wk, wvt=wv.T, e=jnp.asarray(e, bf), cw=conv_w, wo=wo, n2=n2.reshape(1, -1),
        rw=router_w.T.astype(bf), wg=wg.astype(bf), wu=wu.astype(bf), wd=wd.astype(bf))


def _axial_tables(n_tokens, rot_dim):
    rows = n_tokens // GRID_W
    row = jnp.repeat(jnp.arange(rows, dtype=jnp.float32), GRID_W)
    col = jnp.tile(jnp.arange(GRID_W, dtype=jnp.float32), rows)
    n_freq = rot_dim // 4
    inv = ROPE_BASE ** (-jnp.arange(n_freq, dtype=jnp.float32) / n_freq)
    ang = jnp.concatenate([row[:, None] * inv, col[:, None] * inv], axis=-1)
    return jnp.cos(ang), jnp.sin(ang)


def _row_table(n_ctx, n_lat):
    def with_ctx(cos, sin):
        one = jnp.ones((n_ctx, cos.shape[1]), jnp.float32)
        return (jnp.concatenate([cos, one], axis=0), jnp.concatenate([sin, 0.0 * one], axis=0))

    cm, sm = with_ctx(*_axial_tables(n_lat, MLA_ROPE))
    cs, ss = with_ctx(*_axial_tables(n_lat, SWA_HEAD_DIM))
    t = n_ctx + n_lat
    cm2 = jnp.concatenate([cm, cm], axis=1)
    sm2 = jnp.concatenate([sm, sm], axis=1)
    cs2 = jnp.concatenate([cs, cs], axis=1)
    ss2 = jnp.concatenate([-ss, ss], axis=1)
    pad = jnp.zeros((t, HEAD_PAD - MLA_NOPE - MLA_ROPE), jnp.float32)
    cq = jnp.concatenate([jnp.ones((t, MLA_NOPE), jnp.float32), cm2, pad], axis=1) * (MLA_SCALE * LOG2E)
    sq = jnp.concatenate([jnp.zeros((t, MLA_NOPE), jnp.float32), sm2, pad], axis=1) * (MLA_SCALE * LOG2E)
    kr = jnp.concatenate([cm2, sm2, jnp.zeros((t, LANES - 2 * MLA_ROPE), jnp.float32)], axis=1)
    tab = jnp.concatenate([
        jnp.tile(cq, (1, MLA_HEADS)), jnp.tile(sq, (1, MLA_HEADS)), kr,
        jnp.tile(cs2, (1, SWA_HEADS)) * (SWA_SCALE * LOG2E), jnp.tile(ss2, (1, SWA_HEADS)) * (SWA_SCALE * LOG2E),
        jnp.tile(cs2, (1, SWA_KV_HEADS)), jnp.tile(ss2, (1, SWA_KV_HEADS))], axis=1)
    return tab


def _strict_upper(n):
    r = lax.broadcasted_iota(jnp.int32, (n, n), 0)
    c = lax.broadcasted_iota(jnp.int32, (n, n), 1)
    return (r < c).astype(jnp.bfloat16)


def kernel(x, c, ctx, c_ctx, ada_w, ada_b, norm1_g, w_in, mla_q_norm_g, mla_w_uq, mla_kv_norm_g, mla_w_ukv,
           swa_sink, conv_w, w_o, norm2_g, router_w, exp_w_gate, exp_w_up, exp_w_down, final_norm_g):
    b, s, d = x.shape
    n_ctx = ctx.shape[1]
    depth = ada_w.shape[0]
    assert d == D_MODEL and s % MLA_TQ == 0 and n_ctx == TILE and b + 1 <= MOD_ROWS
    cap_lat = EC_CAPACITY_FACTOR * s // N_EXPERTS
    cap_ctx = EC_CAPACITY_FACTOR * n_ctx // N_EXPERTS

    cc = jnp.concatenate([c, c_ctx[None, :], jnp.zeros((MOD_ROWS - b - 1, d), c.dtype)], axis=0)
    mod = _modulation(cc, ada_w, ada_b).reshape(depth, MOD_ROWS, 1, N_MOD * d)
    tab = _row_table(n_ctx, s)
    tri_lat = _strict_upper(s)
    tri_ctx = _strict_upper(n_ctx)
    swa_bias = _swa_bias(n_ctx, s)
    sink_slots = jnp.zeros((8,), jnp.float32)
    fg = final_norm_g.reshape(1, d)

    xl, xc = x, ctx
    for li in range(depth):
        last = li == depth - 1
        lw = _layer_weights(w_in[li], mla_q_norm_g[li], mla_w_uq[li], mla_kv_norm_g[li], mla_w_ukv[li],
                            conv_w[li], w_o[li], norm1_g[li], norm2_g[li], router_w[li],
                            exp_w_gate[li], exp_w_up[li], exp_w_down[li])
        mod_l = mod[li]
        sink = sink_slots.at[:SWA_HEADS].set(swa_sink[li] * LOG2E)
        qm, km, vmt, qs, ks, vs, cz = _projections(xl, xc, mod_l, lw, tab)
        a = _mla_attention(qm, km, vmt, s)
        bsw = _swa_attention(sink, qs, ks, vs, swa_bias, s)
        x1, aff = _mixer_out(a, bsw, cz, xl, mod_l, None, lw, 0)
        idx, taff = _routing(aff, tri_lat, cap_lat)
        if not last:
            ac = _mla_attention_ctx(qm, km, vmt, s)
            bc = _swa_attention_ctx(sink, qs, ks, vs, s)
            xc1, affc = _mixer_out(ac, bc, cz, xc, mod_l, b, lw, s)
            idxc, taffc = _routing(affc, tri_ctx, cap_ctx)
            xc = _experts(*_group_samples(xc1, idxc, taffc), mod_l, b, lw, fg, final=False).reshape(xc1.shape)
        xl = _experts(x1, idx, taff, mod_l, None, lw, fg, final=last)
    return xl
```

```python
import functools

import jax
import jax.numpy as jnp
import numpy as np
from jax import lax
from jax.experimental import pallas as pl
from jax.experimental.pallas import tpu as pltpu

D_MODEL = 1024
GRID_W = 64
NORM_EPS = 1e-6
ROPE_BASE = 10000.0

MLA_HEADS = 6
MLA_Q_RANK = 256
MLA_KV_RANK = 128
MLA_NOPE = 64
MLA_ROPE = 32
MLA_V = 64
MLA_SCALE = (MLA_NOPE + MLA_ROPE) ** -0.5

SWA_HEADS = 6
SWA_KV_HEADS = 2
SWA_REP = SWA_HEADS // SWA_KV_HEADS
SWA_HEAD_DIM = 64
SWA_WINDOW = 128
SWA_SCALE = SWA_HEAD_DIM ** -0.5

CONV_CH = 256
N_EXPERTS = 16
EXPERT_FF = 512
EC_CAPACITY_FACTOR = 2
N_MOD = 6

LANES = 128
SUBLANES = 8
TILE = 256
HEAD_PAD = 128
MOD_ROWS = 40
NEG_BIG = -1e30
LOG2E = 1.4426950408889634

_C_CQ = 0
_C_CKV = 256
_C_KR = 384
_C_SQ = 512
_C_SK = 896
_C_SV = 1024
_C_CB = 1152
_C_CC = 1408
_C_CU = 1664
IN_W = 1920

_T_CQ = 0
_T_SQ = 768
_T_KR = 1536
_T_CS = 1664
_T_SS = 2048
_T_CK = 2432
_T_SK = 2560
TAB_W = 2688

_VMEM_LIMIT = 56 * 1024 * 1024


def _cparams(n_grid):
    return pltpu.CompilerParams(dimension_semantics=("arbitrary",) * n_grid,
                                vmem_limit_bytes=_VMEM_LIMIT)


def _silu(v):
    return v * (1.0 / (1.0 + jnp.exp(-v)))


def _rms(v, g):
    return v * lax.rsqrt(jnp.mean(v * v, axis=-1, keepdims=True) + NORM_EPS) * g


def _dot(a, b):
    return jnp.dot(a, b, preferred_element_type=jnp.float32)


def _dot_nt(a, b):
    return lax.dot_general(a, b, (((1,), (1,)), ((), ())), preferred_element_type=jnp.float32)


def _mod_body(c_ref, w_ref, b_ref, o_ref):
    a = _silu(c_ref[...]).astype(jnp.bfloat16)
    o_ref[...] = _dot(a, w_ref[...].astype(jnp.bfloat16)) + b_ref[...]


def _modulation(cc, ada_w, ada_b):
    depth, d, n = ada_w.shape
    bn = 512
    return pl.pallas_call(
        _mod_body,
        grid=(depth, n // bn),
        in_specs=[
            pl.BlockSpec((MOD_ROWS, d), lambda l, i: (0, 0)),
            pl.BlockSpec((None, d, bn), lambda l, i: (l, 0, i)),
            pl.BlockSpec((None, 1, bn), lambda l, i: (l, 0, i)),
        ],
        out_specs=pl.BlockSpec((None, MOD_ROWS, bn), lambda l, i: (l, 0, i)),
        out_shape=jax.ShapeDtypeStruct((depth, MOD_ROWS, n), jnp.float32),
        compiler_params=_cparams(2),
        name="modulation",
    )(cc, ada_w, ada_b.reshape(depth, 1, n))


def _proj_body(x_ref, ctx_ref, mod_ref, g1_ref, win_ref, qg_ref, wuq_ref, kvg_ref, wk_ref, wvt_ref,
               e_ref, tab_ref, qm_ref, km_ref, vmt_ref, qs_ref, ks_ref, vs_ref, cz_ref, xs_ref):
    j = pl.program_id(0)
    is_ctx = j == pl.num_programs(0) - 1

    @pl.when(is_ctx)
    def _():
        xs_ref[...] = ctx_ref[...]

    @pl.when(jnp.logical_not(is_ctx))
    def _():
        xs_ref[...] = x_ref[...]

    d = D_MODEL
    sh1 = mod_ref[:, 0:d]
    sc1 = mod_ref[:, d:2 * d]
    h = (_rms(xs_ref[...], g1_ref[...]) * (1.0 + sc1) + sh1).astype(jnp.bfloat16)

    def proj(lo, hi):
        return _dot(h, win_ref[:, lo:hi])

    def tab(lo, n):
        return tab_ref[:, lo:lo + n]

    cq = _rms(proj(_C_CQ, _C_CKV), qg_ref[...]).astype(jnp.bfloat16)
    uq = _dot(cq, wuq_ref[...])
    nq = MLA_HEADS * HEAD_PAD
    qm_ref[...] = (uq[:, :nq] * tab(_T_CQ, nq) + uq[:, nq:] * tab(_T_SQ, nq)).astype(jnp.bfloat16)

    ckv = _rms(proj(_C_CKV, _C_KR), kvg_ref[...]).astype(jnp.bfloat16)
    krp = (proj(_C_KR, _C_SQ) * tab(_T_KR, LANES)).astype(jnp.bfloat16)
    km_ref[...] = (_dot(ckv, wk_ref[...]) + _dot(krp, e_ref[...])).astype(jnp.bfloat16)
    vmt_ref[...] = _dot_nt(wvt_ref[...], ckv).astype(jnp.bfloat16)

    nsq = SWA_HEADS * SWA_HEAD_DIM
    lane = lax.broadcasted_iota(jnp.int32, (h.shape[0], LANES), 1)
    first_half = (lane & (SWA_HEAD_DIM // 2)) == 0

    def rotary(x, c_off, s_off):
        half = SWA_HEAD_DIM // 2
        swapped = jnp.where(first_half, pltpu.roll(x, LANES - half, 1), pltpu.roll(x, half, 1))
        return (x * tab(c_off, LANES) + swapped * tab(s_off, LANES)).astype(jnp.bfloat16)

    uq = proj(_C_SQ, _C_SK)
    for c in range(nsq // LANES):
        qs_ref[:, c * LANES:(c + 1) * LANES] = rotary(uq[:, c * LANES:(c + 1) * LANES],
                                                       _T_CS + c * LANES, _T_SS + c * LANES)
    ks_ref[...] = rotary(proj(_C_SK, _C_SV), _T_CK, _T_SK)
    vs_ref[...] = proj(_C_SV, _C_CB).astype(jnp.bfloat16)

    cz_ref[:, 0:CONV_CH] = proj(_C_CB, _C_CC).astype(jnp.bfloat16)
    cz_ref[:, CONV_CH:2 * CONV_CH] = (proj(_C_CC, _C_CU) * proj(_C_CU, IN_W)).astype(jnp.bfloat16)


def _projections(x, ctx, mod_l, lw, tab):
    b, s, d = x.shape
    n_ctx = ctx.shape[1]
    t = n_ctx + s
    nl = s // TILE
    const = lambda j, i: (0, 0)
    row_outs = [MLA_HEADS * HEAD_PAD, MLA_HEADS * HEAD_PAD, None, SWA_HEADS * SWA_HEAD_DIM, LANES, LANES,
                2 * CONV_CH]
    nv = MLA_HEADS * MLA_V
    out_specs = [pl.BlockSpec((None, nv, TILE), lambda j, i: (i, 0, j)) if w is None
                 else pl.BlockSpec((None, TILE, w), lambda j, i: (i, j, 0)) for w in row_outs]
    out_shape = [jax.ShapeDtypeStruct((b, nv, t) if w is None else (b, t, w), jnp.bfloat16) for w in row_outs]
    return pl.pallas_call(
        _proj_body,
        grid=(nl + 1, b),
        in_specs=[
            pl.BlockSpec((None, TILE, d), lambda j, i: (jnp.where(j == nl, 0, i), jnp.minimum(j, nl - 1), 0)),
            pl.BlockSpec((None, TILE, d), lambda j, i: (jnp.where(j == nl, i, 0), 0, 0)),
            pl.BlockSpec((None, 1, N_MOD * d), lambda j, i: (jnp.where(j == nl, b, i), 0, 0)),
            pl.BlockSpec((1, d), const),
            pl.BlockSpec((d, IN_W), const),
            pl.BlockSpec((1, MLA_Q_RANK), const),
            pl.BlockSpec((MLA_Q_RANK, 2 * MLA_HEADS * HEAD_PAD), const),
            pl.BlockSpec((1, MLA_KV_RANK), const),
            pl.BlockSpec((MLA_KV_RANK, MLA_HEADS * HEAD_PAD), const),
            pl.BlockSpec((nv, MLA_KV_RANK), const),
            pl.BlockSpec((LANES, MLA_HEADS * HEAD_PAD), const),
            pl.BlockSpec((TILE, TAB_W), lambda j, i: (j, 0)),
        ],
        out_specs=out_specs,
        out_shape=out_shape,
        scratch_shapes=[pltpu.VMEM((TILE, d), jnp.float32)],
        compiler_params=_cparams(2),
        name="projections",
    )(x, ctx, mod_l, lw["g1"], lw["win"], lw["qg"], lw["wuq"], lw["kvg"], lw["wk"], lw["wvt"], lw["e"], tab)


MLA_TQ = 2048


def _mla_pair(q_ref, k_ref, vt_ref, o_ref, c):
    tq = q_ref.shape[0]
    res = []
    for hh in range(2):
        lo = (2 * c + hh) * HEAD_PAD
        st = _dot_nt(k_ref[:, lo:lo + HEAD_PAD], q_ref[:, lo:lo + HEAD_PAD])
        m = jnp.max(st, axis=0, keepdims=True)
        p = jnp.exp2(st - m)
        l = jnp.sum(p, axis=0, keepdims=True)
        ot = _dot(vt_ref[c * LANES:(c + 1) * LANES, :], p.astype(jnp.bfloat16))
        res.append(ot / l)
    row = lax.broadcasted_iota(jnp.int32, (LANES, tq), 0)
    o_ref[:, c * LANES:(c + 1) * LANES] = jnp.transpose(
        jnp.where(row < MLA_V, res[0], res[1])).astype(jnp.bfloat16)


def _mla_main_body(q_ref, k_ref, vt_ref, o_ref):
    for c in range(MLA_HEADS // 2):
        _mla_pair(q_ref, k_ref, vt_ref, o_ref, c)


def _mla_ctx_body(q_ref, k_ref, vt_ref, o_ref):
    for c in range(MLA_HEADS // 2):
        _mla_pair(q_ref, k_ref, vt_ref, o_ref, c)


def _mla_attention(qm, km, vmt, n_lat):
    b, t, _ = qm.shape
    return pl.pallas_call(
        _mla_main_body,
        grid=(b, n_lat // MLA_TQ),
        in_specs=[
            pl.BlockSpec((None, MLA_TQ, qm.shape[2]), lambda i, j: (i, j, 0)),
            pl.BlockSpec((None, t, km.shape[2]), lambda i, j: (i, 0, 0)),
            pl.BlockSpec((None, vmt.shape[1], t), lambda i, j: (i, 0, 0)),
        ],
        out_specs=pl.BlockSpec((None, MLA_TQ, vmt.shape[1]), lambda i, j: (i, j, 0)),
        out_shape=jax.ShapeDtypeStruct((b, n_lat, vmt.shape[1]), jnp.bfloat16),
        compiler_params=_cparams(2),
        name="mla_attention",
    )(qm, km, vmt)


def _mla_attention_ctx(qm, km, vmt, n_lat):
    b, t, _ = qm.shape
    n_ctx = t - n_lat
    blk = n_lat // n_ctx
    return pl.pallas_call(
        _mla_ctx_body,
        grid=(b,),
        in_specs=[
            pl.BlockSpec((None, n_ctx, qm.shape[2]), lambda i: (i, blk, 0)),
            pl.BlockSpec((None, n_ctx, km.shape[2]), lambda i: (i, blk, 0)),
            pl.BlockSpec((None, vmt.shape[1], n_ctx), lambda i: (i, 0, blk)),
        ],
        out_specs=pl.BlockSpec((None, n_ctx, vmt.shape[1]), lambda i: (i, 0, 0)),
        out_shape=jax.ShapeDtypeStruct((b, n_ctx, vmt.shape[1]), jnp.bfloat16),
        compiler_params=_cparams(1),
        name="mla_attention_ctx",
    )(qm, km, vmt)


def _swa_attend(sink_ref, q_ref, kcat, vcat, valid, o_ref):
    rows = q_ref.shape[0]
    lane = lax.broadcasted_iota(jnp.int32, (rows, LANES), 1)
    lo_half = lane < SWA_HEAD_DIM
    for c in range(SWA_REP):
        q2 = q_ref[:, c * LANES:(c + 1) * LANES]
        res = []
        for g in range(SWA_KV_HEADS):
            keep = lo_half if g == 0 else jnp.logical_not(lo_half)
            qg = jnp.where(keep, q2, jnp.zeros_like(q2))
            s = _dot_nt(qg, kcat)
            if valid is not None:
                s = jnp.where(valid, s, NEG_BIG)
            sk = sink_ref[g * SWA_REP + c]
            m = jnp.maximum(jnp.max(s, axis=-1, keepdims=True), sk)
            p = jnp.exp2(s - m)
            l = jnp.sum(p, axis=-1, keepdims=True) + jnp.exp2(sk - m)
            res.append(_dot(p.astype(jnp.bfloat16), vcat) / l)
        o_ref[:, c * LANES:(c + 1) * LANES] = jnp.where(lo_half, res[0], res[1]).astype(jnp.bfloat16)


SWA_BAND = TILE + 2 * SWA_WINDOW


def _swa_band_start(j, n_lat):
    return jnp.clip(j * TILE - SWA_WINDOW, 0, n_lat - SWA_BAND)


def _swa_main_body(sink_ref, q_ref, k_ref, v_ref, bias_ref, o_ref, *, n_ctx, n_lat):
    ks = pl.multiple_of(_swa_band_start(pl.program_id(1), n_lat), SWA_WINDOW)
    kcat = jnp.concatenate([k_ref[n_lat:n_lat + n_ctx, :], k_ref[pl.ds(ks, SWA_BAND), :]], axis=0)
    vcat = jnp.concatenate([v_ref[n_lat:n_lat + n_ctx, :], v_ref[pl.ds(ks, SWA_BAND), :]], axis=0)
    vt = jnp.transpose(vcat.astype(jnp.float32)).astype(jnp.bfloat16)
    bias = jnp.concatenate([bias_ref[...]] * SWA_REP, axis=1)
    lane = lax.broadcasted_iota(jnp.int32, (TILE, LANES), 1)
    lo_half = lane < SWA_HEAD_DIM
    qcol = lax.broadcasted_iota(jnp.int32, (1, SWA_REP * TILE), 1)
    res = []
    for g in range(SWA_KV_HEADS):
        keep = lo_half if g == 0 else jnp.logical_not(lo_half)
        qg = jnp.concatenate([jnp.where(keep, q_ref[:, c * LANES:(c + 1) * LANES], 0.0).astype(jnp.bfloat16)
                              for c in range(SWA_REP)], axis=0)
        sk = jnp.full((1, SWA_REP * TILE), sink_ref[g * SWA_REP + SWA_REP - 1], jnp.float32)
        for c in range(SWA_REP - 2, -1, -1):
            sk = jnp.where(qcol < (c + 1) * TILE, sink_ref[g * SWA_REP + c], sk)
        st = _dot_nt(kcat, qg) + bias
        m = jnp.maximum(jnp.max(st, axis=0, keepdims=True), sk)
        p = jnp.exp2(st - m)
        l = jnp.sum(p, axis=0, keepdims=True) + jnp.exp2(sk - m)
        res.append(_dot(vt, p.astype(jnp.bfloat16)) / l)
    row = lax.broadcasted_iota(jnp.int32, (LANES, TILE), 0)
    for c in range(SWA_REP):
        pair = jnp.where(row < SWA_HEAD_DIM, res[0][:, c * TILE:(c + 1) * TILE], res[1][:, c * TILE:(c + 1) * TILE])
        o_ref[:, c * LANES:(c + 1) * LANES] = jnp.transpose(pair).astype(jnp.bfloat16)


def _swa_bias(n_ctx, n_lat):
    nt = n_lat // TILE
    r = lax.broadcasted_iota(jnp.int32, (n_ctx + SWA_BAND, TILE), 0)
    q = lax.broadcasted_iota(jnp.int32, (n_ctx + SWA_BAND, TILE), 1)
    out = []
    for j in (0, 1, nt - 1):
        kpos = _swa_band_start(j, n_lat) + r - n_ctx
        valid = (r < n_ctx) | (jnp.abs(j * TILE + q - kpos) <= SWA_WINDOW)
        out.append(jnp.where(valid, 0.0, NEG_BIG).astype(jnp.float32))
    return jnp.stack(out)


def _swa_ctx_body(sink_ref, q_ref, k_ref, v_ref, o_ref):
    _swa_attend(sink_ref, q_ref, k_ref[...], v_ref[...], None, o_ref)


def _swa_attention(sink, qs, ks, vs, bias, n_lat):
    b, t, w = qs.shape
    nt = n_lat // TILE
    assert nt >= 3
    return pl.pallas_call(
        functools.partial(_swa_main_body, n_ctx=t - n_lat, n_lat=n_lat),
        grid=(b, nt),
        in_specs=[
            pl.BlockSpec(memory_space=pltpu.SMEM),
            pl.BlockSpec((None, TILE, w), lambda i, j: (i, j, 0)),
            pl.BlockSpec((None, t, LANES), lambda i, j: (i, 0, 0)),
            pl.BlockSpec((None, t, LANES), lambda i, j: (i, 0, 0)),
            pl.BlockSpec((None,) + bias.shape[1:], lambda i, j: (jnp.where(j == 0, 0, jnp.where(j == nt - 1, 2, 1)), 0, 0)),
        ],
        out_specs=pl.BlockSpec((None, TILE, w), lambda i, j: (i, j, 0)),
        out_shape=jax.ShapeDtypeStruct((b, n_lat, w), jnp.bfloat16),
        compiler_params=_cparams(2),
        name="swa_attention",
    )(sink, qs, ks, vs, bias)


def _swa_attention_ctx(sink, qs, ks, vs, n_lat):
    b, t, w = qs.shape
    n_ctx = t - n_lat
    blk = n_lat // n_ctx
    return pl.pallas_call(
        _swa_ctx_body,
        grid=(b,),
        in_specs=[
            pl.BlockSpec(memory_space=pltpu.SMEM),
            pl.BlockSpec((None, n_ctx, w), lambda i: (i, blk, 0)),
            pl.BlockSpec((None, n_ctx, LANES), lambda i: (i, blk, 0)),
            pl.BlockSpec((None, n_ctx, LANES), lambda i: (i, blk, 0)),
        ],
        out_specs=pl.BlockSpec((None, n_ctx, w), lambda i: (i, 0, 0)),
        out_shape=jax.ShapeDtypeStruct((b, n_ctx, w), jnp.bfloat16),
        compiler_params=_cparams(1),
        name="swa_attention_ctx",
    )(sink, qs, ks, vs)


HALO = 16


def _mix_body(a_ref, b_ref, cz_ref, hp_ref, hn_ref, x_ref, mod_ref, cw_ref, wo_ref, n2_ref, rw_ref,
              x1_ref, aff_ref):
    j = pl.program_id(1)
    d = D_MODEL
    first = j == 0
    last = j == pl.num_programs(1) - 1
    z = cz_ref[:, CONV_CH:2 * CONV_CH].astype(jnp.float32)
    zp = hp_ref[HALO - 1:HALO, CONV_CH:2 * CONV_CH].astype(jnp.float32)
    zn = hn_ref[0:1, CONV_CH:2 * CONV_CH].astype(jnp.float32)
    zp = jnp.where(first, jnp.zeros_like(zp), zp)
    zn = jnp.where(last, jnp.zeros_like(zn), zn)
    row = lax.broadcasted_iota(jnp.int32, z.shape, 0)
    z_dn = jnp.where(row == 0, zp, pltpu.roll(z, 1, 0))
    rows = z.shape[0]
    z_up = jnp.where(row == rows - 1, zn, pltpu.roll(z, rows - 1, 0))
    y = z_dn * cw_ref[0:1, :] + z * cw_ref[1:2, :] + z_up * cw_ref[2:3, :]
    cv = (cz_ref[:, 0:CONV_CH].astype(jnp.float32) * y).astype(jnp.bfloat16)
    mix = jnp.concatenate([a_ref[...], b_ref[...], cv], axis=-1)
    g1 = mod_ref[:, 2 * d:3 * d]
    sh2 = mod_ref[:, 3 * d:4 * d]
    sc2 = mod_ref[:, 4 * d:5 * d]
    x1 = x_ref[...] + g1 * _dot(mix, wo_ref[...])
    x1_ref[...] = x1
    h2 = (_rms(x1, n2_ref[...]) * (1.0 + sc2) + sh2).astype(jnp.bfloat16)
    lg = _dot_nt(rw_ref[...], h2)
    ex = jnp.exp(lg - jnp.max(lg, axis=0, keepdims=True))
    aff_ref[...] = ex / jnp.sum(ex, axis=0, keepdims=True)


MIX_TILE = 512


def _mixer_out(a, bsw, cz, x, mod_l, mod_row0, lw, frame_row0):
    b, n, d = x.shape
    t = cz.shape[1]
    tm = min(MIX_TILE, n)
    assert n % tm == 0 and frame_row0 % tm == 0
    tile0 = frame_row0 // tm
    hb = tm // HALO
    here = lambda i, j: (i, j, 0)
    const = lambda i, j: (0, 0)
    if mod_row0 is None:
        mod_map = lambda i, j: (i, 0, 0)
    else:
        mod_map = lambda i, j: (mod_row0, 0, 0)
    return pl.pallas_call(
        _mix_body,
        grid=(b, n // tm),
        in_specs=[
            pl.BlockSpec((None, tm, a.shape[2]), here),
            pl.BlockSpec((None, tm, bsw.shape[2]), here),
            pl.BlockSpec((None, tm, cz.shape[2]), lambda i, j: (i, j + tile0, 0)),
            pl.BlockSpec((None, HALO, cz.shape[2]),
                         lambda i, j: (i, jnp.maximum((j + tile0) * hb - 1, 0), 0)),
            pl.BlockSpec((None, HALO, cz.shape[2]),
                         lambda i, j: (i, jnp.minimum((j + tile0 + 1) * hb, t // HALO - 1), 0)),
            pl.BlockSpec((None, tm, d), here),
            pl.BlockSpec((None, 1, N_MOD * d), mod_map),
            pl.BlockSpec((3, CONV_CH), const),
            pl.BlockSpec((d, d), const),
            pl.BlockSpec((1, d), const),
            pl.BlockSpec((N_EXPERTS, d), const),
        ],
        out_specs=[pl.BlockSpec((None, tm, d), here),
                   pl.BlockSpec((None, N_EXPERTS, tm), lambda i, j: (i, 0, j))],
        out_shape=[jax.ShapeDtypeStruct((b, n, d), jnp.float32),
                   jax.ShapeDtypeStruct((b, N_EXPERTS, n), jnp.float32)],
        compiler_params=_cparams(2),
        name="mixer_out",
    )(a, bsw, cz, cz, cz, x, mod_l, lw["cw"], lw["wo"], lw["n2"], lw["rw"])


ROUTE_ROWS = 128
ROUTE_SPLIT = 64
ROUTE_UNROLL = 4


def _route_body(aff_ref, tri_ref, idx_ref, taff_ref, pos_ref, parts_ref, *, cap):
    rows, n = aff_ref.shape
    aff = aff_ref[...]
    bits = pltpu.bitcast(aff, jnp.int32)

    def count(mask):
        return jnp.sum(jnp.where(mask, 1.0, 0.0), axis=1, keepdims=True)

    def ones(mask):
        return jnp.where(mask, 1.0, 0.0).astype(jnp.bfloat16)

    def search(i, thr):
        cand = thr | (1 << (30 - i))
        return jnp.where(count(bits >= cand) >= cap, cand, thr)

    thr = lax.fori_loop(0, 31, search, jnp.zeros((rows, 1), jnp.int32))
    gt = bits > thr
    eq = bits == thr
    need = cap - count(gt)
    peq = _dot(ones(eq), tri_ref[...])
    sel = gt | (eq & (peq < need))
    pos = _dot(ones(sel), tri_ref[...])
    pos_ref[...] = jnp.where(sel, pos, -1.0)

    a1 = aff.astype(jnp.bfloat16).astype(jnp.float32)
    a2 = (aff - a1).astype(jnp.bfloat16).astype(jnp.float32)
    parts_ref[0] = a1
    parts_ref[1] = a2
    parts_ref[2] = aff - a1 - a2
    tok = lax.broadcasted_iota(jnp.int32, (1, n), 1)
    tok_hi = (tok // ROUTE_SPLIT).astype(jnp.float32)
    tok_lo = (tok % ROUTE_SPLIT).astype(jnp.float32)
    sub = lax.broadcasted_iota(jnp.int32, (SUBLANES, n), 0)
    slot = lax.broadcasted_iota(jnp.int32, (cap, 1), 0).astype(jnp.float32)

    def per_row(r, carry):
        onehot = jnp.where(pos_ref[pl.ds(r, 1), :] == slot, 1.0, 0.0).astype(jnp.bfloat16)
        vals = jnp.where(sub == 0, tok_hi, jnp.where(sub == 1, tok_lo, 0.0))
        for k in range(3):
            vals = jnp.where(sub == 2 + k, parts_ref[k, pl.ds(r, 1), :], vals)
        got = _dot_nt(vals.astype(jnp.bfloat16), onehot)
        idx_ref[pl.ds(r, 1), :] = (got[0:1, :] * ROUTE_SPLIT + got[1:2, :]).astype(jnp.int32)
        taff_ref[pl.ds(r, 1), :] = got[2:3, :] + got[3:4, :] + got[4:5, :]
        return carry

    lax.fori_loop(0, rows, per_row, 0, unroll=ROUTE_UNROLL)


def _routing(aff, tri, cap):
    b, e, n = aff.shape
    rows = b * e
    rr = min(ROUTE_ROWS, rows)
    idx, taff = pl.pallas_call(
        functools.partial(_route_body, cap=cap),
        grid=(rows // rr,),
        in_specs=[pl.BlockSpec((rr, n), lambda i: (i, 0)),
                  pl.BlockSpec((n, n), lambda i: (0, 0))],
        out_specs=[pl.BlockSpec((rr, cap), lambda i: (i, 0)),
                   pl.BlockSpec((rr, cap), lambda i: (i, 0))],
        out_shape=[jax.ShapeDtypeStruct((rows, cap), jnp.int32),
                   jax.ShapeDtypeStruct((rows, cap), jnp.float32)],
        scratch_shapes=[pltpu.VMEM((rr, n), jnp.float32),
                        pltpu.VMEM((3, rr, n), jnp.float32)],
        compiler_params=_cparams(1),
        name="routing",
    )(aff.reshape(rows, n), tri)
    return idx.reshape(b, e, cap), taff.reshape(b, e, cap)


EXP_PER_STEP = 2
MOE_CHUNK = 1024
SCATTER_GROUP = 16


def _moe_body(idx_ref, taff_ref, x_ref, mod_ref, n2_ref, fg_ref, wg_ref, wu_ref, wd_ref, o_ref,
              h_ref, acc_ref, *bufs, nch, cap, final):
    j = pl.program_id(1)
    d = D_MODEL
    nsl = d // LANES
    chunk = x_ref.shape[0]
    rows = chunk * nsl
    nes = N_EXPERTS // EXP_PER_STEP
    xg_refs = bufs[:EXP_PER_STEP]
    yb_refs = bufs[EXP_PER_STEP:]

    @pl.when(j < nch)
    def _():
        sh2 = mod_ref[:, 3 * d:4 * d]
        sc2 = mod_ref[:, 4 * d:5 * d]
        h2 = _rms(x_ref[...], n2_ref[...]) * (1.0 + sc2) + sh2
        base = pl.multiple_of(j * rows, rows)
        for s in range(nsl):
            h_ref[pl.ds(base + s, chunk, stride=nsl), :] = h2[:, s * LANES:(s + 1) * LANES]
        acc_ref[pl.ds(base, rows), :] = jnp.zeros((rows, LANES), jnp.float32)

    @pl.when((j >= nch) & (j < nch + nes))
    def _():
        tcol = jnp.transpose(taff_ref[...])
        for k in range(EXP_PER_STEP):
            xg_ref, yb_ref = xg_refs[k], yb_refs[k]
            tok = [pl.multiple_of(idx_ref[0, k * cap + jj], nsl) for jj in range(cap)]
            for jj in range(cap):
                xg_ref[jj * nsl:(jj + 1) * nsl, :] = h_ref[pl.ds(tok[jj], nsl), :]
            xs = jnp.concatenate([xg_ref[pl.ds(s, cap, stride=nsl), :] for s in range(nsl)], axis=-1)
            xs = xs.astype(jnp.bfloat16)
            gate = _dot(xs, wg_ref[k])
            up = _dot(xs, wu_ref[k])
            hid = (_silu(gate) * up).astype(jnp.bfloat16)
            y = _dot(hid, wd_ref[k]) * tcol[k * cap:(k + 1) * cap, 0:1]
            for s in range(nsl):
                yb_ref[pl.ds(s, cap, stride=nsl), :] = y[:, s * LANES:(s + 1) * LANES]
            for g0 in range(0, cap, SCATTER_GROUP):
                grp = range(g0, g0 + SCATTER_GROUP)
                vals = [acc_ref[pl.ds(tok[jj], nsl), :] + yb_ref[jj * nsl:(jj + 1) * nsl, :] for jj in grp]
                for jj, v in zip(grp, vals):
                    acc_ref[pl.ds(tok[jj], nsl), :] = v

    @pl.when(j >= nch + nes)
    def _():
        c = j - nch - nes
        base = pl.multiple_of(c * rows, rows)
        m = jnp.concatenate([acc_ref[pl.ds(base + s, chunk, stride=nsl), :] for s in range(nsl)], axis=-1)
        g2 = mod_ref[:, 5 * d:6 * d]
        x2 = x_ref[...] + g2 * m
        if final:
            x2 = _rms(x2, fg_ref[...])
        o_ref[...] = x2


def _experts(x1, idx, taff, mod_l, mod_row0, lw, final_g, final):
    nb, n_tok, d = x1.shape
    cap = idx.shape[2]
    chunk = MOE_CHUNK if n_tok % MOE_CHUNK == 0 else MOE_CHUNK // 2
    assert n_tok % chunk == 0 and cap % SCATTER_GROUP == 0
    nch = n_tok // chunk
    nes = N_EXPERTS // EXP_PER_STEP
    steps = 2 * nch + nes
    nsl = d // LANES
    idx = (idx * nsl).reshape(nb, nes, 1, EXP_PER_STEP * cap)
    taff = jnp.broadcast_to(taff.reshape(nb, nes, 1, EXP_PER_STEP * cap), (nb, nes, SUBLANES, EXP_PER_STEP * cap))

    def chunk_in(i, j):
        c = jnp.where(j < nch, j, jnp.where(j < nch + nes, nch - 1, j - nch - nes))
        return (i, c, 0)

    def chunk_out(i, j):
        return (i, jnp.clip(j - nch - nes, 0, nch - 1), 0)

    def slots(i, j):
        return (i, jnp.clip(j - nch, 0, nes - 1), 0, 0)

    def expert(i, j):
        return (jnp.clip(j - nch, 0, nes - 1), 0, 0)

    const = lambda i, j: (0, 0)
    if mod_row0 is None:
        mod_map = lambda i, j: (i, 0, 0)
    else:
        mod_map = lambda i, j: (mod_row0, 0, 0)
    slot_buf = pltpu.VMEM((cap * nsl, LANES), jnp.float32)
    return pl.pallas_call(
        functools.partial(_moe_body, nch=nch, cap=cap, final=final),
        grid=(nb, steps),
        in_specs=[
            pl.BlockSpec((None, None, 1, EXP_PER_STEP * cap), slots, memory_space=pltpu.SMEM),
            pl.BlockSpec((None, None, SUBLANES, EXP_PER_STEP * cap), slots),
            pl.BlockSpec((None, chunk, d), chunk_in),
            pl.BlockSpec((None, 1, N_MOD * d), mod_map),
            pl.BlockSpec((1, d), const),
            pl.BlockSpec((1, d), const),
            pl.BlockSpec((EXP_PER_STEP, d, EXPERT_FF), expert),
            pl.BlockSpec((EXP_PER_STEP, d, EXPERT_FF), expert),
            pl.BlockSpec((EXP_PER_STEP, EXPERT_FF, d), expert),
        ],
        out_specs=pl.BlockSpec((None, chunk, d), chunk_out),
        out_shape=jax.ShapeDtypeStruct((nb, n_tok, d), jnp.float32),
        scratch_shapes=[pltpu.VMEM((n_tok * nsl, LANES), jnp.float32),
                        pltpu.VMEM((n_tok * nsl, LANES), jnp.float32)] + [slot_buf] * (2 * EXP_PER_STEP),
        compiler_params=_cparams(2),
        name="experts",
    )(idx, taff, x1, mod_l, lw["n2"], final_g, lw["wg"], lw["wu"], lw["wd"])


def _group_samples(x1, idx, taff):
    b, n_tok, d = x1.shape
    cap = idx.shape[2]
    g = max(k for k in range(1, 9) if b % k == 0 and (k * n_tok) % (MOE_CHUNK // 2) == 0)
    off = (jnp.arange(b, dtype=jnp.int32) % g) * n_tok
    idx = idx + off[:, None, None]

    def merge(t):
        t = t.reshape(b // g, g, N_EXPERTS, cap)
        return jnp.swapaxes(t, 1, 2).reshape(b // g, N_EXPERTS, g * cap)

    return x1.reshape(b // g, g * n_tok, d), merge(idx), merge(taff)


def _swap_halves(w):
    half = w.shape[-1] // 2
    return jnp.concatenate([-w[..., half:], w[..., :half]], axis=-1)


def _swa_perm():
    cols = []
    for c in range(SWA_REP):
        for g in range(SWA_KV_HEADS):
            h = g * SWA_REP + c
            cols.extend(range(h * SWA_HEAD_DIM, (h + 1) * SWA_HEAD_DIM))
    return np.asarray(cols, np.int32)


def _layer_weights(w_in, q_g, w_uq, kv_g, w_ukv, conv_w, w_o, n1, n2, router_w, wg, wu, wd):
    bf = jnp.bfloat16
    d = w_in.shape[0]
    offs = np.cumsum([0, MLA_Q_RANK, MLA_KV_RANK, MLA_ROPE, SWA_HEADS * SWA_HEAD_DIM,
                      SWA_KV_HEADS * SWA_HEAD_DIM, SWA_KV_HEADS * SWA_HEAD_DIM, CONV_CH, CONV_CH, CONV_CH])
    part = [w_in[:, offs[i]:offs[i + 1]] for i in range(9)]
    w_cq, w_ckv, w_kr, w_sq, w_sk, w_sv, w_cb, w_cc, w_cu = part
    perm = _swa_perm()

    kr_block = jnp.concatenate([w_kr, _swap_halves(w_kr), jnp.zeros((d, LANES - 2 * MLA_ROPE), w_in.dtype)], axis=1)
    win = jnp.concatenate([
        w_cq, w_ckv, kr_block,
        w_sq[:, perm], w_sk, w_sv, w_cb, w_cc, w_cu], axis=1).astype(bf)

    qd = MLA_NOPE + MLA_ROPE
    uq = w_uq.reshape(MLA_Q_RANK, MLA_HEADS, qd)
    zpad = jnp.zeros((MLA_Q_RANK, MLA_HEADS, HEAD_PAD - qd), w_uq.dtype)
    main = jnp.concatenate([uq, zpad], axis=-1)
    swp = jnp.concatenate([jnp.zeros_like(uq[..., :MLA_NOPE]), _swap_halves(uq[..., MLA_NOPE:]), zpad], axis=-1)
    wuq = jnp.concatenate([main.reshape(MLA_Q_RANK, -1), swp.reshape(MLA_Q_RANK, -1)], axis=1).astype(bf)

    ukv = w_ukv.reshape(MLA_KV_RANK, MLA_HEADS, MLA_NOPE + MLA_V)
    wk = jnp.concatenate([ukv[..., :MLA_NOPE],
                          jnp.zeros((MLA_KV_RANK, MLA_HEADS, HEAD_PAD - MLA_NOPE), w_ukv.dtype)], axis=-1)
    wk = wk.reshape(MLA_KV_RANK, -1).astype(bf)
    wv = ukv[..., MLA_NOPE:].reshape(MLA_KV_RANK, -1).astype(bf)

    e = np.zeros((LANES, MLA_HEADS * HEAD_PAD), np.float32)
    for h in range(MLA_HEADS):
        for l in range(MLA_ROPE):
            e[l, h * HEAD_PAD + MLA_NOPE + l] = 1.0
            e[MLA_ROPE + l, h * HEAD_PAD + MLA_NOPE + l] = 1.0

    n_mla = MLA_HEADS * MLA_V
    n_swa = SWA_HEADS * SWA_HEAD_DIM
    wo = jnp.concatenate([w_o[:n_mla], w_o[n_mla:n_mla + n_swa][perm], w_o[n_mla + n_swa:]], axis=0).astype(bf)
    return dict(
        g1=n1.reshape(1, -1), win=win, qg=q_g.reshape(1, -1), wuq=wuq, kvg=kv_g.reshape(1, -1),
        wk=wk, wvt=wv.T, e=jnp.asarray(e, bf), cw=conv_w, wo=wo, n2=n2.reshape(1, -1),
        rw=router_w.T.astype(bf), wg=wg.astype(bf), wu=wu.astype(bf), wd=wd.astype(bf))


def _axial_tables(n_tokens, rot_dim):
    rows = n_tokens // GRID_W
    row = jnp.repeat(jnp.arange(rows, dtype=jnp.float32), GRID_W)
    col = jnp.tile(jnp.arange(GRID_W, dtype=jnp.float32), rows)
    n_freq = rot_dim // 4
    inv = ROPE_BASE ** (-jnp.arange(n_freq, dtype=jnp.float32) / n_freq)
    ang = jnp.concatenate([row[:, None] * inv, col[:, None] * inv], axis=-1)
    return jnp.cos(ang), jnp.sin(ang)


def _row_table(n_ctx, n_lat):
    def with_ctx(cos, sin):
        one = jnp.ones((n_ctx, cos.shape[1]), jnp.float32)
        return (jnp.concatenate([cos, one], axis=0), jnp.concatenate([sin, 0.0 * one], axis=0))

    cm, sm = with_ctx(*_axial_tables(n_lat, MLA_ROPE))
    cs, ss = with_ctx(*_axial_tables(n_lat, SWA_HEAD_DIM))
    t = n_ctx + n_lat
    cm2 = jnp.concatenate([cm, cm], axis=1)
    sm2 = jnp.concatenate([sm, sm], axis=1)
    cs2 = jnp.concatenate([cs, cs], axis=1)
    ss2 = jnp.concatenate([-ss, ss], axis=1)
    pad = jnp.zeros((t, HEAD_PAD - MLA_NOPE - MLA_ROPE), jnp.float32)
    cq = jnp.concatenate([jnp.ones((t, MLA_NOPE), jnp.float32), cm2, pad], axis=1) * (MLA_SCALE * LOG2E)
    sq = jnp.concatenate([jnp.zeros((t, MLA_NOPE), jnp.float32), sm2, pad], axis=1) * (MLA_SCALE * LOG2E)
    kr = jnp.concatenate([cm2, sm2, jnp.zeros((t, LANES - 2 * MLA_ROPE), jnp.float32)], axis=1)
    tab = jnp.concatenate([
        jnp.tile(cq, (1, MLA_HEADS)), jnp.tile(sq, (1, MLA_HEADS)), kr,
        jnp.tile(cs2, (1, SWA_HEADS)) * (SWA_SCALE * LOG2E), jnp.tile(ss2, (1, SWA_HEADS)) * (SWA_SCALE * LOG2E),
        jnp.tile(cs2, (1, SWA_KV_HEADS)), jnp.tile(ss2, (1, SWA_KV_HEADS))], axis=1)
    return tab


def _strict_upper(n):
    r = lax.broadcasted_iota(jnp.int32, (n, n), 0)
    c = lax.broadcasted_iota(jnp.int32, (n, n), 1)
    return (r < c).astype(jnp.bfloat16)


def kernel(x, c, ctx, c_ctx, ada_w, ada_b, norm1_g, w_in, mla_q_norm_g, mla_w_uq, mla_kv_norm_g, mla_w_ukv,
           swa_sink, conv_w, w_o, norm2_g, router_w, exp_w_gate, exp_w_up, exp_w_down, final_norm_g):
    b, s, d = x.shape
    n_ctx = ctx.shape[1]
    depth = ada_w.shape[0]
    assert d == D_MODEL and s % MLA_TQ == 0 and n_ctx == TILE and b + 1 <= MOD_ROWS
    cap_lat = EC_CAPACITY_FACTOR * s // N_EXPERTS
    cap_ctx = EC_CAPACITY_FACTOR * n_ctx // N_EXPERTS

    cc = jnp.concatenate([c, c_ctx[None, :], jnp.zeros((MOD_ROWS - b - 1, d), c.dtype)], axis=0)
    mod = _modulation(cc, ada_w, ada_b).reshape(depth, MOD_ROWS, 1, N_MOD * d)
    tab = _row_table(n_ctx, s)
    tri_lat = _strict_upper(s)
    tri_ctx = _strict_upper(n_ctx)
    swa_bias = _swa_bias(n_ctx, s)
    sink_slots = jnp.zeros((8,), jnp.float32)
    fg = final_norm_g.reshape(1, d)

    xl, xc = x, ctx
    for li in range(depth):
        last = li == depth - 1
        lw = _layer_weights(w_in[li], mla_q_norm_g[li], mla_w_uq[li], mla_kv_norm_g[li], mla_w_ukv[li],
                            conv_w[li], w_o[li], norm1_g[li], norm2_g[li], router_w[li],
                            exp_w_gate[li], exp_w_up[li], exp_w_down[li])
        mod_l = mod[li]
        sink = sink_slots.at[:SWA_HEADS].set(swa_sink[li] * LOG2E)
        qm, km, vmt, qs, ks, vs, cz = _projections(xl, xc, mod_l, lw, tab)
        a = _mla_attention(qm, km, vmt, s)
        bsw = _swa_attention(sink, qs, ks, vs, swa_bias, s)
        x1, aff = _mixer_out(a, bsw, cz, xl, mod_l, None, lw, 0)
        idx, taff = _routing(aff, tri_lat, cap_lat)
        if not last:
            ac = _mla_attention_ctx(qm, km, vmt, s)
            bc = _swa_attention_ctx(sink, qs, ks, vs, s)
            xc1, affc = _mixer_out(ac, bc, cz, xc, mod_l, b, lw, s)
            idxc, taffc = _routing(affc, tri_ctx, cap_ctx)
            xc = _experts(*_group_samples(xc1, idxc, taffc), mod_l, b, lw, fg, final=False).reshape(xc1.shape)
        xl = _experts(x1, idx, taff, mod_l, None, lw, fg, final=last)
    return xl
```

```python
import functools

import jax
import jax.numpy as jnp
import numpy as np
from jax import lax
from jax.experimental import pallas as pl
from jax.experimental.pallas import tpu as pltpu

D_MODEL = 1024
GRID_W = 64
NORM_EPS = 1e-6
ROPE_BASE = 10000.0

MLA_HEADS = 6
MLA_Q_RANK = 256
MLA_KV_RANK = 128
MLA_NOPE = 64
MLA_ROPE = 32
MLA_V = 64
MLA_SCALE = (MLA_NOPE + MLA_ROPE) ** -0.5

SWA_HEADS = 6
SWA_KV_HEADS = 2
SWA_REP = SWA_HEADS // SWA_KV_HEADS
SWA_HEAD_DIM = 64
SWA_WINDOW = 128
SWA_SCALE = SWA_HEAD_DIM ** -0.5

CONV_CH = 256
N_EXPERTS = 16
EXPERT_FF = 512
EC_CAPACITY_FACTOR = 2
N_MOD = 6

LANES = 128
SUBLANES = 8
TILE = 256
HEAD_PAD = 128
MOD_ROWS = 40
NEG_BIG = -1e30
LOG2E = 1.4426950408889634

_C_CQ = 0
_C_CKV = 256
_C_KR = 384
_C_SQ = 512
_C_SK = 896
_C_SV = 1024
_C_CB = 1152
_C_CC = 1408
_C_CU = 1664
IN_W = 1920

_T_CQ = 0
_T_SQ = 768
_T_KR = 1536
_T_CS = 1664
_T_SS = 2048
_T_CK = 2432
_T_SK = 2560
TAB_W = 2688

_VMEM_LIMIT = 56 * 1024 * 1024


def _cparams(n_grid):
    return pltpu.CompilerParams(dimension_semantics=("arbitrary",) * n_grid,
                                vmem_limit_bytes=_VMEM_LIMIT)


def _silu(v):
    return v * (1.0 / (1.0 + jnp.exp(-v)))


def _rms(v, g):
    return v * lax.rsqrt(jnp.mean(v * v, axis=-1, keepdims=True) + NORM_EPS) * g


def _dot(a, b):
    return jnp.dot(a, b, preferred_element_type=jnp.float32)


def _dot_nt(a, b):
    return lax.dot_general(a, b, (((1,), (1,)), ((), ())), preferred_element_type=jnp.float32)


def _mod_body(c_ref, w_ref, b_ref, o_ref):
    a = _silu(c_ref[...]).astype(jnp.bfloat16)
    o_ref[...] = _dot(a, w_ref[...].astype(jnp.bfloat16)) + b_ref[...]


def _modulation(cc, ada_w, ada_b):
    depth, d, n = ada_w.shape
    bn = 512
    return pl.pallas_call(
        _mod_body,
        grid=(depth, n // bn),
        in_specs=[
            pl.BlockSpec((MOD_ROWS, d), lambda l, i: (0, 0)),
            pl.BlockSpec((None, d, bn), lambda l, i: (l, 0, i)),
            pl.BlockSpec((None, 1, bn), lambda l, i: (l, 0, i)),
        ],
        out_specs=pl.BlockSpec((None, MOD_ROWS, bn), lambda l, i: (l, 0, i)),
        out_shape=jax.ShapeDtypeStruct((depth, MOD_ROWS, n), jnp.float32),
        compiler_params=_cparams(2),
        name="modulation",
    )(cc, ada_w, ada_b.reshape(depth, 1, n))


def _proj_body(x_ref, ctx_ref, mod_ref, g1_ref, win_ref, qg_ref, wuq_ref, kvg_ref, wk_ref, wvt_ref,
               e_ref, tab_ref, qm_ref, km_ref, vmt_ref, qs_ref, ks_ref, vs_ref, cz_ref, xs_ref):
    j = pl.program_id(0)
    is_ctx = j == pl.num_programs(0) - 1

    @pl.when(is_ctx)
    def _():
        xs_ref[...] = ctx_ref[...]

    @pl.when(jnp.logical_not(is_ctx))
    def _():
        xs_ref[...] = x_ref[...]

    d = D_MODEL
    sh1 = mod_ref[:, 0:d]
    sc1 = mod_ref[:, d:2 * d]
    h = (_rms(xs_ref[...], g1_ref[...]) * (1.0 + sc1) + sh1).astype(jnp.bfloat16)

    def proj(lo, hi):
        return _dot(h, win_ref[:, lo:hi])

    def tab(lo, n):
        return tab_ref[:, lo:lo + n]

    cq = _rms(proj(_C_CQ, _C_CKV), qg_ref[...]).astype(jnp.bfloat16)
    uq = _dot(cq, wuq_ref[...])
    nq = MLA_HEADS * HEAD_PAD
    qm_ref[...] = (uq[:, :nq] * tab(_T_CQ, nq) + uq[:, nq:] * tab(_T_SQ, nq)).astype(jnp.bfloat16)

    ckv = _rms(proj(_C_CKV, _C_KR), kvg_ref[...]).astype(jnp.bfloat16)
    krp = (proj(_C_KR, _C_SQ) * tab(_T_KR, LANES)).astype(jnp.bfloat16)
    km_ref[...] = (_dot(ckv, wk_ref[...]) + _dot(krp, e_ref[...])).astype(jnp.bfloat16)
    vmt_ref[...] = _dot_nt(wvt_ref[...], ckv).astype(jnp.bfloat16)

    nsq = SWA_HEADS * SWA_HEAD_DIM
    lane = lax.broadcasted_iota(jnp.int32, (h.shape[0], LANES), 1)
    first_half = (lane & (SWA_HEAD_DIM // 2)) == 0

    def rotary(x, c_off, s_off):
        half = SWA_HEAD_DIM // 2
        swapped = jnp.where(first_half, pltpu.roll(x, LANES - half, 1), pltpu.roll(x, half, 1))
        return (x * tab(c_off, LANES) + swapped * tab(s_off, LANES)).astype(jnp.bfloat16)

    uq = proj(_C_SQ, _C_SK)
    for c in range(nsq // LANES):
        qs_ref[:, c * LANES:(c + 1) * LANES] = rotary(uq[:, c * LANES:(c + 1) * LANES],
                                                       _T_CS + c * LANES, _T_SS + c * LANES)
    ks_ref[...] = rotary(proj(_C_SK, _C_SV), _T_CK, _T_SK)
    vs_ref[...] = proj(_C_SV, _C_CB).astype(jnp.bfloat16)

    cz_ref[:, 0:CONV_CH] = proj(_C_CB, _C_CC).astype(jnp.bfloat16)
    cz_ref[:, CONV_CH:2 * CONV_CH] = (proj(_C_CC, _C_CU) * proj(_C_CU, IN_W)).astype(jnp.bfloat16)


def _projections(x, ctx, mod_l, lw, tab):
    b, s, d = x.shape
    n_ctx = ctx.shape[1]
    t = n_ctx + s
    nl = s // TILE
    const = lambda j, i: (0, 0)
    row_outs = [MLA_HEADS * HEAD_PAD, MLA_HEADS * HEAD_PAD, None, SWA_HEADS * SWA_HEAD_DIM, LANES, LANES,
                2 * CONV_CH]
    nv = MLA_HEADS * MLA_V
    out_specs = [pl.BlockSpec((None, nv, TILE), lambda j, i: (i, 0, j)) if w is None
                 else pl.BlockSpec((None, TILE, w), lambda j, i: (i, j, 0)) for w in row_outs]
    out_shape = [jax.ShapeDtypeStruct((b, nv, t) if w is None else (b, t, w), jnp.bfloat16) for w in row_outs]
    return pl.pallas_call(
        _proj_body,
        grid=(nl + 1, b),
        in_specs=[
            pl.BlockSpec((None, TILE, d), lambda j, i: (jnp.where(j == nl, 0, i), jnp.minimum(j, nl - 1), 0)),
            pl.BlockSpec((None, TILE, d), lambda j, i: (jnp.where(j == nl, i, 0), 0, 0)),
            pl.BlockSpec((None, 1, N_MOD * d), lambda j, i: (jnp.where(j == nl, b, i), 0, 0)),
            pl.BlockSpec((1, d), const),
            pl.BlockSpec((d, IN_W), const),
            pl.BlockSpec((1, MLA_Q_RANK), const),
            pl.BlockSpec((MLA_Q_RANK, 2 * MLA_HEADS * HEAD_PAD), const),
            pl.BlockSpec((1, MLA_KV_RANK), const),
            pl.BlockSpec((MLA_KV_RANK, MLA_HEADS * HEAD_PAD), const),
            pl.BlockSpec((nv, MLA_KV_RANK), const),
            pl.BlockSpec((LANES, MLA_HEADS * HEAD_PAD), const),
            pl.BlockSpec((TILE, TAB_W), lambda j, i: (j, 0)),
        ],
        out_specs=out_specs,
        out_shape=out_shape,
        scratch_shapes=[pltpu.VMEM((TILE, d), jnp.float32)],
        compiler_params=_cparams(2),
        name="projections",
    )(x, ctx, mod_l, lw["g1"], lw["win"], lw["qg"], lw["wuq"], lw["kvg"], lw["wk"], lw["wvt"], lw["e"], tab)


MLA_TQ = 2048


def _mla_pair(q_ref, k_ref, vt_ref, o_ref, c):
    tq = q_ref.shape[0]
    res = []
    for hh in range(2):
        lo = (2 * c + hh) * HEAD_PAD
        st = _dot_nt(k_ref[:, lo:lo + HEAD_PAD], q_ref[:, lo:lo + HEAD_PAD])
        m = jnp.max(st, axis=0, keepdims=True)
        p = jnp.exp2(st - m)
        l = jnp.sum(p, axis=0, keepdims=True)
        ot = _dot(vt_ref[c * LANES:(c + 1) * LANES, :], p.astype(jnp.bfloat16))
        res.append(ot / l)
    row = lax.broadcasted_iota(jnp.int32, (LANES, tq), 0)
    o_ref[:, c * LANES:(c + 1) * LANES] = jnp.transpose(
        jnp.where(row < MLA_V, res[0], res[1])).astype(jnp.bfloat16)


def _mla_main_body(q_ref, k_ref, vt_ref, o_ref):
    for c in range(MLA_HEADS // 2):
        _mla_pair(q_ref, k_ref, vt_ref, o_ref, c)


def _mla_ctx_body(q_ref, k_ref, vt_ref, o_ref):
    for c in range(MLA_HEADS // 2):
        _mla_pair(q_ref, k_ref, vt_ref, o_ref, c)


def _mla_attention(qm, km, vmt, n_lat):
    b, t, _ = qm.shape
    return pl.pallas_call(
        _mla_main_body,
        grid=(b, n_lat // MLA_TQ),
        in_specs=[
            pl.BlockSpec((None, MLA_TQ, qm.shape[2]), lambda i, j: (i, j, 0)),
            pl.BlockSpec((None, t, km.shape[2]), lambda i, j: (i, 0, 0)),
            pl.BlockSpec((None, vmt.shape[1], t), lambda i, j: (i, 0, 0)),
        ],
        out_specs=pl.BlockSpec((None, MLA_TQ, vmt.shape[1]), lambda i, j: (i, j, 0)),
        out_shape=jax.ShapeDtypeStruct((b, n_lat, vmt.shape[1]), jnp.bfloat16),
        compiler_params=_cparams(2),
        name="mla_attention",
    )(qm, km, vmt)


def _mla_attention_ctx(qm, km, vmt, n_lat):
    b, t, _ = qm.shape
    n_ctx = t - n_lat
    blk = n_lat // n_ctx
    return pl.pallas_call(
        _mla_ctx_body,
        grid=(b,),
        in_specs=[
            pl.BlockSpec((None, n_ctx, qm.shape[2]), lambda i: (i, blk, 0)),
            pl.BlockSpec((None, n_ctx, km.shape[2]), lambda i: (i, blk, 0)),
            pl.BlockSpec((None, vmt.shape[1], n_ctx), lambda i: (i, 0, blk)),
        ],
        out_specs=pl.BlockSpec((None, n_ctx, vmt.shape[1]), lambda i: (i, 0, 0)),
        out_shape=jax.ShapeDtypeStruct((b, n_ctx, vmt.shape[1]), jnp.bfloat16),
        compiler_params=_cparams(1),
        name="mla_attention_ctx",
    )(qm, km, vmt)


def _swa_attend(sink_ref, q_ref, kcat, vcat, valid, o_ref):
    rows = q_ref.shape[0]
    lane = lax.broadcasted_iota(jnp.int32, (rows, LANES), 1)
    lo_half = lane < SWA_HEAD_DIM
    for c in range(SWA_REP):
        q2 = q_ref[:, c * LANES:(c + 1) * LANES]
        res = []
        for g in range(SWA_KV_HEADS):
            keep = lo_half if g == 0 else jnp.logical_not(lo_half)
            qg = jnp.where(keep, q2, jnp.zeros_like(q2))
            s = _dot_nt(qg, kcat)
            if valid is not None:
                s = jnp.where(valid, s, NEG_BIG)
            sk = sink_ref[g * SWA_REP + c]
            m = jnp.maximum(jnp.max(s, axis=-1, keepdims=True), sk)
            p = jnp.exp2(s - m)
            l = jnp.sum(p, axis=-1, keepdims=True) + jnp.exp2(sk - m)
            res.append(_dot(p.astype(jnp.bfloat16), vcat) / l)
        o_ref[:, c * LANES:(c + 1) * LANES] = jnp.where(lo_half, res[0], res[1]).astype(jnp.bfloat16)


SWA_BAND = TILE + 2 * SWA_WINDOW


def _swa_band_start(j, n_lat):
    return jnp.clip(j * TILE - SWA_WINDOW, 0, n_lat - SWA_BAND)


def _swa_main_body(sink_ref, q_ref, k_ref, v_ref, bias_ref, o_ref, *, n_ctx, n_lat):
    ks = pl.multiple_of(_swa_band_start(pl.program_id(1), n_lat), SWA_WINDOW)
    kcat = jnp.concatenate([k_ref[n_lat:n_lat + n_ctx, :], k_ref[pl.ds(ks, SWA_BAND), :]], axis=0)
    vcat = jnp.concatenate([v_ref[n_lat:n_lat + n_ctx, :], v_ref[pl.ds(ks, SWA_BAND), :]], axis=0)
    vt = jnp.transpose(vcat.astype(jnp.float32)).astype(jnp.bfloat16)
    bias = jnp.concatenate([bias_ref[...]] * SWA_REP, axis=1)
    lane = lax.broadcasted_iota(jnp.int32, (TILE, LANES), 1)
    lo_half = lane < SWA_HEAD_DIM
    qcol = lax.broadcasted_iota(jnp.int32, (1, SWA_REP * TILE), 1)
    res = []
    for g in range(SWA_KV_HEADS):
        keep = lo_half if g == 0 else jnp.logical_not(lo_half)
        qg = jnp.concatenate([jnp.where(keep, q_ref[:, c * LANES:(c + 1) * LANES], 0.0).astype(jnp.bfloat16)
                              for c in range(SWA_REP)], axis=0)
        sk = jnp.full((1, SWA_REP * TILE), sink_ref[g * SWA_REP + SWA_REP - 1], jnp.float32)
        for c in range(SWA_REP - 2, -1, -1):
            sk = jnp.where(qcol < (c + 1) * TILE, sink_ref[g * SWA_REP + c], sk)
        st = _dot_nt(kcat, qg) + bias
        m = jnp.maximum(jnp.max(st, axis=0, keepdims=True), sk)
        p = jnp.exp2(st - m)
        l = jnp.sum(p, axis=0, keepdims=True) + jnp.exp2(sk - m)
        res.append(_dot(vt, p.astype(jnp.bfloat16)) / l)
    row = lax.broadcasted_iota(jnp.int32, (LANES, TILE), 0)
    for c in range(SWA_REP):
        pair = jnp.where(row < SWA_HEAD_DIM, res[0][:, c * TILE:(c + 1) * TILE], res[1][:, c * TILE:(c + 1) * TILE])
        o_ref[:, c * LANES:(c + 1) * LANES] = jnp.transpose(pair).astype(jnp.bfloat16)


def _swa_bias(n_ctx, n_lat):
    nt = n_lat // TILE
    r = lax.broadcasted_iota(jnp.int32, (n_ctx + SWA_BAND, TILE), 0)
    q = lax.broadcasted_iota(jnp.int32, (n_ctx + SWA_BAND, TILE), 1)
    out = []
    for j in (0, 1, nt - 1):
        kpos = _swa_band_start(j, n_lat) + r - n_ctx
        valid = (r < n_ctx) | (jnp.abs(j * TILE + q - kpos) <= SWA_WINDOW)
        out.append(jnp.where(valid, 0.0, NEG_BIG).astype(jnp.float32))
    return jnp.stack(out)


def _swa_ctx_body(sink_ref, q_ref, k_ref, v_ref, o_ref):
    _swa_attend(sink_ref, q_ref, k_ref[...], v_ref[...], None, o_ref)


def _swa_attention(sink, qs, ks, vs, bias, n_lat):
    b, t, w = qs.shape
    nt = n_lat // TILE
    assert nt >= 3
    return pl.pallas_call(
        functools.partial(_swa_main_body, n_ctx=t - n_lat, n_lat=n_lat),
        grid=(b, nt),
        in_specs=[
            pl.BlockSpec(memory_space=pltpu.SMEM),
            pl.BlockSpec((None, TILE, w), lambda i, j: (i, j, 0)),
            pl.BlockSpec((None, t, LANES), lambda i, j: (i, 0, 0)),
            pl.BlockSpec((None, t, LANES), lambda i, j: (i, 0, 0)),
            pl.BlockSpec((None,) + bias.shape[1:], lambda i, j: (jnp.where(j == 0, 0, jnp.where(j == nt - 1, 2, 1)), 0, 0)),
        ],
        out_specs=pl.BlockSpec((None, TILE, w), lambda i, j: (i, j, 0)),
        out_shape=jax.ShapeDtypeStruct((b, n_lat, w), jnp.bfloat16),
        compiler_params=_cparams(2),
        name="swa_attention",
    )(sink, qs, ks, vs, bias)


def _swa_attention_ctx(sink, qs, ks, vs, n_lat):
    b, t, w = qs.shape
    n_ctx = t - n_lat
    blk = n_lat // n_ctx
    return pl.pallas_call(
        _swa_ctx_body,
        grid=(b,),
        in_specs=[
            pl.BlockSpec(memory_space=pltpu.SMEM),
            pl.BlockSpec((None, n_ctx, w), lambda i: (i, blk, 0)),
            pl.BlockSpec((None, n_ctx, LANES), lambda i: (i, blk, 0)),
            pl.BlockSpec((None, n_ctx, LANES), lambda i: (i, blk, 0)),
        ],
        out_specs=pl.BlockSpec((None, n_ctx, w), lambda i: (i, 0, 0)),
        out_shape=jax.ShapeDtypeStruct((b, n_ctx, w), jnp.bfloat16),
        compiler_params=_cparams(1),
        name="swa_attention_ctx",
    )(sink, qs, ks, vs)


HALO = 16


def _mix_body(a_ref, b_ref, cz_ref, hp_ref, hn_ref, x_ref, mod_ref, cw_ref, wo_ref, n2_ref, rw_ref,
              x1_ref, aff_ref):
    j = pl.program_id(1)
    d = D_MODEL
    first = j == 0
    last = j == pl.num_programs(1) - 1
    z = cz_ref[:, CONV_CH:2 * CONV_CH].astype(jnp.float32)
    zp = hp_ref[HALO - 1:HALO, CONV_CH:2 * CONV_CH].astype(jnp.float32)
    zn = hn_ref[0:1, CONV_CH:2 * CONV_CH].astype(jnp.float32)
    zp = jnp.where(first, jnp.zeros_like(zp), zp)
    zn = jnp.where(last, jnp.zeros_like(zn), zn)
    row = lax.broadcasted_iota(jnp.int32, z.shape, 0)
    z_dn = jnp.where(row == 0, zp, pltpu.roll(z, 1, 0))
    rows = z.shape[0]
    z_up = jnp.where(row == rows - 1, zn, pltpu.roll(z, rows - 1, 0))
    y = z_dn * cw_ref[0:1, :] + z * cw_ref[1:2, :] + z_up * cw_ref[2:3, :]
    cv = (cz_ref[:, 0:CONV_CH].astype(jnp.float32) * y).astype(jnp.bfloat16)
    mix = jnp.concatenate([a_ref[...], b_ref[...], cv], axis=-1)
    g1 = mod_ref[:, 2 * d:3 * d]
    sh2 = mod_ref[:, 3 * d:4 * d]
    sc2 = mod_ref[:, 4 * d:5 * d]
    x1 = x_ref[...] + g1 * _dot(mix, wo_ref[...])
    x1_ref[...] = x1
    h2 = (_rms(x1, n2_ref[...]) * (1.0 + sc2) + sh2).astype(jnp.bfloat16)
    lg = _dot_nt(rw_ref[...], h2)
    ex = jnp.exp(lg - jnp.max(lg, axis=0, keepdims=True))
    aff_ref[...] = ex / jnp.sum(ex, axis=0, keepdims=True)


MIX_TILE = 1024


def _mixer_out(a, bsw, cz, x, mod_l, mod_row0, lw, frame_row0):
    b, n, d = x.shape
    t = cz.shape[1]
    tm = min(MIX_TILE, n)
    assert n % tm == 0 and frame_row0 % tm == 0
    tile0 = frame_row0 // tm
    hb = tm // HALO
    here = lambda i, j: (i, j, 0)
    const = lambda i, j: (0, 0)
    if mod_row0 is None:
        mod_map = lambda i, j: (i, 0, 0)
    else:
        mod_map = lambda i, j: (mod_row0, 0, 0)
    return pl.pallas_call(
        _mix_body,
        grid=(b, n // tm),
        in_specs=[
            pl.BlockSpec((None, tm, a.shape[2]), here),
            pl.BlockSpec((None, tm, bsw.shape[2]), here),
            pl.BlockSpec((None, tm, cz.shape[2]), lambda i, j: (i, j + tile0, 0)),
            pl.BlockSpec((None, HALO, cz.shape[2]),
                         lambda i, j: (i, jnp.maximum((j + tile0) * hb - 1, 0), 0)),
            pl.BlockSpec((None, HALO, cz.shape[2]),
                         lambda i, j: (i, jnp.minimum((j + tile0 + 1) * hb, t // HALO - 1), 0)),
            pl.BlockSpec((None, tm, d), here),
            pl.BlockSpec((None, 1, N_MOD * d), mod_map),
            pl.BlockSpec((3, CONV_CH), const),
            pl.BlockSpec((d, d), const),
            pl.BlockSpec((1, d), const),
            pl.BlockSpec((N_EXPERTS, d), const),
        ],
        out_specs=[pl.BlockSpec((None, tm, d), here),
                   pl.BlockSpec((None, N_EXPERTS, tm), lambda i, j: (i, 0, j))],
        out_shape=[jax.ShapeDtypeStruct((b, n, d), jnp.float32),
                   jax.ShapeDtypeStruct((b, N_EXPERTS, n), jnp.float32)],
        compiler_params=_cparams(2),
        name="mixer_out",
    )(a, bsw, cz, cz, cz, x, mod_l, lw["cw"], lw["wo"], lw["n2"], lw["rw"])


ROUTE_ROWS = 128
ROUTE_SPLIT = 64
ROUTE_UNROLL = 4


def _route_body(aff_ref, tri_ref, idx_ref, taff_ref, pos_ref, parts_ref, *, cap):
    rows, n = aff_ref.shape
    aff = aff_ref[...]
    bits = pltpu.bitcast(aff, jnp.int32)

    def count(mask):
        return jnp.sum(jnp.where(mask, 1.0, 0.0), axis=1, keepdims=True)

    def ones(mask):
        return jnp.where(mask, 1.0, 0.0).astype(jnp.bfloat16)

    def search(i, thr):
        cand = thr | (1 << (30 - i))
        return jnp.where(count(bits >= cand) >= cap, cand, thr)

    thr = lax.fori_loop(0, 31, search, jnp.zeros((rows, 1), jnp.int32))
    gt = bits > thr
    eq = bits == thr
    need = cap - count(gt)
    peq = _dot(ones(eq), tri_ref[...])
    sel = gt | (eq & (peq < need))
    pos = _dot(ones(sel), tri_ref[...])
    pos_ref[...] = jnp.where(sel, pos, -1.0)

    a1 = aff.astype(jnp.bfloat16).astype(jnp.float32)
    a2 = (aff - a1).astype(jnp.bfloat16).astype(jnp.float32)
    parts_ref[0] = a1
    parts_ref[1] = a2
    parts_ref[2] = aff - a1 - a2
    tok = lax.broadcasted_iota(jnp.int32, (1, n), 1)
    tok_hi = (tok // ROUTE_SPLIT).astype(jnp.float32)
    tok_lo = (tok % ROUTE_SPLIT).astype(jnp.float32)
    sub = lax.broadcasted_iota(jnp.int32, (SUBLANES, n), 0)
    slot = lax.broadcasted_iota(jnp.int32, (cap, 1), 0).astype(jnp.float32)

    def per_row(r, carry):
        onehot = jnp.where(pos_ref[pl.ds(r, 1), :] == slot, 1.0, 0.0).astype(jnp.bfloat16)
        vals = jnp.where(sub == 0, tok_hi, jnp.where(sub == 1, tok_lo, 0.0))
        for k in range(3):
            vals = jnp.where(sub == 2 + k, parts_ref[k, pl.ds(r, 1), :], vals)
        got = _dot_nt(vals.astype(jnp.bfloat16), onehot)
        idx_ref[pl.ds(r, 1), :] = (got[0:1, :] * ROUTE_SPLIT + got[1:2, :]).astype(jnp.int32)
        taff_ref[pl.ds(r, 1), :] = got[2:3, :] + got[3:4, :] + got[4:5, :]
        return carry

    lax.fori_loop(0, rows, per_row, 0, unroll=ROUTE_UNROLL)


def _routing(aff, tri, cap):
    b, e, n = aff.shape
    rows = b * e
    rr = min(ROUTE_ROWS, rows)
    idx, taff = pl.pallas_call(
        functools.partial(_route_body, cap=cap),
        grid=(rows // rr,),
        in_specs=[pl.BlockSpec((rr, n), lambda i: (i, 0)),
                  pl.BlockSpec((n, n), lambda i: (0, 0))],
        out_specs=[pl.BlockSpec((rr, cap), lambda i: (i, 0)),
                   pl.BlockSpec((rr, cap), lambda i: (i, 0))],
        out_shape=[jax.ShapeDtypeStruct((rows, cap), jnp.int32),
                   jax.ShapeDtypeStruct((rows, cap), jnp.float32)],
        scratch_shapes=[pltpu.VMEM((rr, n), jnp.float32),
                        pltpu.VMEM((3, rr, n), jnp.float32)],
        compiler_params=_cparams(1),
        name="routing",
    )(aff.reshape(rows, n), tri)
    return idx.reshape(b, e, cap), taff.reshape(b, e, cap)


EXP_PER_STEP = 2
MOE_CHUNK = 1024
SCATTER_GROUP = 16


def _moe_body(idx_ref, taff_ref, x_ref, mod_ref, n2_ref, fg_ref, wg_ref, wu_ref, wd_ref, o_ref,
              h_ref, acc_ref, *bufs, nch, cap, final):
    j = pl.program_id(1)
    d = D_MODEL
    nsl = d // LANES
    chunk = x_ref.shape[0]
    rows = chunk * nsl
    nes = N_EXPERTS // EXP_PER_STEP
    xg_refs = bufs[:EXP_PER_STEP]
    yb_refs = bufs[EXP_PER_STEP:]

    @pl.when(j < nch)
    def _():
        sh2 = mod_ref[:, 3 * d:4 * d]
        sc2 = mod_ref[:, 4 * d:5 * d]
        h2 = _rms(x_ref[...], n2_ref[...]) * (1.0 + sc2) + sh2
        base = pl.multiple_of(j * rows, rows)
        for s in range(nsl):
            h_ref[pl.ds(base + s, chunk, stride=nsl), :] = h2[:, s * LANES:(s + 1) * LANES]
        acc_ref[pl.ds(base, rows), :] = jnp.zeros((rows, LANES), jnp.float32)

    @pl.when((j >= nch) & (j < nch + nes))
    def _():
        tcol = jnp.transpose(taff_ref[...])
        for k in range(EXP_PER_STEP):
            xg_ref, yb_ref = xg_refs[k], yb_refs[k]
            tok = [pl.multiple_of(idx_ref[0, k * cap + jj], nsl) for jj in range(cap)]
            for jj in range(cap):
                xg_ref[jj * nsl:(jj + 1) * nsl, :] = h_ref[pl.ds(tok[jj], nsl), :]
            xs = jnp.concatenate([xg_ref[pl.ds(s, cap, stride=nsl), :] for s in range(nsl)], axis=-1)
            xs = xs.astype(jnp.bfloat16)
            gate = _dot(xs, wg_ref[k])
            up = _dot(xs, wu_ref[k])
            hid = (_silu(gate) * up).astype(jnp.bfloat16)
            y = _dot(hid, wd_ref[k]) * tcol[k * cap:(k + 1) * cap, 0:1]
            for s in range(nsl):
                yb_ref[pl.ds(s, cap, stride=nsl), :] = y[:, s * LANES:(s + 1) * LANES]
            for g0 in range(0, cap, SCATTER_GROUP):
                grp = range(g0, g0 + SCATTER_GROUP)
                vals = [acc_ref[pl.ds(tok[jj], nsl), :] + yb_ref[jj * nsl:(jj + 1) * nsl, :] for jj in grp]
                for jj, v in zip(grp, vals):
                    acc_ref[pl.ds(tok[jj], nsl), :] = v

    @pl.when(j >= nch + nes)
    def _():
        c = j - nch - nes
        base = pl.multiple_of(c * rows, rows)
        m = jnp.concatenate([acc_ref[pl.ds(base + s, chunk, stride=nsl), :] for s in range(nsl)], axis=-1)
        g2 = mod_ref[:, 5 * d:6 * d]
        x2 = x_ref[...] + g2 * m
        if final:
            x2 = _rms(x2, fg_ref[...])
        o_ref[...] = x2


def _experts(x1, idx, taff, mod_l, mod_row0, lw, final_g, final):
    nb, n_tok, d = x1.shape
    cap = idx.shape[2]
    chunk = MOE_CHUNK if n_tok % MOE_CHUNK == 0 else MOE_CHUNK // 2
    assert n_tok % chunk == 0 and cap % SCATTER_GROUP == 0
    nch = n_tok // chunk
    nes = N_EXPERTS // EXP_PER_STEP
    steps = 2 * nch + nes
    nsl = d // LANES
    idx = (idx * nsl).reshape(nb, nes, 1, EXP_PER_STEP * cap)
    taff = jnp.broadcast_to(taff.reshape(nb, nes, 1, EXP_PER_STEP * cap), (nb, nes, SUBLANES, EXP_PER_STEP * cap))

    def chunk_in(i, j):
        c = jnp.where(j < nch, j, jnp.where(j < nch + nes, nch - 1, j - nch - nes))
        return (i, c, 0)

    def chunk_out(i, j):
        return (i, jnp.clip(j - nch - nes, 0, nch - 1), 0)

    def slots(i, j):
        return (i, jnp.clip(j - nch, 0, nes - 1), 0, 0)

    def expert(i, j):
        return (jnp.clip(j - nch, 0, nes - 1), 0, 0)

    const = lambda i, j: (0, 0)
    if mod_row0 is None:
        mod_map = lambda i, j: (i, 0, 0)
    else:
        mod_map = lambda i, j: (mod_row0, 0, 0)
    slot_buf = pltpu.VMEM((cap * nsl, LANES), jnp.float32)
    return pl.pallas_call(
        functools.partial(_moe_body, nch=nch, cap=cap, final=final),
        grid=(nb, steps),
        in_specs=[
            pl.BlockSpec((None, None, 1, EXP_PER_STEP * cap), slots, memory_space=pltpu.SMEM),
            pl.BlockSpec((None, None, SUBLANES, EXP_PER_STEP * cap), slots),
            pl.BlockSpec((None, chunk, d), chunk_in),
            pl.BlockSpec((None, 1, N_MOD * d), mod_map),
            pl.BlockSpec((1, d), const),
            pl.BlockSpec((1, d), const),
            pl.BlockSpec((EXP_PER_STEP, d, EXPERT_FF), expert),
            pl.BlockSpec((EXP_PER_STEP, d, EXPERT_FF), expert),
            pl.BlockSpec((EXP_PER_STEP, EXPERT_FF, d), expert),
        ],
        out_specs=pl.BlockSpec((None, chunk, d), chunk_out),
        out_shape=jax.ShapeDtypeStruct((nb, n_tok, d), jnp.float32),
        scratch_shapes=[pltpu.VMEM((n_tok * nsl, LANES), jnp.float32),
                        pltpu.VMEM((n_tok * nsl, LANES), jnp.float32)] + [slot_buf] * (2 * EXP_PER_STEP),
        compiler_params=_cparams(2),
        name="experts",
    )(idx, taff, x1, mod_l, lw["n2"], final_g, lw["wg"], lw["wu"], lw["wd"])


def _group_samples(x1, idx, taff):
    b, n_tok, d = x1.shape
    cap = idx.shape[2]
    g = max(k for k in range(1, 9) if b % k == 0 and (k * n_tok) % (MOE_CHUNK // 2) == 0)
    off = (jnp.arange(b, dtype=jnp.int32) % g) * n_tok
    idx = idx + off[:, None, None]

    def merge(t):
        t = t.reshape(b // g, g, N_EXPERTS, cap)
        return jnp.swapaxes(t, 1, 2).reshape(b // g, N_EXPERTS, g * cap)

    return x1.reshape(b // g, g * n_tok, d), merge(idx), merge(taff)


def _swap_halves(w):
    half = w.shape[-1] // 2
    return jnp.concatenate([-w[..., half:], w[..., :half]], axis=-1)


def _swa_perm():
    cols = []
    for c in range(SWA_REP):
        for g in range(SWA_KV_HEADS):
            h = g * SWA_REP + c
            cols.extend(range(h * SWA_HEAD_DIM, (h + 1) * SWA_HEAD_DIM))
    return np.asarray(cols, np.int32)


def _layer_weights(w_in, q_g, w_uq, kv_g, w_ukv, conv_w, w_o, n1, n2, router_w, wg, wu, wd):
    bf = jnp.bfloat16
    d = w_in.shape[0]
    offs = np.cumsum([0, MLA_Q_RANK, MLA_KV_RANK, MLA_ROPE, SWA_HEADS * SWA_HEAD_DIM,
                      SWA_KV_HEADS * SWA_HEAD_DIM, SWA_KV_HEADS * SWA_HEAD_DIM, CONV_CH, CONV_CH, CONV_CH])
    part = [w_in[:, offs[i]:offs[i + 1]] for i in range(9)]
    w_cq, w_ckv, w_kr, w_sq, w_sk, w_sv, w_cb, w_cc, w_cu = part
    perm = _swa_perm()

    kr_block = jnp.concatenate([w_kr, _swap_halves(w_kr), jnp.zeros((d, LANES - 2 * MLA_ROPE), w_in.dtype)], axis=1)
    win = jnp.concatenate([
        w_cq, w_ckv, kr_block,
        w_sq[:, perm], w_sk, w_sv, w_cb, w_cc, w_cu], axis=1).astype(bf)

    qd = MLA_NOPE + MLA_ROPE
    uq = w_uq.reshape(MLA_Q_RANK, MLA_HEADS, qd)
    zpad = jnp.zeros((MLA_Q_RANK, MLA_HEADS, HEAD_PAD - qd), w_uq.dtype)
    main = jnp.concatenate([uq, zpad], axis=-1)
    swp = jnp.concatenate([jnp.zeros_like(uq[..., :MLA_NOPE]), _swap_halves(uq[..., MLA_NOPE:]), zpad], axis=-1)
    wuq = jnp.concatenate([main.reshape(MLA_Q_RANK, -1), swp.reshape(MLA_Q_RANK, -1)], axis=1).astype(bf)

    ukv = w_ukv.reshape(MLA_KV_RANK, MLA_HEADS, MLA_NOPE + MLA_V)
    wk = jnp.concatenate([ukv[..., :MLA_NOPE],
                          jnp.zeros((MLA_KV_RANK, MLA_HEADS, HEAD_PAD - MLA_NOPE), w_ukv.dtype)], axis=-1)
    wk = wk.reshape(MLA_KV_RANK, -1).astype(bf)
    wv = ukv[..., MLA_NOPE:].reshape(MLA_KV_RANK, -1).astype(bf)

    e = np.zeros((LANES, MLA_HEADS * HEAD_PAD), np.float32)
    for h in range(MLA_HEADS):
        for l in range(MLA_ROPE):
            e[l, h * HEAD_PAD + MLA_NOPE + l] = 1.0
            e[MLA_ROPE + l, h * HEAD_PAD + MLA_NOPE + l] = 1.0

    n_mla = MLA_HEADS * MLA_V
    n_swa = SWA_HEADS * SWA_HEAD_DIM
    wo = jnp.concatenate([w_o[:n_mla], w_o[n_mla:n_mla + n_swa][perm], w_o[n_mla + n_swa:]], axis=0).astype(bf)
    return dict(
        g1=n1.reshape(1, -1), win=win, qg=q_g.reshape(1, -1), wuq=wuq, kvg=kv_g.reshape(1, -1),
        wk=wk, wvt=wv.T, e=jnp.asarray(e, bf), cw=conv_w, wo=wo, n2=n2.reshape(1, -1),
        rw=router_w.T.astype(bf), wg=wg.astype(bf), wu=wu.astype(bf), wd=wd.astype(bf))


def _axial_tables(n_tokens, rot_dim):
    rows = n_tokens // GRID_W
    row = jnp.repeat(jnp.arange(rows, dtype=jnp.float32), GRID_W)
    col = jnp.tile(jnp.arange(GRID_W, dtype=jnp.float32), rows)
    n_freq = rot_dim // 4
    inv = ROPE_BASE ** (-jnp.arange(n_freq, dtype=jnp.float32) / n_freq)
    ang = jnp.concatenate([row[:, None] * inv, col[:, None] * inv], axis=-1)
    return jnp.cos(ang), jnp.sin(ang)


def _row_table(n_ctx, n_lat):
    def with_ctx(cos, sin):
        one = jnp.ones((n_ctx, cos.shape[1]), jnp.float32)
        return (jnp.concatenate([cos, one], axis=0), jnp.concatenate([sin, 0.0 * one], axis=0))

    cm, sm = with_ctx(*_axial_tables(n_lat, MLA_ROPE))
    cs, ss = with_ctx(*_axial_tables(n_lat, SWA_HEAD_DIM))
    t = n_ctx + n_lat
    cm2 = jnp.concatenate([cm, cm], axis=1)
    sm2 = jnp.concatenate([sm, sm], axis=1)
    cs2 = jnp.concatenate([cs, cs], axis=1)
    ss2 = jnp.concatenate([-ss, ss], axis=1)
    pad = jnp.zeros((t, HEAD_PAD - MLA_NOPE - MLA_ROPE), jnp.float32)
    cq = jnp.concatenate([jnp.ones((t, MLA_NOPE), jnp.float32), cm2, pad], axis=1) * (MLA_SCALE * LOG2E)
    sq = jnp.concatenate([jnp.zeros((t, MLA_NOPE), jnp.float32), sm2, pad], axis=1) * (MLA_SCALE * LOG2E)
    kr = jnp.concatenate([cm2, sm2, jnp.zeros((t, LANES - 2 * MLA_ROPE), jnp.float32)], axis=1)
    tab = jnp.concatenate([
        jnp.tile(cq, (1, MLA_HEADS)), jnp.tile(sq, (1, MLA_HEADS)), kr,
        jnp.tile(cs2, (1, SWA_HEADS)) * (SWA_SCALE * LOG2E), jnp.tile(ss2, (1, SWA_HEADS)) * (SWA_SCALE * LOG2E),
        jnp.tile(cs2, (1, SWA_KV_HEADS)), jnp.tile(ss2, (1, SWA_KV_HEADS))], axis=1)
    return tab


def _strict_upper(n):
    r = lax.broadcasted_iota(jnp.int32, (n, n), 0)
    c = lax.broadcasted_iota(jnp.int32, (n, n), 1)
    return (r < c).astype(jnp.bfloat16)


def kernel(x, c, ctx, c_ctx, ada_w, ada_b, norm1_g, w_in, mla_q_norm_g, mla_w_uq, mla_kv_norm_g, mla_w_ukv,
           swa_sink, conv_w, w_o, norm2_g, router_w, exp_w_gate, exp_w_up, exp_w_down, final_norm_g):
    b, s, d = x.shape
    n_ctx = ctx.shape[1]
    depth = ada_w.shape[0]
    assert d == D_MODEL and s % MLA_TQ == 0 and n_ctx == TILE and b + 1 <= MOD_ROWS
    cap_lat = EC_CAPACITY_FACTOR * s // N_EXPERTS
    cap_ctx = EC_CAPACITY_FACTOR * n_ctx // N_EXPERTS

    cc = jnp.concatenate([c, c_ctx[None, :], jnp.zeros((MOD_ROWS - b - 1, d), c.dtype)], axis=0)
    mod = _modulation(cc, ada_w, ada_b).reshape(depth, MOD_ROWS, 1, N_MOD * d)
    tab = _row_table(n_ctx, s)
    tri_lat = _strict_upper(s)
    tri_ctx = _strict_upper(n_ctx)
    swa_bias = _swa_bias(n_ctx, s)
    sink_slots = jnp.zeros((8,), jnp.float32)
    fg = final_norm_g.reshape(1, d)

    xl, xc = x, ctx
    for li in range(depth):
        last = li == depth - 1
        lw = _layer_weights(w_in[li], mla_q_norm_g[li], mla_w_uq[li], mla_kv_norm_g[li], mla_w_ukv[li],
                            conv_w[li], w_o[li], norm1_g[li], norm2_g[li], router_w[li],
                            exp_w_gate[li], exp_w_up[li], exp_w_down[li])
        mod_l = mod[li]
        sink = sink_slots.at[:SWA_HEADS].set(swa_sink[li] * LOG2E)
        qm, km, vmt, qs, ks, vs, cz = _projections(xl, xc, mod_l, lw, tab)
        a = _mla_attention(qm, km, vmt, s)
        bsw = _swa_attention(sink, qs, ks, vs, swa_bias, s)
        x1, aff = _mixer_out(a, bsw, cz, xl, mod_l, None, lw, 0)
        idx, taff = _routing(aff, tri_lat, cap_lat)
        if not last:
            ac = _mla_attention_ctx(qm, km, vmt, s)
            bc = _swa_attention_ctx(sink, qs, ks, vs, s)
            xc1, affc = _mixer_out(ac, bc, cz, xc, mod_l, b, lw, s)
            idxc, taffc = _routing(affc, tri_ctx, cap_ctx)
            xc = _experts(*_group_samples(xc1, idxc, taffc), mod_l, b, lw, fg, final=False).reshape(xc1.shape)
        xl = _experts(x1, idx, taff, mod_l, None, lw, fg, final=last)
    return xl
```

```python
import functools

import jax
import jax.numpy as jnp
import numpy as np
from jax import lax
from jax.experimental import pallas as pl
from jax.experimental.pallas import tpu as pltpu

D_MODEL = 1024
GRID_W = 64
NORM_EPS = 1e-6
ROPE_BASE = 10000.0

MLA_HEADS = 6
MLA_Q_RANK = 256
MLA_KV_RANK = 128
MLA_NOPE = 64
MLA_ROPE = 32
MLA_V = 64
MLA_SCALE = (MLA_NOPE + MLA_ROPE) ** -0.5

SWA_HEADS = 6
SWA_KV_HEADS = 2
SWA_REP = SWA_HEADS // SWA_KV_HEADS
SWA_HEAD_DIM = 64
SWA_WINDOW = 128
SWA_SCALE = SWA_HEAD_DIM ** -0.5

CONV_CH = 256
N_EXPERTS = 16
EXPERT_FF = 512
EC_CAPACITY_FACTOR = 2
N_MOD = 6

LANES = 128
SUBLANES = 8
TILE = 256
HEAD_PAD = 128
MOD_ROWS = 40
NEG_BIG = -1e30
LOG2E = 1.4426950408889634

_C_CQ = 0
_C_CKV = 256
_C_KR = 384
_C_SQ = 512
_C_SK = 896
_C_SV = 1024
_C_CB = 1152
_C_CC = 1408
_C_CU = 1664
IN_W = 1920

_T_CQ = 0
_T_SQ = 768
_T_KR = 1536
_T_CS = 1664
_T_SS = 2048
_T_CK = 2432
_T_SK = 2560
TAB_W = 2688

_VMEM_LIMIT = 56 * 1024 * 1024


def _cparams(n_grid):
    return pltpu.CompilerParams(dimension_semantics=("arbitrary",) * n_grid,
                                vmem_limit_bytes=_VMEM_LIMIT)


def _silu(v):
    return v * (1.0 / (1.0 + jnp.exp(-v)))


def _rms(v, g):
    return v * lax.rsqrt(jnp.mean(v * v, axis=-1, keepdims=True) + NORM_EPS) * g


def _dot(a, b):
    return jnp.dot(a, b, preferred_element_type=jnp.float32)


def _dot_nt(a, b):
    return lax.dot_general(a, b, (((1,), (1,)), ((), ())), preferred_element_type=jnp.float32)


def _mod_body(c_ref, w_ref, b_ref, o_ref):
    a = _silu(c_ref[...]).astype(jnp.bfloat16)
    o_ref[...] = _dot(a, w_ref[...].astype(jnp.bfloat16)) + b_ref[...]


def _modulation(cc, ada_w, ada_b):
    depth, d, n = ada_w.shape
    bn = 512
    return pl.pallas_call(
        _mod_body,
        grid=(depth, n // bn),
        in_specs=[
            pl.BlockSpec((MOD_ROWS, d), lambda l, i: (0, 0)),
            pl.BlockSpec((None, d, bn), lambda l, i: (l, 0, i)),
            pl.BlockSpec((None, 1, bn), lambda l, i: (l, 0, i)),
        ],
        out_specs=pl.BlockSpec((None, MOD_ROWS, bn), lambda l, i: (l, 0, i)),
        out_shape=jax.ShapeDtypeStruct((depth, MOD_ROWS, n), jnp.float32),
        compiler_params=_cparams(2),
        name="modulation",
    )(cc, ada_w, ada_b.reshape(depth, 1, n))


def _proj_body(x_ref, ctx_ref, mod_ref, g1_ref, win_ref, qg_ref, wuq_ref, kvg_ref, wk_ref, wvt_ref,
               e_ref, tab_ref, qm_ref, km_ref, vmt_ref, qs_ref, ks_ref, vs_ref, cz_ref, xs_ref):
    j = pl.program_id(0)
    is_ctx = j == pl.num_programs(0) - 1

    @pl.when(is_ctx)
    def _():
        xs_ref[...] = ctx_ref[...]

    @pl.when(jnp.logical_not(is_ctx))
    def _():
        xs_ref[...] = x_ref[...]

    d = D_MODEL
    sh1 = mod_ref[:, 0:d]
    sc1 = mod_ref[:, d:2 * d]
    h = (_rms(xs_ref[...], g1_ref[...]) * (1.0 + sc1) + sh1).astype(jnp.bfloat16)

    def proj(lo, hi):
        return _dot(h, win_ref[:, lo:hi])

    def tab(lo, n):
        return tab_ref[:, lo:lo + n]

    cq = _rms(proj(_C_CQ, _C_CKV), qg_ref[...]).astype(jnp.bfloat16)
    uq = _dot(cq, wuq_ref[...])
    nq = MLA_HEADS * HEAD_PAD
    qm_ref[...] = (uq[:, :nq] * tab(_T_CQ, nq) + uq[:, nq:] * tab(_T_SQ, nq)).astype(jnp.bfloat16)

    ckv = _rms(proj(_C_CKV, _C_KR), kvg_ref[...]).astype(jnp.bfloat16)
    krp = (proj(_C_KR, _C_SQ) * tab(_T_KR, LANES)).astype(jnp.bfloat16)
    km_ref[...] = (_dot(ckv, wk_ref[...]) + _dot(krp, e_ref[...])).astype(jnp.bfloat16)
    vmt_ref[...] = _dot_nt(wvt_ref[...], ckv).astype(jnp.bfloat16)

    nsq = SWA_HEADS * SWA_HEAD_DIM
    lane = lax.broadcasted_iota(jnp.int32, (h.shape[0], LANES), 1)
    first_half = (lane & (SWA_HEAD_DIM // 2)) == 0

    def rotary(x, c_off, s_off):
        half = SWA_HEAD_DIM // 2
        swapped = jnp.where(first_half, pltpu.roll(x, LANES - half, 1), pltpu.roll(x, half, 1))
        return (x * tab(c_off, LANES) + swapped * tab(s_off, LANES)).astype(jnp.bfloat16)

    uq = proj(_C_SQ, _C_SK)
    for c in range(nsq // LANES):
        qs_ref[:, c * LANES:(c + 1) * LANES] = rotary(uq[:, c * LANES:(c + 1) * LANES],
                                                       _T_CS + c * LANES, _T_SS + c * LANES)
    ks_ref[...] = rotary(proj(_C_SK, _C_SV), _T_CK, _T_SK)
    vs_ref[...] = proj(_C_SV, _C_CB).astype(jnp.bfloat16)

    cz_ref[:, 0:CONV_CH] = proj(_C_CB, _C_CC).astype(jnp.bfloat16)
    cz_ref[:, CONV_CH:2 * CONV_CH] = (proj(_C_CC, _C_CU) * proj(_C_CU, IN_W)).astype(jnp.bfloat16)


def _projections(x, ctx, mod_l, lw, tab):
    b, s, d = x.shape
    n_ctx = ctx.shape[1]
    t = n_ctx + s
    nl = s // TILE
    const = lambda j, i: (0, 0)
    row_outs = [MLA_HEADS * HEAD_PAD, MLA_HEADS * HEAD_PAD, None, SWA_HEADS * SWA_HEAD_DIM, LANES, LANES,
                2 * CONV_CH]
    nv = MLA_HEADS * MLA_V
    out_specs = [pl.BlockSpec((None, nv, TILE), lambda j, i: (i, 0, j)) if w is None
                 else pl.BlockSpec((None, TILE, w), lambda j, i: (i, j, 0)) for w in row_outs]
    out_shape = [jax.ShapeDtypeStruct((b, nv, t) if w is None else (b, t, w), jnp.bfloat16) for w in row_outs]
    return pl.pallas_call(
        _proj_body,
        grid=(nl + 1, b),
        in_specs=[
            pl.BlockSpec((None, TILE, d), lambda j, i: (jnp.where(j == nl, 0, i), jnp.minimum(j, nl - 1), 0)),
            pl.BlockSpec((None, TILE, d), lambda j, i: (jnp.where(j == nl, i, 0), 0, 0)),
            pl.BlockSpec((None, 1, N_MOD * d), lambda j, i: (jnp.where(j == nl, b, i), 0, 0)),
            pl.BlockSpec((1, d), const),
            pl.BlockSpec((d, IN_W), const),
            pl.BlockSpec((1, MLA_Q_RANK), const),
            pl.BlockSpec((MLA_Q_RANK, 2 * MLA_HEADS * HEAD_PAD), const),
            pl.BlockSpec((1, MLA_KV_RANK), const),
            pl.BlockSpec((MLA_KV_RANK, MLA_HEADS * HEAD_PAD), const),
            pl.BlockSpec((nv, MLA_KV_RANK), const),
            pl.BlockSpec((LANES, MLA_HEADS * HEAD_PAD), const),
            pl.BlockSpec((TILE, TAB_W), lambda j, i: (j, 0)),
        ],
        out_specs=out_specs,
        out_shape=out_shape,
        scratch_shapes=[pltpu.VMEM((TILE, d), jnp.float32)],
        compiler_params=_cparams(2),
        name="projections",
    )(x, ctx, mod_l, lw["g1"], lw["win"], lw["qg"], lw["wuq"], lw["kvg"], lw["wk"], lw["wvt"], lw["e"], tab)


MLA_TQ = 2048


def _mla_pair(q_ref, k_ref, vt_ref, o_ref, c):
    tq = q_ref.shape[0]
    res = []
    for hh in range(2):
        lo = (2 * c + hh) * HEAD_PAD
        st = _dot_nt(k_ref[:, lo:lo + HEAD_PAD], q_ref[:, lo:lo + HEAD_PAD])
        m = jnp.max(st, axis=0, keepdims=True)
        p = jnp.exp2(st - m)
        l = jnp.sum(p, axis=0, keepdims=True)
        ot = _dot(vt_ref[c * LANES:(c + 1) * LANES, :], p.astype(jnp.bfloat16))
        res.append(ot / l)
    row = lax.broadcasted_iota(jnp.int32, (LANES, tq), 0)
    o_ref[:, c * LANES:(c + 1) * LANES] = jnp.transpose(
        jnp.where(row < MLA_V, res[0], res[1])).astype(jnp.bfloat16)


def _mla_main_body(q_ref, k_ref, vt_ref, o_ref):
    for c in range(MLA_HEADS // 2):
        _mla_pair(q_ref, k_ref, vt_ref, o_ref, c)


def _mla_ctx_body(q_ref, k_ref, vt_ref, o_ref):
    for c in range(MLA_HEADS // 2):
        _mla_pair(q_ref, k_ref, vt_ref, o_ref, c)


def _mla_attention(qm, km, vmt, n_lat):
    b, t, _ = qm.shape
    return pl.pallas_call(
        _mla_main_body,
        grid=(b, n_lat // MLA_TQ),
        in_specs=[
            pl.BlockSpec((None, MLA_TQ, qm.shape[2]), lambda i, j: (i, j, 0)),
            pl.BlockSpec((None, t, km.shape[2]), lambda i, j: (i, 0, 0)),
            pl.BlockSpec((None, vmt.shape[1], t), lambda i, j: (i, 0, 0)),
        ],
        out_specs=pl.BlockSpec((None, MLA_TQ, vmt.shape[1]), lambda i, j: (i, j, 0)),
        out_shape=jax.ShapeDtypeStruct((b, n_lat, vmt.shape[1]), jnp.bfloat16),
        compiler_params=_cparams(2),
        name="mla_attention",
    )(qm, km, vmt)


def _mla_attention_ctx(qm, km, vmt, n_lat):
    b, t, _ = qm.shape
    n_ctx = t - n_lat
    blk = n_lat // n_ctx
    return pl.pallas_call(
        _mla_ctx_body,
        grid=(b,),
        in_specs=[
            pl.BlockSpec((None, n_ctx, qm.shape[2]), lambda i: (i, blk, 0)),
            pl.BlockSpec((None, n_ctx, km.shape[2]), lambda i: (i, blk, 0)),
            pl.BlockSpec((None, vmt.shape[1], n_ctx), lambda i: (i, 0, blk)),
        ],
        out_specs=pl.BlockSpec((None, n_ctx, vmt.shape[1]), lambda i: (i, 0, 0)),
        out_shape=jax.ShapeDtypeStruct((b, n_ctx, vmt.shape[1]), jnp.bfloat16),
        compiler_params=_cparams(1),
        name="mla_attention_ctx",
    )(qm, km, vmt)


def _swa_attend(sink_ref, q_ref, kcat, vcat, valid, o_ref):
    rows = q_ref.shape[0]
    lane = lax.broadcasted_iota(jnp.int32, (rows, LANES), 1)
    lo_half = lane < SWA_HEAD_DIM
    for c in range(SWA_REP):
        q2 = q_ref[:, c * LANES:(c + 1) * LANES]
        res = []
        for g in range(SWA_KV_HEADS):
            keep = lo_half if g == 0 else jnp.logical_not(lo_half)
            qg = jnp.where(keep, q2, jnp.zeros_like(q2))
            s = _dot_nt(qg, kcat)
            if valid is not None:
                s = jnp.where(valid, s, NEG_BIG)
            sk = sink_ref[g * SWA_REP + c]
            m = jnp.maximum(jnp.max(s, axis=-1, keepdims=True), sk)
            p = jnp.exp2(s - m)
            l = jnp.sum(p, axis=-1, keepdims=True) + jnp.exp2(sk - m)
            res.append(_dot(p.astype(jnp.bfloat16), vcat) / l)
        o_ref[:, c * LANES:(c + 1) * LANES] = jnp.where(lo_half, res[0], res[1]).astype(jnp.bfloat16)


SWA_BAND = TILE + 2 * SWA_WINDOW


def _swa_band_start(j, n_lat):
    return jnp.clip(j * TILE - SWA_WINDOW, 0, n_lat - SWA_BAND)


def _swa_main_body(sink_ref, q_ref, k_ref, v_ref, bias_ref, o_ref, *, n_ctx, n_lat):
    ks = pl.multiple_of(_swa_band_start(pl.program_id(1), n_lat), SWA_WINDOW)
    kcat = jnp.concatenate([k_ref[n_lat:n_lat + n_ctx, :], k_ref[pl.ds(ks, SWA_BAND), :]], axis=0)
    vcat = jnp.concatenate([v_ref[n_lat:n_lat + n_ctx, :], v_ref[pl.ds(ks, SWA_BAND), :]], axis=0)
    vt = jnp.transpose(vcat.astype(jnp.float32)).astype(jnp.bfloat16)
    bias = jnp.concatenate([bias_ref[...]] * SWA_REP, axis=1)
    lane = lax.broadcasted_iota(jnp.int32, (TILE, LANES), 1)
    lo_half = lane < SWA_HEAD_DIM
    qcol = lax.broadcasted_iota(jnp.int32, (1, SWA_REP * TILE), 1)
    res = []
    for g in range(SWA_KV_HEADS):
        keep = lo_half if g == 0 else jnp.logical_not(lo_half)
        qg = jnp.concatenate([jnp.where(keep, q_ref[:, c * LANES:(c + 1) * LANES], 0.0).astype(jnp.bfloat16)
                              for c in range(SWA_REP)], axis=0)
        sk = jnp.full((1, SWA_REP * TILE), sink_ref[g * SWA_REP + SWA_REP - 1], jnp.float32)
        for c in range(SWA_REP - 2, -1, -1):
            sk = jnp.where(qcol < (c + 1) * TILE, sink_ref[g * SWA_REP + c], sk)
        st = _dot_nt(kcat, qg) + bias
        m = jnp.maximum(jnp.max(st, axis=0, keepdims=True), sk)
        p = jnp.exp2(st - m)
        l = jnp.sum(p, axis=0, keepdims=True) + jnp.exp2(sk - m)
        res.append(_dot(vt, p.astype(jnp.bfloat16)) / l)
    row = lax.broadcasted_iota(jnp.int32, (LANES, TILE), 0)
    for c in range(SWA_REP):
        pair = jnp.where(row < SWA_HEAD_DIM, res[0][:, c * TILE:(c + 1) * TILE], res[1][:, c * TILE:(c + 1) * TILE])
        o_ref[:, c * LANES:(c + 1) * LANES] = jnp.transpose(pair).astype(jnp.bfloat16)


def _swa_bias(n_ctx, n_lat):
    nt = n_lat // TILE
    r = lax.broadcasted_iota(jnp.int32, (n_ctx + SWA_BAND, TILE), 0)
    q = lax.broadcasted_iota(jnp.int32, (n_ctx + SWA_BAND, TILE), 1)
    out = []
    for j in (0, 1, nt - 1):
        kpos = _swa_band_start(j, n_lat) + r - n_ctx
        valid = (r < n_ctx) | (jnp.abs(j * TILE + q - kpos) <= SWA_WINDOW)
        out.append(jnp.where(valid, 0.0, NEG_BIG).astype(jnp.float32))
    return jnp.stack(out)


def _swa_ctx_body(sink_ref, q_ref, k_ref, v_ref, o_ref):
    _swa_attend(sink_ref, q_ref, k_ref[...], v_ref[...], None, o_ref)


def _swa_attention(sink, qs, ks, vs, bias, n_lat):
    b, t, w = qs.shape
    nt = n_lat // TILE
    assert nt >= 3
    return pl.pallas_call(
        functools.partial(_swa_main_body, n_ctx=t - n_lat, n_lat=n_lat),
        grid=(b, nt),
        in_specs=[
            pl.BlockSpec(memory_space=pltpu.SMEM),
            pl.BlockSpec((None, TILE, w), lambda i, j: (i, j, 0)),
            pl.BlockSpec((None, t, LANES), lambda i, j: (i, 0, 0)),
            pl.BlockSpec((None, t, LANES), lambda i, j: (i, 0, 0)),
            pl.BlockSpec((None,) + bias.shape[1:], lambda i, j: (jnp.where(j == 0, 0, jnp.where(j == nt - 1, 2, 1)), 0, 0)),
        ],
        out_specs=pl.BlockSpec((None, TILE, w), lambda i, j: (i, j, 0)),
        out_shape=jax.ShapeDtypeStruct((b, n_lat, w), jnp.bfloat16),
        compiler_params=_cparams(2),
        name="swa_attention",
    )(sink, qs, ks, vs, bias)


def _swa_attention_ctx(sink, qs, ks, vs, n_lat):
    b, t, w = qs.shape
    n_ctx = t - n_lat
    blk = n_lat // n_ctx
    return pl.pallas_call(
        _swa_ctx_body,
        grid=(b,),
        in_specs=[
            pl.BlockSpec(memory_space=pltpu.SMEM),
            pl.BlockSpec((None, n_ctx, w), lambda i: (i, blk, 0)),
            pl.BlockSpec((None, n_ctx, LANES), lambda i: (i, blk, 0)),
            pl.BlockSpec((None, n_ctx, LANES), lambda i: (i, blk, 0)),
        ],
        out_specs=pl.BlockSpec((None, n_ctx, w), lambda i: (i, 0, 0)),
        out_shape=jax.ShapeDtypeStruct((b, n_ctx, w), jnp.bfloat16),
        compiler_params=_cparams(1),
        name="swa_attention_ctx",
    )(sink, qs, ks, vs)


HALO = 16


def _mix_body(a_ref, b_ref, cz_ref, hp_ref, hn_ref, x_ref, mod_ref, cw_ref, wo_ref, n2_ref, rw_ref,
              x1_ref, aff_ref):
    j = pl.program_id(1)
    d = D_MODEL
    first = j == 0
    last = j == pl.num_programs(1) - 1
    z = cz_ref[:, CONV_CH:2 * CONV_CH].astype(jnp.float32)
    zp = hp_ref[HALO - 1:HALO, CONV_CH:2 * CONV_CH].astype(jnp.float32)
    zn = hn_ref[0:1, CONV_CH:2 * CONV_CH].astype(jnp.float32)
    zp = jnp.where(first, jnp.zeros_like(zp), zp)
    zn = jnp.where(last, jnp.zeros_like(zn), zn)
    row = lax.broadcasted_iota(jnp.int32, z.shape, 0)
    z_dn = jnp.where(row == 0, zp, pltpu.roll(z, 1, 0))
    rows = z.shape[0]
    z_up = jnp.where(row == rows - 1, zn, pltpu.roll(z, rows - 1, 0))
    y = z_dn * cw_ref[0:1, :] + z * cw_ref[1:2, :] + z_up * cw_ref[2:3, :]
    cv = (cz_ref[:, 0:CONV_CH].astype(jnp.float32) * y).astype(jnp.bfloat16)
    mix = jnp.concatenate([a_ref[...], b_ref[...], cv], axis=-1)
    g1 = mod_ref[:, 2 * d:3 * d]
    sh2 = mod_ref[:, 3 * d:4 * d]
    sc2 = mod_ref[:, 4 * d:5 * d]
    x1 = x_ref[...] + g1 * _dot(mix, wo_ref[...])
    x1_ref[...] = x1
    h2 = (_rms(x1, n2_ref[...]) * (1.0 + sc2) + sh2).astype(jnp.bfloat16)
    lg = _dot_nt(rw_ref[...], h2)
    ex = jnp.exp(lg - jnp.max(lg, axis=0, keepdims=True))
    aff_ref[...] = ex / jnp.sum(ex, axis=0, keepdims=True)


MIX_TILE = 1024


def _mixer_out(a, bsw, cz, x, mod_l, mod_row0, lw, frame_row0):
    b, n, d = x.shape
    t = cz.shape[1]
    tm = min(MIX_TILE, n)
    assert n % tm == 0 and frame_row0 % tm == 0
    tile0 = frame_row0 // tm
    hb = tm // HALO
    here = lambda i, j: (i, j, 0)
    const = lambda i, j: (0, 0)
    if mod_row0 is None:
        mod_map = lambda i, j: (i, 0, 0)
    else:
        mod_map = lambda i, j: (mod_row0, 0, 0)
    return pl.pallas_call(
        _mix_body,
        grid=(b, n // tm),
        in_specs=[
            pl.BlockSpec((None, tm, a.shape[2]), here),
            pl.BlockSpec((None, tm, bsw.shape[2]), here),
            pl.BlockSpec((None, tm, cz.shape[2]), lambda i, j: (i, j + tile0, 0)),
            pl.BlockSpec((None, HALO, cz.shape[2]),
                         lambda i, j: (i, jnp.maximum((j + tile0) * hb - 1, 0), 0)),
            pl.BlockSpec((None, HALO, cz.shape[2]),
                         lambda i, j: (i, jnp.minimum((j + tile0 + 1) * hb, t // HALO - 1), 0)),
            pl.BlockSpec((None, tm, d), here),
            pl.BlockSpec((None, 1, N_MOD * d), mod_map),
            pl.BlockSpec((3, CONV_CH), const),
            pl.BlockSpec((d, d), const),
            pl.BlockSpec((1, d), const),
            pl.BlockSpec((N_EXPERTS, d), const),
        ],
        out_specs=[pl.BlockSpec((None, tm, d), here),
                   pl.BlockSpec((None, N_EXPERTS, tm), lambda i, j: (i, 0, j))],
        out_shape=[jax.ShapeDtypeStruct((b, n, d), jnp.float32),
                   jax.ShapeDtypeStruct((b, N_EXPERTS, n), jnp.float32)],
        compiler_params=_cparams(2),
        name="mixer_out",
    )(a, bsw, cz, cz, cz, x, mod_l, lw["cw"], lw["wo"], lw["n2"], lw["rw"])


ROUTE_ROWS = 128
ROUTE_SPLIT = 64
ROUTE_UNROLL = 4


def _route_body(aff_ref, tri_ref, idx_ref, taff_ref, pos_ref, parts_ref, *, cap):
    rows, n = aff_ref.shape
    aff = aff_ref[...]
    bits = pltpu.bitcast(aff, jnp.int32)

    def count(mask):
        return jnp.sum(jnp.where(mask, 1.0, 0.0), axis=1, keepdims=True)

    def ones(mask):
        return jnp.where(mask, 1.0, 0.0).astype(jnp.bfloat16)

    def search(i, thr):
        cand = thr | (1 << (30 - i))
        return jnp.where(count(bits >= cand) >= cap, cand, thr)

    thr = lax.fori_loop(0, 31, search, jnp.zeros((rows, 1), jnp.int32))
    gt = bits > thr
    eq = bits == thr
    need = cap - count(gt)
    peq = _dot(ones(eq), tri_ref[...])
    sel = gt | (eq & (peq < need))
    pos = _dot(ones(sel), tri_ref[...])
    pos_ref[...] = jnp.where(sel, pos, -1.0)

    a1 = aff.astype(jnp.bfloat16).astype(jnp.float32)
    a2 = (aff - a1).astype(jnp.bfloat16).astype(jnp.float32)
    parts_ref[0] = a1
    parts_ref[1] = a2
    parts_ref[2] = aff - a1 - a2
    tok = lax.broadcasted_iota(jnp.int32, (1, n), 1)
    tok_hi = (tok // ROUTE_SPLIT).astype(jnp.float32)
    tok_lo = (tok % ROUTE_SPLIT).astype(jnp.float32)
    sub = lax.broadcasted_iota(jnp.int32, (SUBLANES, n), 0)
    slot = lax.broadcasted_iota(jnp.int32, (cap, 1), 0).astype(jnp.float32)

    def per_row(r, carry):
        onehot = jnp.where(pos_ref[pl.ds(r, 1), :] == slot, 1.0, 0.0).astype(jnp.bfloat16)
        vals = jnp.where(sub == 0, tok_hi, jnp.where(sub == 1, tok_lo, 0.0))
        for k in range(3):
            vals = jnp.where(sub == 2 + k, parts_ref[k, pl.ds(r, 1), :], vals)
        got = _dot_nt(vals.astype(jnp.bfloat16), onehot)
        idx_ref[pl.ds(r, 1), :] = (got[0:1, :] * ROUTE_SPLIT + got[1:2, :]).astype(jnp.int32)
        taff_ref[pl.ds(r, 1), :] = got[2:3, :] + got[3:4, :] + got[4:5, :]
        return carry

    lax.fori_loop(0, rows, per_row, 0, unroll=ROUTE_UNROLL)


def _routing(aff, tri, cap):
    b, e, n = aff.shape
    rows = b * e
    rr = min(ROUTE_ROWS, rows)
    idx, taff = pl.pallas_call(
        functools.partial(_route_body, cap=cap),
        grid=(rows // rr,),
        in_specs=[pl.BlockSpec((rr, n), lambda i: (i, 0)),
                  pl.BlockSpec((n, n), lambda i: (0, 0))],
        out_specs=[pl.BlockSpec((rr, cap), lambda i: (i, 0)),
                   pl.BlockSpec((rr, cap), lambda i: (i, 0))],
        out_shape=[jax.ShapeDtypeStruct((rows, cap), jnp.int32),
                   jax.ShapeDtypeStruct((rows, cap), jnp.float32)],
        scratch_shapes=[pltpu.VMEM((rr, n), jnp.float32),
                        pltpu.VMEM((3, rr, n), jnp.float32)],
        compiler_params=_cparams(1),
        name="routing",
    )(aff.reshape(rows, n), tri)
    return idx.reshape(b, e, cap), taff.reshape(b, e, cap)


EXP_PER_STEP = 2
MOE_CHUNK = 1024
SCATTER_GROUP = 16


def _moe_body(idx_ref, taff_ref, x_ref, mod_ref, n2_ref, fg_ref, wg_ref, wu_ref, wd_ref, o_ref,
              h_ref, acc_ref, *bufs, nch, cap, final):
    j = pl.program_id(1)
    d = D_MODEL
    nsl = d // LANES
    chunk = x_ref.shape[0]
    rows = chunk * nsl
    nes = N_EXPERTS // EXP_PER_STEP
    xg_refs = bufs[:EXP_PER_STEP]
    yb_refs = bufs[EXP_PER_STEP:]

    @pl.when(j < nch)
    def _():
        sh2 = mod_ref[:, 3 * d:4 * d]
        sc2 = mod_ref[:, 4 * d:5 * d]
        h2 = _rms(x_ref[...], n2_ref[...]) * (1.0 + sc2) + sh2
        base = pl.multiple_of(j * rows, rows)
        for s in range(nsl):
            h_ref[pl.ds(base + s, chunk, stride=nsl), :] = h2[:, s * LANES:(s + 1) * LANES]
        acc_ref[pl.ds(base, rows), :] = jnp.zeros((rows, LANES), jnp.float32)

    @pl.when((j >= nch) & (j < nch + nes))
    def _():
        tcol = jnp.transpose(taff_ref[...])
        for k in range(EXP_PER_STEP):
            xg_ref, yb_ref = xg_refs[k], yb_refs[k]
            tok = [pl.multiple_of(idx_ref[0, k * cap + jj], nsl) for jj in range(cap)]
            for jj in range(cap):
                xg_ref[jj * nsl:(jj + 1) * nsl, :] = h_ref[pl.ds(tok[jj], nsl), :]
            xs = jnp.concatenate([xg_ref[pl.ds(s, cap, stride=nsl), :] for s in range(nsl)], axis=-1)
            xs = xs.astype(jnp.bfloat16)
            gate = _dot(xs, wg_ref[k])
            up = _dot(xs, wu_ref[k])
            hid = (_silu(gate) * up).astype(jnp.bfloat16)
            y = _dot(hid, wd_ref[k]) * tcol[k * cap:(k + 1) * cap, 0:1]
            for s in range(nsl):
                yb_ref[pl.ds(s, cap, stride=nsl), :] = y[:, s * LANES:(s + 1) * LANES]
            for g0 in range(0, cap, SCATTER_GROUP):
                grp = range(g0, g0 + SCATTER_GROUP)
                vals = [acc_ref[pl.ds(tok[jj], nsl), :] + yb_ref[jj * nsl:(jj + 1) * nsl, :] for jj in grp]
                for jj, v in zip(grp, vals):
                    acc_ref[pl.ds(tok[jj], nsl), :] = v

    @pl.when(j >= nch + nes)
    def _():
        c = j - nch - nes
        base = pl.multiple_of(c * rows, rows)
        m = jnp.concatenate([acc_ref[pl.ds(base + s, chunk, stride=nsl), :] for s in range(nsl)], axis=-1)
        g2 = mod_ref[:, 5 * d:6 * d]
        x2 = x_ref[...] + g2 * m
        if final:
            x2 = _rms(x2, fg_ref[...])
        o_ref[...] = x2


def _experts(x1, idx, taff, mod_l, mod_row0, lw, final_g, final):
    nb, n_tok, d = x1.shape
    cap = idx.shape[2]
    chunk = MOE_CHUNK if n_tok % MOE_CHUNK == 0 else MOE_CHUNK // 2
    assert n_tok % chunk == 0 and cap % SCATTER_GROUP == 0
    nch = n_tok // chunk
    nes = N_EXPERTS // EXP_PER_STEP
    steps = 2 * nch + nes
    nsl = d // LANES
    idx = (idx * nsl).reshape(nb, nes, 1, EXP_PER_STEP * cap)
    taff = jnp.broadcast_to(taff.reshape(nb, nes, 1, EXP_PER_STEP * cap), (nb, nes, SUBLANES, EXP_PER_STEP * cap))

    def chunk_in(i, j):
        c = jnp.where(j < nch, j, jnp.where(j < nch + nes, nch - 1, j - nch - nes))
        return (i, c, 0)

    def chunk_out(i, j):
        return (i, jnp.clip(j - nch - nes, 0, nch - 1), 0)

    def slots(i, j):
        return (i, jnp.clip(j - nch, 0, nes - 1), 0, 0)

    first = lw["expert0"] // EXP_PER_STEP

    def expert(i, j):
        return (first + jnp.clip(j - nch, 0, nes - 1), 0, 0)

    const = lambda i, j: (0, 0)
    if mod_row0 is None:
        mod_map = lambda i, j: (i, 0, 0)
    else:
        mod_map = lambda i, j: (mod_row0, 0, 0)
    slot_buf = pltpu.VMEM((cap * nsl, LANES), jnp.float32)
    return pl.pallas_call(
        functools.partial(_moe_body, nch=nch, cap=cap, final=final),
        grid=(nb, steps),
        in_specs=[
            pl.BlockSpec((None, None, 1, EXP_PER_STEP * cap), slots, memory_space=pltpu.SMEM),
            pl.BlockSpec((None, None, SUBLANES, EXP_PER_STEP * cap), slots),
            pl.BlockSpec((None, chunk, d), chunk_in),
            pl.BlockSpec((None, 1, N_MOD * d), mod_map),
            pl.BlockSpec((1, d), const),
            pl.BlockSpec((1, d), const),
            pl.BlockSpec((EXP_PER_STEP, d, EXPERT_FF), expert),
            pl.BlockSpec((EXP_PER_STEP, d, EXPERT_FF), expert),
            pl.BlockSpec((EXP_PER_STEP, EXPERT_FF, d), expert),
        ],
        out_specs=pl.BlockSpec((None, chunk, d), chunk_out),
        out_shape=jax.ShapeDtypeStruct((nb, n_tok, d), jnp.float32),
        scratch_shapes=[pltpu.VMEM((n_tok * nsl, LANES), jnp.float32),
                        pltpu.VMEM((n_tok * nsl, LANES), jnp.float32)] + [slot_buf] * (2 * EXP_PER_STEP),
        compiler_params=_cparams(2),
        name="experts",
    )(idx, taff, x1, mod_l, lw["n2"], final_g, lw["wg"], lw["wu"], lw["wd"])


def _group_samples(x1, idx, taff):
    b, n_tok, d = x1.shape
    cap = idx.shape[2]
    g = max(k for k in range(1, 9) if b % k == 0 and (k * n_tok) % (MOE_CHUNK // 2) == 0)
    off = (jnp.arange(b, dtype=jnp.int32) % g) * n_tok
    idx = idx + off[:, None, None]

    def merge(t):
        t = t.reshape(b // g, g, N_EXPERTS, cap)
        return jnp.swapaxes(t, 1, 2).reshape(b // g, N_EXPERTS, g * cap)

    return x1.reshape(b // g, g * n_tok, d), merge(idx), merge(taff)


def _swap_halves(w):
    half = w.shape[-1] // 2
    return jnp.concatenate([-w[..., half:], w[..., :half]], axis=-1)


def _swa_perm():
    cols = []
    for c in range(SWA_REP):
        for g in range(SWA_KV_HEADS):
            h = g * SWA_REP + c
            cols.extend(range(h * SWA_HEAD_DIM, (h + 1) * SWA_HEAD_DIM))
    return np.asarray(cols, np.int32)


def _layer_weights(w_in, q_g, w_uq, kv_g, w_ukv, conv_w, w_o, n1, n2, router_w):
    bf = jnp.bfloat16
    d = w_in.shape[0]
    offs = np.cumsum([0, MLA_Q_RANK, MLA_KV_RANK, MLA_ROPE, SWA_HEADS * SWA_HEAD_DIM,
                      SWA_KV_HEADS * SWA_HEAD_DIM, SWA_KV_HEADS * SWA_HEAD_DIM, CONV_CH, CONV_CH, CONV_CH])
    part = [w_in[:, offs[i]:offs[i + 1]] for i in range(9)]
    w_cq, w_ckv, w_kr, w_sq, w_sk, w_sv, w_cb, w_cc, w_cu = part
    perm = _swa_perm()

    kr_block = jnp.concatenate([w_kr, _swap_halves(w_kr), jnp.zeros((d, LANES - 2 * MLA_ROPE), w_in.dtype)], axis=1)
    win = jnp.concatenate([
        w_cq, w_ckv, kr_block,
        w_sq[:, perm], w_sk, w_sv, w_cb, w_cc, w_cu], axis=1).astype(bf)

    qd = MLA_NOPE + MLA_ROPE
    uq = w_uq.reshape(MLA_Q_RANK, MLA_HEADS, qd)
    zpad = jnp.zeros((MLA_Q_RANK, MLA_HEADS, HEAD_PAD - qd), w_uq.dtype)
    main = jnp.concatenate([uq, zpad], axis=-1)
    swp = jnp.concatenate([jnp.zeros_like(uq[..., :MLA_NOPE]), _swap_halves(uq[..., MLA_NOPE:]), zpad], axis=-1)
    wuq = jnp.concatenate([main.reshape(MLA_Q_RANK, -1), swp.reshape(MLA_Q_RANK, -1)], axis=1).astype(bf)

    ukv = w_ukv.reshape(MLA_KV_RANK, MLA_HEADS, MLA_NOPE + MLA_V)
    wk = jnp.concatenate([ukv[..., :MLA_NOPE],
                          jnp.zeros((MLA_KV_RANK, MLA_HEADS, HEAD_PAD - MLA_NOPE), w_ukv.dtype)], axis=-1)
    wk = wk.reshape(MLA_KV_RANK, -1).astype(bf)
    wv = ukv[..., MLA_NOPE:].reshape(MLA_KV_RANK, -1).astype(bf)

    e = np.zeros((LANES, MLA_HEADS * HEAD_PAD), np.float32)
    for h in range(MLA_HEADS):
        for l in range(MLA_ROPE):
            e[l, h * HEAD_PAD + MLA_NOPE + l] = 1.0
            e[MLA_ROPE + l, h * HEAD_PAD + MLA_NOPE + l] = 1.0

    n_mla = MLA_HEADS * MLA_V
    n_swa = SWA_HEADS * SWA_HEAD_DIM
    wo = jnp.concatenate([w_o[:n_mla], w_o[n_mla:n_mla + n_swa][perm], w_o[n_mla + n_swa:]], axis=0).astype(bf)
    return dict(
        g1=n1.reshape(1, -1), win=win, qg=q_g.reshape(1, -1), wuq=wuq, kvg=kv_g.reshape(1, -1),
        wk=wk, wvt=wv.T, e=jnp.asarray(e, bf), cw=conv_w, wo=wo, n2=n2.reshape(1, -1),
        rw=router_w.T.astype(bf))


def _axial_tables(n_tokens, rot_dim):
    rows = n_tokens // GRID_W
    row = jnp.repeat(jnp.arange(rows, dtype=jnp.float32), GRID_W)
    col = jnp.tile(jnp.arange(GRID_W, dtype=jnp.float32), rows)
    n_freq = rot_dim // 4
    inv = ROPE_BASE ** (-jnp.arange(n_freq, dtype=jnp.float32) / n_freq)
    ang = jnp.concatenate([row[:, None] * inv, col[:, None] * inv], axis=-1)
    return jnp.cos(ang), jnp.sin(ang)


def _row_table(n_ctx, n_lat):
    def with_ctx(cos, sin):
        one = jnp.ones((n_ctx, cos.shape[1]), jnp.float32)
        return (jnp.concatenate([cos, one], axis=0), jnp.concatenate([sin, 0.0 * one], axis=0))

    cm, sm = with_ctx(*_axial_tables(n_lat, MLA_ROPE))
    cs, ss = with_ctx(*_axial_tables(n_lat, SWA_HEAD_DIM))
    t = n_ctx + n_lat
    cm2 = jnp.concatenate([cm, cm], axis=1)
    sm2 = jnp.concatenate([sm, sm], axis=1)
    cs2 = jnp.concatenate([cs, cs], axis=1)
    ss2 = jnp.concatenate([-ss, ss], axis=1)
    pad = jnp.zeros((t, HEAD_PAD - MLA_NOPE - MLA_ROPE), jnp.float32)
    cq = jnp.concatenate([jnp.ones((t, MLA_NOPE), jnp.float32), cm2, pad], axis=1) * (MLA_SCALE * LOG2E)
    sq = jnp.concatenate([jnp.zeros((t, MLA_NOPE), jnp.float32), sm2, pad], axis=1) * (MLA_SCALE * LOG2E)
    kr = jnp.concatenate([cm2, sm2, jnp.zeros((t, LANES - 2 * MLA_ROPE), jnp.float32)], axis=1)
    tab = jnp.concatenate([
        jnp.tile(cq, (1, MLA_HEADS)), jnp.tile(sq, (1, MLA_HEADS)), kr,
        jnp.tile(cs2, (1, SWA_HEADS)) * (SWA_SCALE * LOG2E), jnp.tile(ss2, (1, SWA_HEADS)) * (SWA_SCALE * LOG2E),
        jnp.tile(cs2, (1, SWA_KV_HEADS)), jnp.tile(ss2, (1, SWA_KV_HEADS))], axis=1)
    return tab


def _strict_upper(n):
    r = lax.broadcasted_iota(jnp.int32, (n, n), 0)
    c = lax.broadcasted_iota(jnp.int32, (n, n), 1)
    return (r < c).astype(jnp.bfloat16)


def kernel(x, c, ctx, c_ctx, ada_w, ada_b, norm1_g, w_in, mla_q_norm_g, mla_w_uq, mla_kv_norm_g, mla_w_ukv,
           swa_sink, conv_w, w_o, norm2_g, router_w, exp_w_gate, exp_w_up, exp_w_down, final_norm_g):
    b, s, d = x.shape
    n_ctx = ctx.shape[1]
    depth = ada_w.shape[0]
    assert d == D_MODEL and s % MLA_TQ == 0 and n_ctx == TILE and b + 1 <= MOD_ROWS
    cap_lat = EC_CAPACITY_FACTOR * s // N_EXPERTS
    cap_ctx = EC_CAPACITY_FACTOR * n_ctx // N_EXPERTS

    cc = jnp.concatenate([c, c_ctx[None, :], jnp.zeros((MOD_ROWS - b - 1, d), c.dtype)], axis=0)
    mod = _modulation(cc, ada_w, ada_b).reshape(depth, MOD_ROWS, 1, N_MOD * d)
    tab = _row_table(n_ctx, s)
    tri_lat = _strict_upper(s)
    tri_ctx = _strict_upper(n_ctx)
    swa_bias = _swa_bias(n_ctx, s)
    sink_slots = jnp.zeros((8,), jnp.float32)
    fg = final_norm_g.reshape(1, d)
    wg_all = exp_w_gate.astype(jnp.bfloat16).reshape((depth * N_EXPERTS,) + exp_w_gate.shape[2:])
    wu_all = exp_w_up.astype(jnp.bfloat16).reshape((depth * N_EXPERTS,) + exp_w_up.shape[2:])
    wd_all = exp_w_down.astype(jnp.bfloat16).reshape((depth * N_EXPERTS,) + exp_w_down.shape[2:])

    xl, xc = x, ctx
    for li in range(depth):
        last = li == depth - 1
        lw = _layer_weights(w_in[li], mla_q_norm_g[li], mla_w_uq[li], mla_kv_norm_g[li], mla_w_ukv[li],
                            conv_w[li], w_o[li], norm1_g[li], norm2_g[li], router_w[li])
        lw.update(wg=wg_all, wu=wu_all, wd=wd_all, expert0=li * N_EXPERTS)
        mod_l = mod[li]
        sink = sink_slots.at[:SWA_HEADS].set(swa_sink[li] * LOG2E)
        qm, km, vmt, qs, ks, vs, cz = _projections(xl, xc, mod_l, lw, tab)
        a = _mla_attention(qm, km, vmt, s)
        bsw = _swa_attention(sink, qs, ks, vs, swa_bias, s)
        x1, aff = _mixer_out(a, bsw, cz, xl, mod_l, None, lw, 0)
        idx, taff = _routing(aff, tri_lat, cap_lat)
        if not last:
            ac = _mla_attention_ctx(qm, km, vmt, s)
            bc = _swa_attention_ctx(sink, qs, ks, vs, s)
            xc1, affc = _mixer_out(ac, bc, cz, xc, mod_l, b, lw, s)
            idxc, taffc = _routing(affc, tri_ctx, cap_ctx)
            xc = _experts(*_group_samples(xc1, idxc, taffc), mod_l, b, lw, fg, final=False).reshape(xc1.shape)
        xl = _experts(x1, idx, taff, mod_l, None, lw, fg, final=last)
    return xl
```

```python
import functools

import jax
import jax.numpy as jnp
import numpy as np
from jax import lax
from jax.experimental import pallas as pl
from jax.experimental.pallas import tpu as pltpu

D_MODEL = 1024
GRID_W = 64
NORM_EPS = 1e-6
ROPE_BASE = 10000.0

MLA_HEADS = 6
MLA_Q_RANK = 256
MLA_KV_RANK = 128
MLA_NOPE = 64
MLA_ROPE = 32
MLA_V = 64
MLA_SCALE = (MLA_NOPE + MLA_ROPE) ** -0.5

SWA_HEADS = 6
SWA_KV_HEADS = 2
SWA_REP = SWA_HEADS // SWA_KV_HEADS
SWA_HEAD_DIM = 64
SWA_WINDOW = 128
SWA_SCALE = SWA_HEAD_DIM ** -0.5

CONV_CH = 256
N_EXPERTS = 16
EXPERT_FF = 512
EC_CAPACITY_FACTOR = 2
N_MOD = 6

LANES = 128
SUBLANES = 8
TILE = 256
HEAD_PAD = 128
MOD_ROWS = 40
NEG_BIG = -1e30
LOG2E = 1.4426950408889634

_C_CQ = 0
_C_CKV = 256
_C_KR = 384
_C_SQ = 512
_C_SK = 896
_C_SV = 1024
_C_CB = 1152
_C_CC = 1408
_C_CU = 1664
IN_W = 1920

_T_CQ = 0
_T_SQ = 768
_T_KR = 1536
_T_CS = 1664
_T_SS = 2048
_T_CK = 2432
_T_SK = 2560
TAB_W = 2688

_VMEM_LIMIT = 56 * 1024 * 1024


def _cparams(n_grid):
    return pltpu.CompilerParams(dimension_semantics=("arbitrary",) * n_grid,
                                vmem_limit_bytes=_VMEM_LIMIT)


def _silu(v):
    return v * (1.0 / (1.0 + jnp.exp(-v)))


def _rms(v, g):
    return v * lax.rsqrt(jnp.mean(v * v, axis=-1, keepdims=True) + NORM_EPS) * g


def _dot(a, b):
    return jnp.dot(a, b, preferred_element_type=jnp.float32)


def _dot_nt(a, b):
    return lax.dot_general(a, b, (((1,), (1,)), ((), ())), preferred_element_type=jnp.float32)


def _mod_body(c_ref, w_ref, b_ref, o_ref):
    a = _silu(c_ref[...]).astype(jnp.bfloat16)
    o_ref[...] = _dot(a, w_ref[...].astype(jnp.bfloat16)) + b_ref[...]


def _modulation(cc, ada_w, ada_b):
    depth, d, n = ada_w.shape
    bn = 512
    return pl.pallas_call(
        _mod_body,
        grid=(depth, n // bn),
        in_specs=[
            pl.BlockSpec((MOD_ROWS, d), lambda l, i: (0, 0)),
            pl.BlockSpec((None, d, bn), lambda l, i: (l, 0, i)),
            pl.BlockSpec((None, 1, bn), lambda l, i: (l, 0, i)),
        ],
        out_specs=pl.BlockSpec((None, MOD_ROWS, bn), lambda l, i: (l, 0, i)),
        out_shape=jax.ShapeDtypeStruct((depth, MOD_ROWS, n), jnp.float32),
        compiler_params=_cparams(2),
        name="modulation",
    )(cc, ada_w, ada_b.reshape(depth, 1, n))


def _proj_body(x_ref, ctx_ref, mod_ref, g1_ref, win_ref, qg_ref, wuq_ref, kvg_ref, wk_ref, wvt_ref,
               e_ref, tab_ref, qm_ref, km_ref, vmt_ref, qs_ref, ks_ref, vs_ref, cz_ref, xs_ref):
    j = pl.program_id(0)
    is_ctx = j == pl.num_programs(0) - 1

    @pl.when(is_ctx)
    def _():
        xs_ref[...] = ctx_ref[...]

    @pl.when(jnp.logical_not(is_ctx))
    def _():
        xs_ref[...] = x_ref[...]

    d = D_MODEL
    sh1 = mod_ref[:, 0:d]
    sc1 = mod_ref[:, d:2 * d]
    h = (_rms(xs_ref[...], g1_ref[...]) * (1.0 + sc1) + sh1).astype(jnp.bfloat16)

    def proj(lo, hi):
        return _dot(h, win_ref[:, lo:hi])

    def tab(lo, n):
        return tab_ref[:, lo:lo + n]

    cq = _rms(proj(_C_CQ, _C_CKV), qg_ref[...]).astype(jnp.bfloat16)
    uq = _dot(cq, wuq_ref[...])
    lane = lax.broadcasted_iota(jnp.int32, (h.shape[0], LANES), 1)
    rope_lo = (lane & (MLA_ROPE // 2)) == 0
    for c in range(MLA_HEADS):
        x = uq[:, c * HEAD_PAD:(c + 1) * HEAD_PAD]
        swapped = jnp.where(rope_lo, pltpu.roll(x, HEAD_PAD - MLA_ROPE // 2, 1), pltpu.roll(x, MLA_ROPE // 2, 1))
        qm_ref[:, c * HEAD_PAD:(c + 1) * HEAD_PAD] = (
            x * tab(_T_CQ + c * HEAD_PAD, HEAD_PAD) + swapped * tab(_T_SQ + c * HEAD_PAD, HEAD_PAD)
        ).astype(jnp.bfloat16)

    ckv = _rms(proj(_C_CKV, _C_KR), kvg_ref[...]).astype(jnp.bfloat16)
    krp = (proj(_C_KR, _C_SQ) * tab(_T_KR, LANES)).astype(jnp.bfloat16)
    km_ref[...] = (_dot(ckv, wk_ref[...]) + _dot(krp, e_ref[...])).astype(jnp.bfloat16)
    vmt_ref[...] = _dot_nt(wvt_ref[...], ckv).astype(jnp.bfloat16)

    nsq = SWA_HEADS * SWA_HEAD_DIM
    first_half = (lane & (SWA_HEAD_DIM // 2)) == 0

    def rotary(x, c_off, s_off):
        half = SWA_HEAD_DIM // 2
        swapped = jnp.where(first_half, pltpu.roll(x, LANES - half, 1), pltpu.roll(x, half, 1))
        return (x * tab(c_off, LANES) + swapped * tab(s_off, LANES)).astype(jnp.bfloat16)

    uq = proj(_C_SQ, _C_SK)
    for c in range(nsq // LANES):
        qs_ref[:, c * LANES:(c + 1) * LANES] = rotary(uq[:, c * LANES:(c + 1) * LANES],
                                                       _T_CS + c * LANES, _T_SS + c * LANES)
    ks_ref[...] = rotary(proj(_C_SK, _C_SV), _T_CK, _T_SK)
    vs_ref[...] = proj(_C_SV, _C_CB).astype(jnp.bfloat16)

    cz_ref[:, 0:CONV_CH] = proj(_C_CB, _C_CC).astype(jnp.bfloat16)
    cz_ref[:, CONV_CH:2 * CONV_CH] = (proj(_C_CC, _C_CU) * proj(_C_CU, IN_W)).astype(jnp.bfloat16)


def _projections(x, ctx, mod_l, lw, tab):
    b, s, d = x.shape
    n_ctx = ctx.shape[1]
    t = n_ctx + s
    nl = s // TILE
    const = lambda j, i: (0, 0)
    row_outs = [MLA_HEADS * HEAD_PAD, MLA_HEADS * HEAD_PAD, None, SWA_HEADS * SWA_HEAD_DIM, LANES, LANES,
                2 * CONV_CH]
    nv = MLA_HEADS * MLA_V
    out_specs = [pl.BlockSpec((None, nv, TILE), lambda j, i: (i, 0, j)) if w is None
                 else pl.BlockSpec((None, TILE, w), lambda j, i: (i, j, 0)) for w in row_outs]
    out_shape = [jax.ShapeDtypeStruct((b, nv, t) if w is None else (b, t, w), jnp.bfloat16) for w in row_outs]
    return pl.pallas_call(
        _proj_body,
        grid=(nl + 1, b),
        in_specs=[
            pl.BlockSpec((None, TILE, d), lambda j, i: (jnp.where(j == nl, 0, i), jnp.minimum(j, nl - 1), 0)),
            pl.BlockSpec((None, TILE, d), lambda j, i: (jnp.where(j == nl, i, 0), 0, 0)),
            pl.BlockSpec((None, 1, N_MOD * d), lambda j, i: (jnp.where(j == nl, b, i), 0, 0)),
            pl.BlockSpec((1, d), const),
            pl.BlockSpec((d, IN_W), const),
            pl.BlockSpec((1, MLA_Q_RANK), const),
            pl.BlockSpec((MLA_Q_RANK, MLA_HEADS * HEAD_PAD), const),
            pl.BlockSpec((1, MLA_KV_RANK), const),
            pl.BlockSpec((MLA_KV_RANK, MLA_HEADS * HEAD_PAD), const),
            pl.BlockSpec((nv, MLA_KV_RANK), const),
            pl.BlockSpec((LANES, MLA_HEADS * HEAD_PAD), const),
            pl.BlockSpec((TILE, TAB_W), lambda j, i: (j, 0)),
        ],
        out_specs=out_specs,
        out_shape=out_shape,
        scratch_shapes=[pltpu.VMEM((TILE, d), jnp.float32)],
        compiler_params=_cparams(2),
        name="projections",
    )(x, ctx, mod_l, lw["g1"], lw["win"], lw["qg"], lw["wuq"], lw["kvg"], lw["wk"], lw["wvt"], lw["e"], tab)


MLA_TQ = 2048


def _mla_pair(q_ref, k_ref, vt_ref, o_ref, c):
    tq = q_ref.shape[0]
    res = []
    for hh in range(2):
        lo = (2 * c + hh) * HEAD_PAD
        st = _dot_nt(k_ref[:, lo:lo + HEAD_PAD], q_ref[:, lo:lo + HEAD_PAD])
        m = jnp.max(st, axis=0, keepdims=True)
        p = jnp.exp2(st - m)
        l = jnp.sum(p, axis=0, keepdims=True)
        ot = _dot(vt_ref[c * LANES:(c + 1) * LANES, :], p.astype(jnp.bfloat16))
        res.append(ot / l)
    row = lax.broadcasted_iota(jnp.int32, (LANES, tq), 0)
    o_ref[:, c * LANES:(c + 1) * LANES] = jnp.transpose(
        jnp.where(row < MLA_V, res[0], res[1])).astype(jnp.bfloat16)


def _mla_main_body(q_ref, k_ref, vt_ref, o_ref):
    for c in range(MLA_HEADS // 2):
        _mla_pair(q_ref, k_ref, vt_ref, o_ref, c)


def _mla_ctx_body(q_ref, k_ref, vt_ref, o_ref):
    for c in range(MLA_HEADS // 2):
        _mla_pair(q_ref, k_ref, vt_ref, o_ref, c)


def _mla_attention(qm, km, vmt, n_lat):
    b, t, _ = qm.shape
    return pl.pallas_call(
        _mla_main_body,
        grid=(b, n_lat // MLA_TQ),
        in_specs=[
            pl.BlockSpec((None, MLA_TQ, qm.shape[2]), lambda i, j: (i, j, 0)),
            pl.BlockSpec((None, t, km.shape[2]), lambda i, j: (i, 0, 0)),
            pl.BlockSpec((None, vmt.shape[1], t), lambda i, j: (i, 0, 0)),
        ],
        out_specs=pl.BlockSpec((None, MLA_TQ, vmt.shape[1]), lambda i, j: (i, j, 0)),
        out_shape=jax.ShapeDtypeStruct((b, n_lat, vmt.shape[1]), jnp.bfloat16),
        compiler_params=_cparams(2),
        name="mla_attention",
    )(qm, km, vmt)


def _mla_attention_ctx(qm, km, vmt, n_lat):
    b, t, _ = qm.shape
    n_ctx = t - n_lat
    blk = n_lat // n_ctx
    return pl.pallas_call(
        _mla_ctx_body,
        grid=(b,),
        in_specs=[
            pl.BlockSpec((None, n_ctx, qm.shape[2]), lambda i: (i, blk, 0)),
            pl.BlockSpec((None, n_ctx, km.shape[2]), lambda i: (i, blk, 0)),
            pl.BlockSpec((None, vmt.shape[1], n_ctx), lambda i: (i, 0, blk)),
        ],
        out_specs=pl.BlockSpec((None, n_ctx, vmt.shape[1]), lambda i: (i, 0, 0)),
        out_shape=jax.ShapeDtypeStruct((b, n_ctx, vmt.shape[1]), jnp.bfloat16),
        compiler_params=_cparams(1),
        name="mla_attention_ctx",
    )(qm, km, vmt)


def _swa_attend(sink_ref, q_ref, kcat, vcat, valid, o_ref):
    rows = q_ref.shape[0]
    lane = lax.broadcasted_iota(jnp.int32, (rows, LANES), 1)
    lo_half = lane < SWA_HEAD_DIM
    for c in range(SWA_REP):
        q2 = q_ref[:, c * LANES:(c + 1) * LANES]
        res = []
        for g in range(SWA_KV_HEADS):
            keep = lo_half if g == 0 else jnp.logical_not(lo_half)
            qg = jnp.where(keep, q2, jnp.zeros_like(q2))
            s = _dot_nt(qg, kcat)
            if valid is not None:
                s = jnp.where(valid, s, NEG_BIG)
            sk = sink_ref[g * SWA_REP + c]
            m = jnp.maximum(jnp.max(s, axis=-1, keepdims=True), sk)
            p = jnp.exp2(s - m)
            l = jnp.sum(p, axis=-1, keepdims=True) + jnp.exp2(sk - m)
            res.append(_dot(p.astype(jnp.bfloat16), vcat) / l)
        o_ref[:, c * LANES:(c + 1) * LANES] = jnp.where(lo_half, res[0], res[1]).astype(jnp.bfloat16)


SWA_BAND = TILE + 2 * SWA_WINDOW


def _swa_band_start(j, n_lat):
    return jnp.clip(j * TILE - SWA_WINDOW, 0, n_lat - SWA_BAND)


def _swa_main_body(sink_ref, q_ref, k_ref, v_ref, bias_ref, o_ref, *, n_ctx, n_lat):
    ks = pl.multiple_of(_swa_band_start(pl.program_id(1), n_lat), SWA_WINDOW)
    kcat = jnp.concatenate([k_ref[n_lat:n_lat + n_ctx, :], k_ref[pl.ds(ks, SWA_BAND), :]], axis=0)
    vcat = jnp.concatenate([v_ref[n_lat:n_lat + n_ctx, :], v_ref[pl.ds(ks, SWA_BAND), :]], axis=0)
    vt = jnp.transpose(vcat.astype(jnp.float32)).astype(jnp.bfloat16)
    bias = jnp.concatenate([bias_ref[...]] * SWA_REP, axis=1)
    lane = lax.broadcasted_iota(jnp.int32, (TILE, LANES), 1)
    lo_half = lane < SWA_HEAD_DIM
    qcol = lax.broadcasted_iota(jnp.int32, (1, SWA_REP * TILE), 1)
    res = []
    for g in range(SWA_KV_HEADS):
        keep = lo_half if g == 0 else jnp.logical_not(lo_half)
        qg = jnp.concatenate([jnp.where(keep, q_ref[:, c * LANES:(c + 1) * LANES], 0.0).astype(jnp.bfloat16)
                              for c in range(SWA_REP)], axis=0)
        sk = jnp.full((1, SWA_REP * TILE), sink_ref[g * SWA_REP + SWA_REP - 1], jnp.float32)
        for c in range(SWA_REP - 2, -1, -1):
            sk = jnp.where(qcol < (c + 1) * TILE, sink_ref[g * SWA_REP + c], sk)
        st = _dot_nt(kcat, qg) + bias
        m = jnp.maximum(jnp.max(st, axis=0, keepdims=True), sk)
        p = jnp.exp2(st - m)
        l = jnp.sum(p, axis=0, keepdims=True) + jnp.exp2(sk - m)
        res.append(_dot(vt, p.astype(jnp.bfloat16)) / l)
    row = lax.broadcasted_iota(jnp.int32, (LANES, TILE), 0)
    for c in range(SWA_REP):
        pair = jnp.where(row < SWA_HEAD_DIM, res[0][:, c * TILE:(c + 1) * TILE], res[1][:, c * TILE:(c + 1) * TILE])
        o_ref[:, c * LANES:(c + 1) * LANES] = jnp.transpose(pair).astype(jnp.bfloat16)


def _swa_bias(n_ctx, n_lat):
    nt = n_lat // TILE
    r = lax.broadcasted_iota(jnp.int32, (n_ctx + SWA_BAND, TILE), 0)
    q = lax.broadcasted_iota(jnp.int32, (n_ctx + SWA_BAND, TILE), 1)
    out = []
    for j in (0, 1, nt - 1):
        kpos = _swa_band_start(j, n_lat) + r - n_ctx
        valid = (r < n_ctx) | (jnp.abs(j * TILE + q - kpos) <= SWA_WINDOW)
        out.append(jnp.where(valid, 0.0, NEG_BIG).astype(jnp.float32))
    return jnp.stack(out)


def _swa_ctx_body(sink_ref, q_ref, k_ref, v_ref, o_ref):
    _swa_attend(sink_ref, q_ref, k_ref[...], v_ref[...], None, o_ref)


def _swa_attention(sink, qs, ks, vs, bias, n_lat):
    b, t, w = qs.shape
    nt = n_lat // TILE
    assert nt >= 3
    return pl.pallas_call(
        functools.partial(_swa_main_body, n_ctx=t - n_lat, n_lat=n_lat),
        grid=(b, nt),
        in_specs=[
            pl.BlockSpec(memory_space=pltpu.SMEM),
            pl.BlockSpec((None, TILE, w), lambda i, j: (i, j, 0)),
            pl.BlockSpec((None, t, LANES), lambda i, j: (i, 0, 0)),
            pl.BlockSpec((None, t, LANES), lambda i, j: (i, 0, 0)),
            pl.BlockSpec((None,) + bias.shape[1:], lambda i, j: (jnp.where(j == 0, 0, jnp.where(j == nt - 1, 2, 1)), 0, 0)),
        ],
        out_specs=pl.BlockSpec((None, TILE, w), lambda i, j: (i, j, 0)),
        out_shape=jax.ShapeDtypeStruct((b, n_lat, w), jnp.bfloat16),
        compiler_params=_cparams(2),
        name="swa_attention",
    )(sink, qs, ks, vs, bias)


def _swa_attention_ctx(sink, qs, ks, vs, n_lat):
    b, t, w = qs.shape
    n_ctx = t - n_lat
    blk = n_lat // n_ctx
    return pl.pallas_call(
        _swa_ctx_body,
        grid=(b,),
        in_specs=[
            pl.BlockSpec(memory_space=pltpu.SMEM),
            pl.BlockSpec((None, n_ctx, w), lambda i: (i, blk, 0)),
            pl.BlockSpec((None, n_ctx, LANES), lambda i: (i, blk, 0)),
            pl.BlockSpec((None, n_ctx, LANES), lambda i: (i, blk, 0)),
        ],
        out_specs=pl.BlockSpec((None, n_ctx, w), lambda i: (i, 0, 0)),
        out_shape=jax.ShapeDtypeStruct((b, n_ctx, w), jnp.bfloat16),
        compiler_params=_cparams(1),
        name="swa_attention_ctx",
    )(sink, qs, ks, vs)


HALO = 16


def _mix_body(a_ref, b_ref, cz_ref, hp_ref, hn_ref, x_ref, mod_ref, cw_ref, wo_ref, n2_ref, rw_ref,
              x1_ref, aff_ref):
    j = pl.program_id(1)
    d = D_MODEL
    first = j == 0
    last = j == pl.num_programs(1) - 1
    z = cz_ref[:, CONV_CH:2 * CONV_CH].astype(jnp.float32)
    zp = hp_ref[HALO - 1:HALO, CONV_CH:2 * CONV_CH].astype(jnp.float32)
    zn = hn_ref[0:1, CONV_CH:2 * CONV_CH].astype(jnp.float32)
    zp = jnp.where(first, jnp.zeros_like(zp), zp)
    zn = jnp.where(last, jnp.zeros_like(zn), zn)
    row = lax.broadcasted_iota(jnp.int32, z.shape, 0)
    z_dn = jnp.where(row == 0, zp, pltpu.roll(z, 1, 0))
    rows = z.shape[0]
    z_up = jnp.where(row == rows - 1, zn, pltpu.roll(z, rows - 1, 0))
    y = z_dn * cw_ref[0:1, :] + z * cw_ref[1:2, :] + z_up * cw_ref[2:3, :]
    cv = (cz_ref[:, 0:CONV_CH].astype(jnp.float32) * y).astype(jnp.bfloat16)
    mix = jnp.concatenate([a_ref[...], b_ref[...], cv], axis=-1)
    g1 = mod_ref[:, 2 * d:3 * d]
    sh2 = mod_ref[:, 3 * d:4 * d]
    sc2 = mod_ref[:, 4 * d:5 * d]
    x1 = x_ref[...] + g1 * _dot(mix, wo_ref[...])
    x1_ref[...] = x1
    h2 = (_rms(x1, n2_ref[...]) * (1.0 + sc2) + sh2).astype(jnp.bfloat16)
    lg = _dot_nt(rw_ref[...], h2)
    ex = jnp.exp(lg - jnp.max(lg, axis=0, keepdims=True))
    aff_ref[...] = ex / jnp.sum(ex, axis=0, keepdims=True)


MIX_TILE = 1024


def _mixer_out(a, bsw, cz, x, mod_l, mod_row0, lw, frame_row0):
    b, n, d = x.shape
    t = cz.shape[1]
    tm = min(MIX_TILE, n)
    assert n % tm == 0 and frame_row0 % tm == 0
    tile0 = frame_row0 // tm
    hb = tm // HALO
    here = lambda i, j: (i, j, 0)
    const = lambda i, j: (0, 0)
    if mod_row0 is None:
        mod_map = lambda i, j: (i, 0, 0)
    else:
        mod_map = lambda i, j: (mod_row0, 0, 0)
    return pl.pallas_call(
        _mix_body,
        grid=(b, n // tm),
        in_specs=[
            pl.BlockSpec((None, tm, a.shape[2]), here),
            pl.BlockSpec((None, tm, bsw.shape[2]), here),
            pl.BlockSpec((None, tm, cz.shape[2]), lambda i, j: (i, j + tile0, 0)),
            pl.BlockSpec((None, HALO, cz.shape[2]),
                         lambda i, j: (i, jnp.maximum((j + tile0) * hb - 1, 0), 0)),
            pl.BlockSpec((None, HALO, cz.shape[2]),
                         lambda i, j: (i, jnp.minimum((j + tile0 + 1) * hb, t // HALO - 1), 0)),
            pl.BlockSpec((None, tm, d), here),
            pl.BlockSpec((None, 1, N_MOD * d), mod_map),
            pl.BlockSpec((3, CONV_CH), const),
            pl.BlockSpec((d, d), const),
            pl.BlockSpec((1, d), const),
            pl.BlockSpec((N_EXPERTS, d), const),
        ],
        out_specs=[pl.BlockSpec((None, tm, d), here),
                   pl.BlockSpec((None, N_EXPERTS, tm), lambda i, j: (i, 0, j))],
        out_shape=[jax.ShapeDtypeStruct((b, n, d), jnp.float32),
                   jax.ShapeDtypeStruct((b, N_EXPERTS, n), jnp.float32)],
        compiler_params=_cparams(2),
        name="mixer_out",
    )(a, bsw, cz, cz, cz, x, mod_l, lw["cw"], lw["wo"], lw["n2"], lw["rw"])


ROUTE_ROWS = 128
ROUTE_SPLIT = 64
ROUTE_UNROLL = 4


def _route_body(aff_ref, tri_ref, idx_ref, taff_ref, pos_ref, parts_ref, *, cap):
    rows, n = aff_ref.shape
    aff = aff_ref[...]
    bits = pltpu.bitcast(aff, jnp.int32)

    def count(mask):
        return jnp.sum(jnp.where(mask, 1.0, 0.0), axis=1, keepdims=True)

    def ones(mask):
        return jnp.where(mask, 1.0, 0.0).astype(jnp.bfloat16)

    def search(i, thr):
        cand = thr | (1 << (30 - i))
        return jnp.where(count(bits >= cand) >= cap, cand, thr)

    thr = lax.fori_loop(0, 31, search, jnp.zeros((rows, 1), jnp.int32))
    gt = bits > thr
    eq = bits == thr
    need = cap - count(gt)
    peq = _dot(ones(eq), tri_ref[...])
    sel = gt | (eq & (peq < need))
    pos = _dot(ones(sel), tri_ref[...])
    pos_ref[...] = jnp.where(sel, pos, -1.0)

    a1 = aff.astype(jnp.bfloat16).astype(jnp.float32)
    a2 = (aff - a1).astype(jnp.bfloat16).astype(jnp.float32)
    parts_ref[0] = a1
    parts_ref[1] = a2
    parts_ref[2] = aff - a1 - a2
    tok = lax.broadcasted_iota(jnp.int32, (1, n), 1)
    tok_hi = (tok // ROUTE_SPLIT).astype(jnp.float32)
    tok_lo = (tok % ROUTE_SPLIT).astype(jnp.float32)
    sub = lax.broadcasted_iota(jnp.int32, (SUBLANES, n), 0)
    slot = lax.broadcasted_iota(jnp.int32, (cap, 1), 0).astype(jnp.float32)

    def per_row(r, carry):
        onehot = jnp.where(pos_ref[pl.ds(r, 1), :] == slot, 1.0, 0.0).astype(jnp.bfloat16)
        vals = jnp.where(sub == 0, tok_hi, jnp.where(sub == 1, tok_lo, 0.0))
        for k in range(3):
            vals = jnp.where(sub == 2 + k, parts_ref[k, pl.ds(r, 1), :], vals)
        got = _dot_nt(vals.astype(jnp.bfloat16), onehot)
        idx_ref[pl.ds(r, 1), :] = (got[0:1, :] * ROUTE_SPLIT + got[1:2, :]).astype(jnp.int32)
        taff_ref[pl.ds(r, 1), :] = got[2:3, :] + got[3:4, :] + got[4:5, :]
        return carry

    lax.fori_loop(0, rows, per_row, 0, unroll=ROUTE_UNROLL)


def _routing(aff, tri, cap):
    b, e, n = aff.shape
    rows = b * e
    rr = min(ROUTE_ROWS, rows)
    idx, taff = pl.pallas_call(
        functools.partial(_route_body, cap=cap),
        grid=(rows // rr,),
        in_specs=[pl.BlockSpec((rr, n), lambda i: (i, 0)),
                  pl.BlockSpec((n, n), lambda i: (0, 0))],
        out_specs=[pl.BlockSpec((rr, cap), lambda i: (i, 0)),
                   pl.BlockSpec((rr, cap), lambda i: (i, 0))],
        out_shape=[jax.ShapeDtypeStruct((rows, cap), jnp.int32),
                   jax.ShapeDtypeStruct((rows, cap), jnp.float32)],
        scratch_shapes=[pltpu.VMEM((rr, n), jnp.float32),
                        pltpu.VMEM((3, rr, n), jnp.float32)],
        compiler_params=_cparams(1),
        name="routing",
    )(aff.reshape(rows, n), tri)
    return idx.reshape(b, e, cap), taff.reshape(b, e, cap)


EXP_PER_STEP = 2
MOE_CHUNK = 1024
SCATTER_GROUP = 16


def _moe_body(idx_ref, taff_ref, x_ref, mod_ref, n2_ref, fg_ref, wg_ref, wu_ref, wd_ref, o_ref,
              h_ref, acc_ref, *bufs, nch, cap, final):
    j = pl.program_id(1)
    d = D_MODEL
    nsl = d // LANES
    chunk = x_ref.shape[0]
    rows = chunk * nsl
    nes = N_EXPERTS // EXP_PER_STEP
    xg_refs = bufs[:EXP_PER_STEP]
    yb_refs = bufs[EXP_PER_STEP:]

    @pl.when(j < nch)
    def _():
        sh2 = mod_ref[:, 3 * d:4 * d]
        sc2 = mod_ref[:, 4 * d:5 * d]
        h2 = _rms(x_ref[...], n2_ref[...]) * (1.0 + sc2) + sh2
        base = pl.multiple_of(j * rows, rows)
        for s in range(nsl):
            h_ref[pl.ds(base + s, chunk, stride=nsl), :] = h2[:, s * LANES:(s + 1) * LANES]
        acc_ref[pl.ds(base, rows), :] = jnp.zeros((rows, LANES), jnp.float32)

    @pl.when((j >= nch) & (j < nch + nes))
    def _():
        tcol = jnp.transpose(taff_ref[...])
        for k in range(EXP_PER_STEP):
            xg_ref, yb_ref = xg_refs[k], yb_refs[k]
            tok = [pl.multiple_of(idx_ref[0, k * cap + jj], nsl) for jj in range(cap)]
            for jj in range(cap):
                xg_ref[jj * nsl:(jj + 1) * nsl, :] = h_ref[pl.ds(tok[jj], nsl), :]
            xs = jnp.concatenate([xg_ref[pl.ds(s, cap, stride=nsl), :] for s in range(nsl)], axis=-1)
            xs = xs.astype(jnp.bfloat16)
            gate = _dot(xs, wg_ref[k])
            up = _dot(xs, wu_ref[k])
            hid = (_silu(gate) * up).astype(jnp.bfloat16)
            y = _dot(hid, wd_ref[k]) * tcol[k * cap:(k + 1) * cap, 0:1]
            for s in range(nsl):
                yb_ref[pl.ds(s, cap, stride=nsl), :] = y[:, s * LANES:(s + 1) * LANES]
            for g0 in range(0, cap, SCATTER_GROUP):
                grp = range(g0, g0 + SCATTER_GROUP)
                vals = [acc_ref[pl.ds(tok[jj], nsl), :] + yb_ref[jj * nsl:(jj + 1) * nsl, :] for jj in grp]
                for jj, v in zip(grp, vals):
                    acc_ref[pl.ds(tok[jj], nsl), :] = v

    @pl.when(j >= nch + nes)
    def _():
        c = j - nch - nes
        base = pl.multiple_of(c * rows, rows)
        m = jnp.concatenate([acc_ref[pl.ds(base + s, chunk, stride=nsl), :] for s in range(nsl)], axis=-1)
        g2 = mod_ref[:, 5 * d:6 * d]
        x2 = x_ref[...] + g2 * m
        if final:
            x2 = _rms(x2, fg_ref[...])
        o_ref[...] = x2


def _experts(x1, idx, taff, mod_l, mod_row0, lw, final_g, final):
    nb, n_tok, d = x1.shape
    cap = idx.shape[2]
    chunk = MOE_CHUNK if n_tok % MOE_CHUNK == 0 else MOE_CHUNK // 2
    assert n_tok % chunk == 0 and cap % SCATTER_GROUP == 0
    nch = n_tok // chunk
    nes = N_EXPERTS // EXP_PER_STEP
    steps = 2 * nch + nes
    nsl = d // LANES
    idx = (idx * nsl).reshape(nb, nes, 1, EXP_PER_STEP * cap)
    taff = jnp.broadcast_to(taff.reshape(nb, nes, 1, EXP_PER_STEP * cap), (nb, nes, SUBLANES, EXP_PER_STEP * cap))

    def chunk_in(i, j):
        c = jnp.where(j < nch, j, jnp.where(j < nch + nes, nch - 1, j - nch - nes))
        return (i, c, 0)

    def chunk_out(i, j):
        return (i, jnp.clip(j - nch - nes, 0, nch - 1), 0)

    def slots(i, j):
        return (i, jnp.clip(j - nch, 0, nes - 1), 0, 0)

    first = lw["expert0"] // EXP_PER_STEP

    def expert(i, j):
        return (first + jnp.clip(j - nch, 0, nes - 1), 0, 0)

    const = lambda i, j: (0, 0)
    if mod_row0 is None:
        mod_map = lambda i, j: (i, 0, 0)
    else:
        mod_map = lambda i, j: (mod_row0, 0, 0)
    slot_buf = pltpu.VMEM((cap * nsl, LANES), jnp.float32)
    return pl.pallas_call(
        functools.partial(_moe_body, nch=nch, cap=cap, final=final),
        grid=(nb, steps),
        in_specs=[
            pl.BlockSpec((None, None, 1, EXP_PER_STEP * cap), slots, memory_space=pltpu.SMEM),
            pl.BlockSpec((None, None, SUBLANES, EXP_PER_STEP * cap), slots),
            pl.BlockSpec((None, chunk, d), chunk_in),
            pl.BlockSpec((None, 1, N_MOD * d), mod_map),
            pl.BlockSpec((1, d), const),
            pl.BlockSpec((1, d), const),
            pl.BlockSpec((EXP_PER_STEP, d, EXPERT_FF), expert),
            pl.BlockSpec((EXP_PER_STEP, d, EXPERT_FF), expert),
            pl.BlockSpec((EXP_PER_STEP, EXPERT_FF, d), expert),
        ],
        out_specs=pl.BlockSpec((None, chunk, d), chunk_out),
        out_shape=jax.ShapeDtypeStruct((nb, n_tok, d), jnp.float32),
        scratch_shapes=[pltpu.VMEM((n_tok * nsl, LANES), jnp.float32),
                        pltpu.VMEM((n_tok * nsl, LANES), jnp.float32)] + [slot_buf] * (2 * EXP_PER_STEP),
        compiler_params=_cparams(2),
        name="experts",
    )(idx, taff, x1, mod_l, lw["n2"], final_g, lw["wg"], lw["wu"], lw["wd"])


def _group_samples(x1, idx, taff):
    b, n_tok, d = x1.shape
    cap = idx.shape[2]
    g = max(k for k in range(1, 9) if b % k == 0 and (k * n_tok) % (MOE_CHUNK // 2) == 0)
    off = (jnp.arange(b, dtype=jnp.int32) % g) * n_tok
    idx = idx + off[:, None, None]

    def merge(t):
        t = t.reshape(b // g, g, N_EXPERTS, cap)
        return jnp.swapaxes(t, 1, 2).reshape(b // g, N_EXPERTS, g * cap)

    return x1.reshape(b // g, g * n_tok, d), merge(idx), merge(taff)


def _swap_halves(w):
    half = w.shape[-1] // 2
    return jnp.concatenate([-w[..., half:], w[..., :half]], axis=-1)


def _swa_perm():
    cols = []
    for c in range(SWA_REP):
        for g in range(SWA_KV_HEADS):
            h = g * SWA_REP + c
            cols.extend(range(h * SWA_HEAD_DIM, (h + 1) * SWA_HEAD_DIM))
    return np.asarray(cols, np.int32)


def _layer_weights(w_in, q_g, w_uq, kv_g, w_ukv, conv_w, w_o, n1, n2, router_w):
    bf = jnp.bfloat16
    d = w_in.shape[0]
    offs = np.cumsum([0, MLA_Q_RANK, MLA_KV_RANK, MLA_ROPE, SWA_HEADS * SWA_HEAD_DIM,
                      SWA_KV_HEADS * SWA_HEAD_DIM, SWA_KV_HEADS * SWA_HEAD_DIM, CONV_CH, CONV_CH, CONV_CH])
    part = [w_in[:, offs[i]:offs[i + 1]] for i in range(9)]
    w_cq, w_ckv, w_kr, w_sq, w_sk, w_sv, w_cb, w_cc, w_cu = part
    perm = _swa_perm()

    kr_block = jnp.concatenate([w_kr, _swap_halves(w_kr), jnp.zeros((d, LANES - 2 * MLA_ROPE), w_in.dtype)], axis=1)
    win = jnp.concatenate([
        w_cq, w_ckv, kr_block,
        w_sq[:, perm], w_sk, w_sv, w_cb, w_cc, w_cu], axis=1).astype(bf)

    qd = MLA_NOPE + MLA_ROPE
    uq = w_uq.reshape(MLA_Q_RANK, MLA_HEADS, qd)
    zpad = jnp.zeros((MLA_Q_RANK, MLA_HEADS, HEAD_PAD - qd), w_uq.dtype)
    main = jnp.concatenate([uq, zpad], axis=-1)
    wuq = main.reshape(MLA_Q_RANK, -1).astype(bf)

    ukv = w_ukv.reshape(MLA_KV_RANK, MLA_HEADS, MLA_NOPE + MLA_V)
    wk = jnp.concatenate([ukv[..., :MLA_NOPE],
                          jnp.zeros((MLA_KV_RANK, MLA_HEADS, HEAD_PAD - MLA_NOPE), w_ukv.dtype)], axis=-1)
    wk = wk.reshape(MLA_KV_RANK, -1).astype(bf)
    wv = ukv[..., MLA_NOPE:].reshape(MLA_KV_RANK, -1).astype(bf)

    e = np.zeros((LANES, MLA_HEADS * HEAD_PAD), np.float32)
    for h in range(MLA_HEADS):
        for l in range(MLA_ROPE):
            e[l, h * HEAD_PAD + MLA_NOPE + l] = 1.0
            e[MLA_ROPE + l, h * HEAD_PAD + MLA_NOPE + l] = 1.0

    n_mla = MLA_HEADS * MLA_V
    n_swa = SWA_HEADS * SWA_HEAD_DIM
    wo = jnp.concatenate([w_o[:n_mla], w_o[n_mla:n_mla + n_swa][perm], w_o[n_mla + n_swa:]], axis=0).astype(bf)
    return dict(
        g1=n1.reshape(1, -1), win=win, qg=q_g.reshape(1, -1), wuq=wuq, kvg=kv_g.reshape(1, -1),
        wk=wk, wvt=wv.T, e=jnp.asarray(e, bf), cw=conv_w, wo=wo, n2=n2.reshape(1, -1),
        rw=router_w.T.astype(bf))


def _axial_tables(n_tokens, rot_dim):
    rows = n_tokens // GRID_W
    row = jnp.repeat(jnp.arange(rows, dtype=jnp.float32), GRID_W)
    col = jnp.tile(jnp.arange(GRID_W, dtype=jnp.float32), rows)
    n_freq = rot_dim // 4
    inv = ROPE_BASE ** (-jnp.arange(n_freq, dtype=jnp.float32) / n_freq)
    ang = jnp.concatenate([row[:, None] * inv, col[:, None] * inv], axis=-1)
    return jnp.cos(ang), jnp.sin(ang)


def _row_table(n_ctx, n_lat):
    def with_ctx(cos, sin):
        one = jnp.ones((n_ctx, cos.shape[1]), jnp.float32)
        return (jnp.concatenate([cos, one], axis=0), jnp.concatenate([sin, 0.0 * one], axis=0))

    cm, sm = with_ctx(*_axial_tables(n_lat, MLA_ROPE))
    cs, ss = with_ctx(*_axial_tables(n_lat, SWA_HEAD_DIM))
    t = n_ctx + n_lat
    cm2 = jnp.concatenate([cm, cm], axis=1)
    sm2 = jnp.concatenate([sm, sm], axis=1)
    cs2 = jnp.concatenate([cs, cs], axis=1)
    ss2 = jnp.concatenate([-ss, ss], axis=1)
    pad = jnp.zeros((t, HEAD_PAD - MLA_NOPE - MLA_ROPE), jnp.float32)
    cq = jnp.concatenate([jnp.ones((t, MLA_NOPE), jnp.float32), cm2, pad], axis=1) * (MLA_SCALE * LOG2E)
    sq = jnp.concatenate([jnp.zeros((t, MLA_NOPE), jnp.float32), -sm, sm, pad], axis=1) * (MLA_SCALE * LOG2E)
    kr = jnp.concatenate([cm2, sm2, jnp.zeros((t, LANES - 2 * MLA_ROPE), jnp.float32)], axis=1)
    tab = jnp.concatenate([
        jnp.tile(cq, (1, MLA_HEADS)), jnp.tile(sq, (1, MLA_HEADS)), kr,
        jnp.tile(cs2, (1, SWA_HEADS)) * (SWA_SCALE * LOG2E), jnp.tile(ss2, (1, SWA_HEADS)) * (SWA_SCALE * LOG2E),
        jnp.tile(cs2, (1, SWA_KV_HEADS)), jnp.tile(ss2, (1, SWA_KV_HEADS))], axis=1)
    return tab


def _strict_upper(n):
    r = lax.broadcasted_iota(jnp.int32, (n, n), 0)
    c = lax.broadcasted_iota(jnp.int32, (n, n), 1)
    return (r < c).astype(jnp.bfloat16)


def kernel(x, c, ctx, c_ctx, ada_w, ada_b, norm1_g, w_in, mla_q_norm_g, mla_w_uq, mla_kv_norm_g, mla_w_ukv,
           swa_sink, conv_w, w_o, norm2_g, router_w, exp_w_gate, exp_w_up, exp_w_down, final_norm_g):
    b, s, d = x.shape
    n_ctx = ctx.shape[1]
    depth = ada_w.shape[0]
    assert d == D_MODEL and s % MLA_TQ == 0 and n_ctx == TILE and b + 1 <= MOD_ROWS
    cap_lat = EC_CAPACITY_FACTOR * s // N_EXPERTS
    cap_ctx = EC_CAPACITY_FACTOR * n_ctx // N_EXPERTS

    cc = jnp.concatenate([c, c_ctx[None, :], jnp.zeros((MOD_ROWS - b - 1, d), c.dtype)], axis=0)
    mod = _modulation(cc, ada_w, ada_b).reshape(depth, MOD_ROWS, 1, N_MOD * d)
    tab = _row_table(n_ctx, s)
    tri_lat = _strict_upper(s)
    tri_ctx = _strict_upper(n_ctx)
    swa_bias = _swa_bias(n_ctx, s)
    sink_slots = jnp.zeros((8,), jnp.float32)
    fg = final_norm_g.reshape(1, d)
    wg_all = exp_w_gate.astype(jnp.bfloat16).reshape((depth * N_EXPERTS,) + exp_w_gate.shape[2:])
    wu_all = exp_w_up.astype(jnp.bfloat16).reshape((depth * N_EXPERTS,) + exp_w_up.shape[2:])
    wd_all = exp_w_down.astype(jnp.bfloat16).reshape((depth * N_EXPERTS,) + exp_w_down.shape[2:])

    xl, xc = x, ctx
    for li in range(depth):
        last = li == depth - 1
        lw = _layer_weights(w_in[li], mla_q_norm_g[li], mla_w_uq[li], mla_kv_norm_g[li], mla_w_ukv[li],
                            conv_w[li], w_o[li], norm1_g[li], norm2_g[li], router_w[li])
        lw.update(wg=wg_all, wu=wu_all, wd=wd_all, expert0=li * N_EXPERTS)
        mod_l = mod[li]
        sink = sink_slots.at[:SWA_HEADS].set(swa_sink[li] * LOG2E)
        qm, km, vmt, qs, ks, vs, cz = _projections(xl, xc, mod_l, lw, tab)
        a = _mla_attention(qm, km, vmt, s)
        bsw = _swa_attention(sink, qs, ks, vs, swa_bias, s)
        x1, aff = _mixer_out(a, bsw, cz, xl, mod_l, None, lw, 0)
        idx, taff = _routing(aff, tri_lat, cap_lat)
        if not last:
            ac = _mla_attention_ctx(qm, km, vmt, s)
            bc = _swa_attention_ctx(sink, qs, ks, vs, s)
            xc1, affc = _mixer_out(ac, bc, cz, xc, mod_l, b, lw, s)
            idxc, taffc = _routing(affc, tri_ctx, cap_ctx)
            xc = _experts(*_group_samples(xc1, idxc, taffc), mod_l, b, lw, fg, final=False).reshape(xc1.shape)
        xl = _experts(x1, idx, taff, mod_l, None, lw, fg, final=last)
    return xl
```

```python
import functools

import jax
import jax.numpy as jnp
import numpy as np
from jax import lax
from jax.experimental import pallas as pl
from jax.experimental.pallas import tpu as pltpu

D_MODEL = 1024
GRID_W = 64
NORM_EPS = 1e-6
ROPE_BASE = 10000.0

MLA_HEADS = 6
MLA_Q_RANK = 256
MLA_KV_RANK = 128
MLA_NOPE = 64
MLA_ROPE = 32
MLA_V = 64
MLA_SCALE = (MLA_NOPE + MLA_ROPE) ** -0.5

SWA_HEADS = 6
SWA_KV_HEADS = 2
SWA_REP = SWA_HEADS // SWA_KV_HEADS
SWA_HEAD_DIM = 64
SWA_WINDOW = 128
SWA_SCALE = SWA_HEAD_DIM ** -0.5

CONV_CH = 256
N_EXPERTS = 16
EXPERT_FF = 512
EC_CAPACITY_FACTOR = 2
N_MOD = 6

LANES = 128
SUBLANES = 8
TILE = 256
HEAD_PAD = 128
MOD_ROWS = 40
NEG_BIG = -1e30
LOG2E = 1.4426950408889634

_C_CQ = 0
_C_CKV = 256
_C_KR = 384
_C_SQ = 512
_C_SK = 896
_C_SV = 1024
_C_CB = 1152
_C_CC = 1408
_C_CU = 1664
IN_W = 1920

_T_CQ = 0
_T_SQ = 768
_T_KR = 1536
_T_CS = 1664
_T_SS = 2048
_T_CK = 2432
_T_SK = 2560
TAB_W = 2688

_VMEM_LIMIT = 56 * 1024 * 1024


def _cparams(n_grid):
    return pltpu.CompilerParams(dimension_semantics=("arbitrary",) * n_grid,
                                vmem_limit_bytes=_VMEM_LIMIT)


def _silu(v):
    return v * (1.0 / (1.0 + jnp.exp(-v)))


def _rms(v, g):
    return v * lax.rsqrt(jnp.mean(v * v, axis=-1, keepdims=True) + NORM_EPS) * g


def _dot(a, b):
    return jnp.dot(a, b, preferred_element_type=jnp.float32)


def _dot_nt(a, b):
    return lax.dot_general(a, b, (((1,), (1,)), ((), ())), preferred_element_type=jnp.float32)


def _mod_body(c_ref, w_ref, b_ref, o_ref):
    a = _silu(c_ref[...]).astype(jnp.bfloat16)
    o_ref[...] = _dot(a, w_ref[...].astype(jnp.bfloat16)) + b_ref[...]


def _modulation(cc, ada_w, ada_b):
    depth, d, n = ada_w.shape
    bn = 512
    return pl.pallas_call(
        _mod_body,
        grid=(depth, n // bn),
        in_specs=[
            pl.BlockSpec((MOD_ROWS, d), lambda l, i: (0, 0)),
            pl.BlockSpec((None, d, bn), lambda l, i: (l, 0, i)),
            pl.BlockSpec((None, 1, bn), lambda l, i: (l, 0, i)),
        ],
        out_specs=pl.BlockSpec((None, MOD_ROWS, bn), lambda l, i: (l, 0, i)),
        out_shape=jax.ShapeDtypeStruct((depth, MOD_ROWS, n), jnp.float32),
        compiler_params=_cparams(2),
        name="modulation",
    )(cc, ada_w, ada_b.reshape(depth, 1, n))


def _proj_body(x_ref, ctx_ref, mod_ref, g1_ref, win_ref, qg_ref, wuq_ref, kvg_ref, wk_ref, wvt_ref,
               tab_ref, qm_ref, km_ref, vmt_ref, qs_ref, ks_ref, vs_ref, cz_ref, xs_ref):
    j = pl.program_id(0)
    is_ctx = j == pl.num_programs(0) - 1

    @pl.when(is_ctx)
    def _():
        xs_ref[...] = ctx_ref[...]

    @pl.when(jnp.logical_not(is_ctx))
    def _():
        xs_ref[...] = x_ref[...]

    d = D_MODEL
    sh1 = mod_ref[:, 0:d]
    sc1 = mod_ref[:, d:2 * d]
    h = (_rms(xs_ref[...], g1_ref[...]) * (1.0 + sc1) + sh1).astype(jnp.bfloat16)

    def proj(lo, hi):
        return _dot(h, win_ref[:, lo:hi])

    def tab(lo, n):
        return tab_ref[:, lo:lo + n]

    cq = _rms(proj(_C_CQ, _C_CKV), qg_ref[...]).astype(jnp.bfloat16)
    uq = _dot(cq, wuq_ref[...])
    lane = lax.broadcasted_iota(jnp.int32, (h.shape[0], LANES), 1)
    rope_lo = (lane & (MLA_ROPE // 2)) == 0
    for c in range(MLA_HEADS):
        x = uq[:, c * HEAD_PAD:(c + 1) * HEAD_PAD]
        swapped = jnp.where(rope_lo, pltpu.roll(x, HEAD_PAD - MLA_ROPE // 2, 1), pltpu.roll(x, MLA_ROPE // 2, 1))
        qm_ref[:, c * HEAD_PAD:(c + 1) * HEAD_PAD] = (
            x * tab(_T_CQ + c * HEAD_PAD, HEAD_PAD) + swapped * tab(_T_SQ + c * HEAD_PAD, HEAD_PAD)
        ).astype(jnp.bfloat16)

    ckv = _rms(proj(_C_CKV, _C_KR), kvg_ref[...]).astype(jnp.bfloat16)
    krp = proj(_C_KR, _C_SQ) * tab(_T_KR, LANES)
    in_rope = (lane >= MLA_NOPE) & (lane < MLA_NOPE + MLA_ROPE)
    kr = jnp.where(in_rope, pltpu.roll(krp, MLA_NOPE, 1) + pltpu.roll(krp, MLA_NOPE - MLA_ROPE, 1), 0.0)
    kn = _dot(ckv, wk_ref[...])
    for c in range(MLA_HEADS):
        km_ref[:, c * HEAD_PAD:(c + 1) * HEAD_PAD] = (kn[:, c * HEAD_PAD:(c + 1) * HEAD_PAD] + kr).astype(jnp.bfloat16)
    vmt_ref[...] = _dot_nt(wvt_ref[...], ckv).astype(jnp.bfloat16)

    nsq = SWA_HEADS * SWA_HEAD_DIM
    first_half = (lane & (SWA_HEAD_DIM // 2)) == 0

    def rotary(x, c_off, s_off):
        half = SWA_HEAD_DIM // 2
        swapped = jnp.where(first_half, pltpu.roll(x, LANES - half, 1), pltpu.roll(x, half, 1))
        return (x * tab(c_off, LANES) + swapped * tab(s_off, LANES)).astype(jnp.bfloat16)

    uq = proj(_C_SQ, _C_SK)
    for c in range(nsq // LANES):
        qs_ref[:, c * LANES:(c + 1) * LANES] = rotary(uq[:, c * LANES:(c + 1) * LANES],
                                                       _T_CS + c * LANES, _T_SS + c * LANES)
    ks_ref[...] = rotary(proj(_C_SK, _C_SV), _T_CK, _T_SK)
    vs_ref[...] = proj(_C_SV, _C_CB).astype(jnp.bfloat16)

    cz_ref[:, 0:CONV_CH] = proj(_C_CB, _C_CC).astype(jnp.bfloat16)
    cz_ref[:, CONV_CH:2 * CONV_CH] = (proj(_C_CC, _C_CU) * proj(_C_CU, IN_W)).astype(jnp.bfloat16)


def _projections(x, ctx, mod_l, lw, tab):
    b, s, d = x.shape
    n_ctx = ctx.shape[1]
    t = n_ctx + s
    nl = s // TILE
    const = lambda j, i: (0, 0)
    row_outs = [MLA_HEADS * HEAD_PAD, MLA_HEADS * HEAD_PAD, None, SWA_HEADS * SWA_HEAD_DIM, LANES, LANES,
                2 * CONV_CH]
    nv = MLA_HEADS * MLA_V
    out_specs = [pl.BlockSpec((None, nv, TILE), lambda j, i: (i, 0, j)) if w is None
                 else pl.BlockSpec((None, TILE, w), lambda j, i: (i, j, 0)) for w in row_outs]
    out_shape = [jax.ShapeDtypeStruct((b, nv, t) if w is None else (b, t, w), jnp.bfloat16) for w in row_outs]
    return pl.pallas_call(
        _proj_body,
        grid=(nl + 1, b),
        in_specs=[
            pl.BlockSpec((None, TILE, d), lambda j, i: (jnp.where(j == nl, 0, i), jnp.minimum(j, nl - 1), 0)),
            pl.BlockSpec((None, TILE, d), lambda j, i: (jnp.where(j == nl, i, 0), 0, 0)),
            pl.BlockSpec((None, 1, N_MOD * d), lambda j, i: (jnp.where(j == nl, b, i), 0, 0)),
            pl.BlockSpec((1, d), const),
            pl.BlockSpec((d, IN_W), const),
            pl.BlockSpec((1, MLA_Q_RANK), const),
            pl.BlockSpec((MLA_Q_RANK, MLA_HEADS * HEAD_PAD), const),
            pl.BlockSpec((1, MLA_KV_RANK), const),
            pl.BlockSpec((MLA_KV_RANK, MLA_HEADS * HEAD_PAD), const),
            pl.BlockSpec((nv, MLA_KV_RANK), const),
            pl.BlockSpec((TILE, TAB_W), lambda j, i: (j, 0)),
        ],
        out_specs=out_specs,
        out_shape=out_shape,
        scratch_shapes=[pltpu.VMEM((TILE, d), jnp.float32)],
        compiler_params=_cparams(2),
        name="projections",
    )(x, ctx, mod_l, lw["g1"], lw["win"], lw["qg"], lw["wuq"], lw["kvg"], lw["wk"], lw["wvt"], tab)


MLA_TQ = 2048


def _mla_pair(q_ref, k_ref, vt_ref, o_ref, c):
    tq = q_ref.shape[0]
    res = []
    for hh in range(2):
        lo = (2 * c + hh) * HEAD_PAD
        st = _dot_nt(k_ref[:, lo:lo + HEAD_PAD], q_ref[:, lo:lo + HEAD_PAD])
        m = jnp.max(st, axis=0, keepdims=True)
        p = jnp.exp2(st - m)
        l = jnp.sum(p, axis=0, keepdims=True)
        ot = _dot(vt_ref[c * LANES:(c + 1) * LANES, :], p.astype(jnp.bfloat16))
        res.append(ot / l)
    row = lax.broadcasted_iota(jnp.int32, (LANES, tq), 0)
    o_ref[:, c * LANES:(c + 1) * LANES] = jnp.transpose(
        jnp.where(row < MLA_V, res[0], res[1])).astype(jnp.bfloat16)


def _mla_main_body(q_ref, k_ref, vt_ref, o_ref):
    for c in range(MLA_HEADS // 2):
        _mla_pair(q_ref, k_ref, vt_ref, o_ref, c)


def _mla_ctx_body(q_ref, k_ref, vt_ref, o_ref):
    for c in range(MLA_HEADS // 2):
        _mla_pair(q_ref, k_ref, vt_ref, o_ref, c)


def _mla_attention(qm, km, vmt, n_lat):
    b, t, _ = qm.shape
    return pl.pallas_call(
        _mla_main_body,
        grid=(b, n_lat // MLA_TQ),
        in_specs=[
            pl.BlockSpec((None, MLA_TQ, qm.shape[2]), lambda i, j: (i, j, 0)),
            pl.BlockSpec((None, t, km.shape[2]), lambda i, j: (i, 0, 0)),
            pl.BlockSpec((None, vmt.shape[1], t), lambda i, j: (i, 0, 0)),
        ],
        out_specs=pl.BlockSpec((None, MLA_TQ, vmt.shape[1]), lambda i, j: (i, j, 0)),
        out_shape=jax.ShapeDtypeStruct((b, n_lat, vmt.shape[1]), jnp.bfloat16),
        compiler_params=_cparams(2),
        name="mla_attention",
    )(qm, km, vmt)


def _mla_attention_ctx(qm, km, vmt, n_lat):
    b, t, _ = qm.shape
    n_ctx = t - n_lat
    blk = n_lat // n_ctx
    return pl.pallas_call(
        _mla_ctx_body,
        grid=(b,),
        in_specs=[
            pl.BlockSpec((None, n_ctx, qm.shape[2]), lambda i: (i, blk, 0)),
            pl.BlockSpec((None, n_ctx, km.shape[2]), lambda i: (i, blk, 0)),
            pl.BlockSpec((None, vmt.shape[1], n_ctx), lambda i: (i, 0, blk)),
        ],
        out_specs=pl.BlockSpec((None, n_ctx, vmt.shape[1]), lambda i: (i, 0, 0)),
        out_shape=jax.ShapeDtypeStruct((b, n_ctx, vmt.shape[1]), jnp.bfloat16),
        compiler_params=_cparams(1),
        name="mla_attention_ctx",
    )(qm, km, vmt)


def _swa_attend(sink_ref, q_ref, kcat, vcat, valid, o_ref):
    rows = q_ref.shape[0]
    lane = lax.broadcasted_iota(jnp.int32, (rows, LANES), 1)
    lo_half = lane < SWA_HEAD_DIM
    for c in range(SWA_REP):
        q2 = q_ref[:, c * LANES:(c + 1) * LANES]
        res = []
        for g in range(SWA_KV_HEADS):
            keep = lo_half if g == 0 else jnp.logical_not(lo_half)
            qg = jnp.where(keep, q2, jnp.zeros_like(q2))
            s = _dot_nt(qg, kcat)
            if valid is not None:
                s = jnp.where(valid, s, NEG_BIG)
            sk = sink_ref[g * SWA_REP + c]
            m = jnp.maximum(jnp.max(s, axis=-1, keepdims=True), sk)
            p = jnp.exp2(s - m)
            l = jnp.sum(p, axis=-1, keepdims=True) + jnp.exp2(sk - m)
            res.append(_dot(p.astype(jnp.bfloat16), vcat) / l)
        o_ref[:, c * LANES:(c + 1) * LANES] = jnp.where(lo_half, res[0], res[1]).astype(jnp.bfloat16)


SWA_BAND = TILE + 2 * SWA_WINDOW


def _swa_band_start(j, n_lat):
    return jnp.clip(j * TILE - SWA_WINDOW, 0, n_lat - SWA_BAND)


def _swa_main_body(sink_ref, q_ref, k_ref, v_ref, bias_ref, o_ref, *, n_ctx, n_lat):
    ks = pl.multiple_of(_swa_band_start(pl.program_id(1), n_lat), SWA_WINDOW)
    kcat = jnp.concatenate([k_ref[n_lat:n_lat + n_ctx, :], k_ref[pl.ds(ks, SWA_BAND), :]], axis=0)
    vcat = jnp.concatenate([v_ref[n_lat:n_lat + n_ctx, :], v_ref[pl.ds(ks, SWA_BAND), :]], axis=0)
    vt = jnp.transpose(vcat.astype(jnp.float32)).astype(jnp.bfloat16)
    bias = jnp.concatenate([bias_ref[...]] * SWA_REP, axis=1)
    lane = lax.broadcasted_iota(jnp.int32, (TILE, LANES), 1)
    lo_half = lane < SWA_HEAD_DIM
    qcol = lax.broadcasted_iota(jnp.int32, (1, SWA_REP * TILE), 1)
    res = []
    for g in range(SWA_KV_HEADS):
        keep = lo_half if g == 0 else jnp.logical_not(lo_half)
        qg = jnp.concatenate([jnp.where(keep, q_ref[:, c * LANES:(c + 1) * LANES], 0.0).astype(jnp.bfloat16)
                              for c in range(SWA_REP)], axis=0)
        sk = jnp.full((1, SWA_REP * TILE), sink_ref[g * SWA_REP + SWA_REP - 1], jnp.float32)
        for c in range(SWA_REP - 2, -1, -1):
            sk = jnp.where(qcol < (c + 1) * TILE, sink_ref[g * SWA_REP + c], sk)
        st = _dot_nt(kcat, qg) + bias
        m = jnp.maximum(jnp.max(st, axis=0, keepdims=True), sk)
        p = jnp.exp2(st - m)
        l = jnp.sum(p, axis=0, keepdims=True) + jnp.exp2(sk - m)
        res.append(_dot(vt, p.astype(jnp.bfloat16)) / l)
    row = lax.broadcasted_iota(jnp.int32, (LANES, TILE), 0)
    for c in range(SWA_REP):
        pair = jnp.where(row < SWA_HEAD_DIM, res[0][:, c * TILE:(c + 1) * TILE], res[1][:, c * TILE:(c + 1) * TILE])
        o_ref[:, c * LANES:(c + 1) * LANES] = jnp.transpose(pair).astype(jnp.bfloat16)


def _swa_bias(n_ctx, n_lat):
    nt = n_lat // TILE
    r = lax.broadcasted_iota(jnp.int32, (n_ctx + SWA_BAND, TILE), 0)
    q = lax.broadcasted_iota(jnp.int32, (n_ctx + SWA_BAND, TILE), 1)
    out = []
    for j in (0, 1, nt - 1):
        kpos = _swa_band_start(j, n_lat) + r - n_ctx
        valid = (r < n_ctx) | (jnp.abs(j * TILE + q - kpos) <= SWA_WINDOW)
        out.append(jnp.where(valid, 0.0, NEG_BIG).astype(jnp.float32))
    return jnp.stack(out)


def _swa_ctx_body(sink_ref, q_ref, k_ref, v_ref, o_ref):
    _swa_attend(sink_ref, q_ref, k_ref[...], v_ref[...], None, o_ref)


def _swa_attention(sink, qs, ks, vs, bias, n_lat):
    b, t, w = qs.shape
    nt = n_lat // TILE
    assert nt >= 3
    return pl.pallas_call(
        functools.partial(_swa_main_body, n_ctx=t - n_lat, n_lat=n_lat),
        grid=(b, nt),
        in_specs=[
            pl.BlockSpec(memory_space=pltpu.SMEM),
            pl.BlockSpec((None, TILE, w), lambda i, j: (i, j, 0)),
            pl.BlockSpec((None, t, LANES), lambda i, j: (i, 0, 0)),
            pl.BlockSpec((None, t, LANES), lambda i, j: (i, 0, 0)),
            pl.BlockSpec((None,) + bias.shape[1:], lambda i, j: (jnp.where(j == 0, 0, jnp.where(j == nt - 1, 2, 1)), 0, 0)),
        ],
        out_specs=pl.BlockSpec((None, TILE, w), lambda i, j: (i, j, 0)),
        out_shape=jax.ShapeDtypeStruct((b, n_lat, w), jnp.bfloat16),
        compiler_params=_cparams(2),
        name="swa_attention",
    )(sink, qs, ks, vs, bias)


def _swa_attention_ctx(sink, qs, ks, vs, n_lat):
    b, t, w = qs.shape
    n_ctx = t - n_lat
    blk = n_lat // n_ctx
    return pl.pallas_call(
        _swa_ctx_body,
        grid=(b,),
        in_specs=[
            pl.BlockSpec(memory_space=pltpu.SMEM),
            pl.BlockSpec((None, n_ctx, w), lambda i: (i, blk, 0)),
            pl.BlockSpec((None, n_ctx, LANES), lambda i: (i, blk, 0)),
            pl.BlockSpec((None, n_ctx, LANES), lambda i: (i, blk, 0)),
        ],
        out_specs=pl.BlockSpec((None, n_ctx, w), lambda i: (i, 0, 0)),
        out_shape=jax.ShapeDtypeStruct((b, n_ctx, w), jnp.bfloat16),
        compiler_params=_cparams(1),
        name="swa_attention_ctx",
    )(sink, qs, ks, vs)


HALO = 16


def _mix_body(a_ref, b_ref, cz_ref, hp_ref, hn_ref, x_ref, mod_ref, cw_ref, wo_ref, n2_ref, rw_ref,
              x1_ref, aff_ref):
    j = pl.program_id(1)
    d = D_MODEL
    first = j == 0
    last = j == pl.num_programs(1) - 1
    z = cz_ref[:, CONV_CH:2 * CONV_CH].astype(jnp.float32)
    zp = hp_ref[HALO - 1:HALO, CONV_CH:2 * CONV_CH].astype(jnp.float32)
    zn = hn_ref[0:1, CONV_CH:2 * CONV_CH].astype(jnp.float32)
    zp = jnp.where(first, jnp.zeros_like(zp), zp)
    zn = jnp.where(last, jnp.zeros_like(zn), zn)
    row = lax.broadcasted_iota(jnp.int32, z.shape, 0)
    z_dn = jnp.where(row == 0, zp, pltpu.roll(z, 1, 0))
    rows = z.shape[0]
    z_up = jnp.where(row == rows - 1, zn, pltpu.roll(z, rows - 1, 0))
    y = z_dn * cw_ref[0:1, :] + z * cw_ref[1:2, :] + z_up * cw_ref[2:3, :]
    cv = (cz_ref[:, 0:CONV_CH].astype(jnp.float32) * y).astype(jnp.bfloat16)
    mix = jnp.concatenate([a_ref[...], b_ref[...], cv], axis=-1)
    g1 = mod_ref[:, 2 * d:3 * d]
    sh2 = mod_ref[:, 3 * d:4 * d]
    sc2 = mod_ref[:, 4 * d:5 * d]
    x1 = x_ref[...] + g1 * _dot(mix, wo_ref[...])
    x1_ref[...] = x1
    h2 = (_rms(x1, n2_ref[...]) * (1.0 + sc2) + sh2).astype(jnp.bfloat16)
    lg = _dot_nt(rw_ref[...], h2)
    ex = jnp.exp(lg - jnp.max(lg, axis=0, keepdims=True))
    aff_ref[...] = ex / jnp.sum(ex, axis=0, keepdims=True)


MIX_TILE = 1024


def _mixer_out(a, bsw, cz, x, mod_l, mod_row0, lw, frame_row0):
    b, n, d = x.shape
    t = cz.shape[1]
    tm = min(MIX_TILE, n)
    assert n % tm == 0 and frame_row0 % tm == 0
    tile0 = frame_row0 // tm
    hb = tm // HALO
    here = lambda i, j: (i, j, 0)
    const = lambda i, j: (0, 0)
    if mod_row0 is None:
        mod_map = lambda i, j: (i, 0, 0)
    else:
        mod_map = lambda i, j: (mod_row0, 0, 0)
    return pl.pallas_call(
        _mix_body,
        grid=(b, n // tm),
        in_specs=[
            pl.BlockSpec((None, tm, a.shape[2]), here),
            pl.BlockSpec((None, tm, bsw.shape[2]), here),
            pl.BlockSpec((None, tm, cz.shape[2]), lambda i, j: (i, j + tile0, 0)),
            pl.BlockSpec((None, HALO, cz.shape[2]),
                         lambda i, j: (i, jnp.maximum((j + tile0) * hb - 1, 0), 0)),
            pl.BlockSpec((None, HALO, cz.shape[2]),
                         lambda i, j: (i, jnp.minimum((j + tile0 + 1) * hb, t // HALO - 1), 0)),
            pl.BlockSpec((None, tm, d), here),
            pl.BlockSpec((None, 1, N_MOD * d), mod_map),
            pl.BlockSpec((3, CONV_CH), const),
            pl.BlockSpec((d, d), const),
            pl.BlockSpec((1, d), const),
            pl.BlockSpec((N_EXPERTS, d), const),
        ],
        out_specs=[pl.BlockSpec((None, tm, d), here),
                   pl.BlockSpec((None, N_EXPERTS, tm), lambda i, j: (i, 0, j))],
        out_shape=[jax.ShapeDtypeStruct((b, n, d), jnp.float32),
                   jax.ShapeDtypeStruct((b, N_EXPERTS, n), jnp.float32)],
        compiler_params=_cparams(2),
        name="mixer_out",
    )(a, bsw, cz, cz, cz, x, mod_l, lw["cw"], lw["wo"], lw["n2"], lw["rw"])


ROUTE_ROWS = 128
ROUTE_SPLIT = 64
ROUTE_UNROLL = 4


def _route_body(aff_ref, tri_ref, idx_ref, taff_ref, pos_ref, parts_ref, *, cap):
    rows, n = aff_ref.shape
    aff = aff_ref[...]
    bits = pltpu.bitcast(aff, jnp.int32)

    def count(mask):
        return jnp.sum(jnp.where(mask, 1.0, 0.0), axis=1, keepdims=True)

    def ones(mask):
        return jnp.where(mask, 1.0, 0.0).astype(jnp.bfloat16)

    def search(i, thr):
        cand = thr | (1 << (30 - i))
        return jnp.where(count(bits >= cand) >= cap, cand, thr)

    thr = lax.fori_loop(0, 31, search, jnp.zeros((rows, 1), jnp.int32))
    gt = bits > thr
    eq = bits == thr
    need = cap - count(gt)
    peq = _dot(ones(eq), tri_ref[...])
    sel = gt | (eq & (peq < need))
    pos = _dot(ones(sel), tri_ref[...])
    pos_ref[...] = jnp.where(sel, pos, -1.0)

    a1 = aff.astype(jnp.bfloat16).astype(jnp.float32)
    a2 = (aff - a1).astype(jnp.bfloat16).astype(jnp.float32)
    parts_ref[0] = a1
    parts_ref[1] = a2
    parts_ref[2] = aff - a1 - a2
    tok = lax.broadcasted_iota(jnp.int32, (1, n), 1)
    tok_hi = (tok // ROUTE_SPLIT).astype(jnp.float32)
    tok_lo = (tok % ROUTE_SPLIT).astype(jnp.float32)
    sub = lax.broadcasted_iota(jnp.int32, (SUBLANES, n), 0)
    slot = lax.broadcasted_iota(jnp.int32, (cap, 1), 0).astype(jnp.float32)

    def per_row(r, carry):
        onehot = jnp.where(pos_ref[pl.ds(r, 1), :] == slot, 1.0, 0.0).astype(jnp.bfloat16)
        vals = jnp.where(sub == 0, tok_hi, jnp.where(sub == 1, tok_lo, 0.0))
        for k in range(3):
            vals = jnp.where(sub == 2 + k, parts_ref[k, pl.ds(r, 1), :], vals)
        got = _dot_nt(vals.astype(jnp.bfloat16), onehot)
        idx_ref[pl.ds(r, 1), :] = (got[0:1, :] * ROUTE_SPLIT + got[1:2, :]).astype(jnp.int32)
        taff_ref[pl.ds(r, 1), :] = got[2:3, :] + got[3:4, :] + got[4:5, :]
        return carry

    lax.fori_loop(0, rows, per_row, 0, unroll=ROUTE_UNROLL)


def _routing(aff, tri, cap):
    b, e, n = aff.shape
    rows = b * e
    rr = min(ROUTE_ROWS, rows)
    idx, taff = pl.pallas_call(
        functools.partial(_route_body, cap=cap),
        grid=(rows // rr,),
        in_specs=[pl.BlockSpec((rr, n), lambda i: (i, 0)),
                  pl.BlockSpec((n, n), lambda i: (0, 0))],
        out_specs=[pl.BlockSpec((rr, cap), lambda i: (i, 0)),
                   pl.BlockSpec((rr, cap), lambda i: (i, 0))],
        out_shape=[jax.ShapeDtypeStruct((rows, cap), jnp.int32),
                   jax.ShapeDtypeStruct((rows, cap), jnp.float32)],
        scratch_shapes=[pltpu.VMEM((rr, n), jnp.float32),
                        pltpu.VMEM((3, rr, n), jnp.float32)],
        compiler_params=_cparams(1),
        name="routing",
    )(aff.reshape(rows, n), tri)
    return idx.reshape(b, e, cap), taff.reshape(b, e, cap)


EXP_PER_STEP = 2
MOE_CHUNK = 1024
SCATTER_GROUP = 16


def _moe_body(idx_ref, taff_ref, x_ref, mod_ref, n2_ref, fg_ref, wg_ref, wu_ref, wd_ref, o_ref,
              h_ref, acc_ref, *bufs, nch, cap, final):
    j = pl.program_id(1)
    d = D_MODEL
    nsl = d // LANES
    chunk = x_ref.shape[0]
    rows = chunk * nsl
    nes = N_EXPERTS // EXP_PER_STEP
    xg_refs = bufs[:EXP_PER_STEP]
    yb_refs = bufs[EXP_PER_STEP:]

    @pl.when(j < nch)
    def _():
        sh2 = mod_ref[:, 3 * d:4 * d]
        sc2 = mod_ref[:, 4 * d:5 * d]
        h2 = _rms(x_ref[...], n2_ref[...]) * (1.0 + sc2) + sh2
        base = pl.multiple_of(j * rows, rows)
        for s in range(nsl):
            h_ref[pl.ds(base + s, chunk, stride=nsl), :] = h2[:, s * LANES:(s + 1) * LANES]
        acc_ref[pl.ds(base, rows), :] = jnp.zeros((rows, LANES), jnp.float32)

    @pl.when((j >= nch) & (j < nch + nes))
    def _():
        tcol = jnp.transpose(taff_ref[...])
        for k in range(EXP_PER_STEP):
            xg_ref, yb_ref = xg_refs[k], yb_refs[k]
            tok = [pl.multiple_of(idx_ref[0, k * cap + jj], nsl) for jj in range(cap)]
            for jj in range(cap):
                xg_ref[jj * nsl:(jj + 1) * nsl, :] = h_ref[pl.ds(tok[jj], nsl), :]
            xs = jnp.concatenate([xg_ref[pl.ds(s, cap, stride=nsl), :] for s in range(nsl)], axis=-1)
            xs = xs.astype(jnp.bfloat16)
            gate = _dot(xs, wg_ref[k])
            up = _dot(xs, wu_ref[k])
            hid = (_silu(gate) * up).astype(jnp.bfloat16)
            y = _dot(hid, wd_ref[k]) * tcol[k * cap:(k + 1) * cap, 0:1]
            for s in range(nsl):
                yb_ref[pl.ds(s, cap, stride=nsl), :] = y[:, s * LANES:(s + 1) * LANES]
            for g0 in range(0, cap, SCATTER_GROUP):
                grp = range(g0, g0 + SCATTER_GROUP)
                vals = [acc_ref[pl.ds(tok[jj], nsl), :] + yb_ref[jj * nsl:(jj + 1) * nsl, :] for jj in grp]
                for jj, v in zip(grp, vals):
                    acc_ref[pl.ds(tok[jj], nsl), :] = v

    @pl.when(j >= nch + nes)
    def _():
        c = j - nch - nes
        base = pl.multiple_of(c * rows, rows)
        m = jnp.concatenate([acc_ref[pl.ds(base + s, chunk, stride=nsl), :] for s in range(nsl)], axis=-1)
        g2 = mod_ref[:, 5 * d:6 * d]
        x2 = x_ref[...] + g2 * m
        if final:
            x2 = _rms(x2, fg_ref[...])
        o_ref[...] = x2


def _experts(x1, idx, taff, mod_l, mod_row0, lw, final_g, final):
    nb, n_tok, d = x1.shape
    cap = idx.shape[2]
    chunk = MOE_CHUNK if n_tok % MOE_CHUNK == 0 else MOE_CHUNK // 2
    assert n_tok % chunk == 0 and cap % SCATTER_GROUP == 0
    nch = n_tok // chunk
    nes = N_EXPERTS // EXP_PER_STEP
    steps = 2 * nch + nes
    nsl = d // LANES
    idx = (idx * nsl).reshape(nb, nes, 1, EXP_PER_STEP * cap)
    taff = jnp.broadcast_to(taff.reshape(nb, nes, 1, EXP_PER_STEP * cap), (nb, nes, SUBLANES, EXP_PER_STEP * cap))

    def chunk_in(i, j):
        c = jnp.where(j < nch, j, jnp.where(j < nch + nes, nch - 1, j - nch - nes))
        return (i, c, 0)

    def chunk_out(i, j):
        return (i, jnp.clip(j - nch - nes, 0, nch - 1), 0)

    def slots(i, j):
        return (i, jnp.clip(j - nch, 0, nes - 1), 0, 0)

    first = lw["expert0"] // EXP_PER_STEP

    def expert(i, j):
        return (first + jnp.clip(j - nch, 0, nes - 1), 0, 0)

    const = lambda i, j: (0, 0)
    if mod_row0 is None:
        mod_map = lambda i, j: (i, 0, 0)
    else:
        mod_map = lambda i, j: (mod_row0, 0, 0)
    slot_buf = pltpu.VMEM((cap * nsl, LANES), jnp.float32)
    return pl.pallas_call(
        functools.partial(_moe_body, nch=nch, cap=cap, final=final),
        grid=(nb, steps),
        in_specs=[
            pl.BlockSpec((None, None, 1, EXP_PER_STEP * cap), slots, memory_space=pltpu.SMEM),
            pl.BlockSpec((None, None, SUBLANES, EXP_PER_STEP * cap), slots),
            pl.BlockSpec((None, chunk, d), chunk_in),
            pl.BlockSpec((None, 1, N_MOD * d), mod_map),
            pl.BlockSpec((1, d), const),
            pl.BlockSpec((1, d), const),
            pl.BlockSpec((EXP_PER_STEP, d, EXPERT_FF), expert),
            pl.BlockSpec((EXP_PER_STEP, d, EXPERT_FF), expert),
            pl.BlockSpec((EXP_PER_STEP, EXPERT_FF, d), expert),
        ],
        out_specs=pl.BlockSpec((None, chunk, d), chunk_out),
        out_shape=jax.ShapeDtypeStruct((nb, n_tok, d), jnp.float32),
        scratch_shapes=[pltpu.VMEM((n_tok * nsl, LANES), jnp.float32),
                        pltpu.VMEM((n_tok * nsl, LANES), jnp.float32)] + [slot_buf] * (2 * EXP_PER_STEP),
        compiler_params=_cparams(2),
        name="experts",
    )(idx, taff, x1, mod_l, lw["n2"], final_g, lw["wg"], lw["wu"], lw["wd"])


def _group_samples(x1, idx, taff):
    b, n_tok, d = x1.shape
    cap = idx.shape[2]
    g = max(k for k in range(1, 9) if b % k == 0 and (k * n_tok) % (MOE_CHUNK // 2) == 0)
    off = (jnp.arange(b, dtype=jnp.int32) % g) * n_tok
    idx = idx + off[:, None, None]

    def merge(t):
        t = t.reshape(b // g, g, N_EXPERTS, cap)
        return jnp.swapaxes(t, 1, 2).reshape(b // g, N_EXPERTS, g * cap)

    return x1.reshape(b // g, g * n_tok, d), merge(idx), merge(taff)


def _swap_halves(w):
    half = w.shape[-1] // 2
    return jnp.concatenate([-w[..., half:], w[..., :half]], axis=-1)


def _swa_perm():
    cols = []
    for c in range(SWA_REP):
        for g in range(SWA_KV_HEADS):
            h = g * SWA_REP + c
            cols.extend(range(h * SWA_HEAD_DIM, (h + 1) * SWA_HEAD_DIM))
    return np.asarray(cols, np.int32)


def _layer_weights(w_in, q_g, w_uq, kv_g, w_ukv, conv_w, w_o, n1, n2, router_w):
    bf = jnp.bfloat16
    d = w_in.shape[0]
    offs = np.cumsum([0, MLA_Q_RANK, MLA_KV_RANK, MLA_ROPE, SWA_HEADS * SWA_HEAD_DIM,
                      SWA_KV_HEADS * SWA_HEAD_DIM, SWA_KV_HEADS * SWA_HEAD_DIM, CONV_CH, CONV_CH, CONV_CH])
    part = [w_in[:, offs[i]:offs[i + 1]] for i in range(9)]
    w_cq, w_ckv, w_kr, w_sq, w_sk, w_sv, w_cb, w_cc, w_cu = part
    perm = _swa_perm()

    kr_block = jnp.concatenate([w_kr, _swap_halves(w_kr), jnp.zeros((d, LANES - 2 * MLA_ROPE), w_in.dtype)], axis=1)
    win = jnp.concatenate([
        w_cq, w_ckv, kr_block,
        w_sq[:, perm], w_sk, w_sv, w_cb, w_cc, w_cu], axis=1).astype(bf)

    qd = MLA_NOPE + MLA_ROPE
    uq = w_uq.reshape(MLA_Q_RANK, MLA_HEADS, qd)
    zpad = jnp.zeros((MLA_Q_RANK, MLA_HEADS, HEAD_PAD - qd), w_uq.dtype)
    main = jnp.concatenate([uq, zpad], axis=-1)
    wuq = main.reshape(MLA_Q_RANK, -1).astype(bf)

    ukv = w_ukv.reshape(MLA_KV_RANK, MLA_HEADS, MLA_NOPE + MLA_V)
    wk = jnp.concatenate([ukv[..., :MLA_NOPE],
                          jnp.zeros((MLA_KV_RANK, MLA_HEADS, HEAD_PAD - MLA_NOPE), w_ukv.dtype)], axis=-1)
    wk = wk.reshape(MLA_KV_RANK, -1).astype(bf)
    wv = ukv[..., MLA_NOPE:].reshape(MLA_KV_RANK, -1).astype(bf)

    n_mla = MLA_HEADS * MLA_V
    n_swa = SWA_HEADS * SWA_HEAD_DIM
    wo = jnp.concatenate([w_o[:n_mla], w_o[n_mla:n_mla + n_swa][perm], w_o[n_mla + n_swa:]], axis=0).astype(bf)
    return dict(
        g1=n1.reshape(1, -1), win=win, qg=q_g.reshape(1, -1), wuq=wuq, kvg=kv_g.reshape(1, -1),
        wk=wk, wvt=wv.T, cw=conv_w, wo=wo, n2=n2.reshape(1, -1),
        rw=router_w.T.astype(bf))


def _axial_tables(n_tokens, rot_dim):
    rows = n_tokens // GRID_W
    row = jnp.repeat(jnp.arange(rows, dtype=jnp.float32), GRID_W)
    col = jnp.tile(jnp.arange(GRID_W, dtype=jnp.float32), rows)
    n_freq = rot_dim // 4
    inv = ROPE_BASE ** (-jnp.arange(n_freq, dtype=jnp.float32) / n_freq)
    ang = jnp.concatenate([row[:, None] * inv, col[:, None] * inv], axis=-1)
    return jnp.cos(ang), jnp.sin(ang)


def _row_table(n_ctx, n_lat):
    def with_ctx(cos, sin):
        one = jnp.ones((n_ctx, cos.shape[1]), jnp.float32)
        return (jnp.concatenate([cos, one], axis=0), jnp.concatenate([sin, 0.0 * one], axis=0))

    cm, sm = with_ctx(*_axial_tables(n_lat, MLA_ROPE))
    cs, ss = with_ctx(*_axial_tables(n_lat, SWA_HEAD_DIM))
    t = n_ctx + n_lat
    cm2 = jnp.concatenate([cm, cm], axis=1)
    sm2 = jnp.concatenate([sm, sm], axis=1)
    cs2 = jnp.concatenate([cs, cs], axis=1)
    ss2 = jnp.concatenate([-ss, ss], axis=1)
    pad = jnp.zeros((t, HEAD_PAD - MLA_NOPE - MLA_ROPE), jnp.float32)
    cq = jnp.concatenate([jnp.ones((t, MLA_NOPE), jnp.float32), cm2, pad], axis=1) * (MLA_SCALE * LOG2E)
    sq = jnp.concatenate([jnp.zeros((t, MLA_NOPE), jnp.float32), -sm, sm, pad], axis=1) * (MLA_SCALE * LOG2E)
    kr = jnp.concatenate([cm2, sm2, jnp.zeros((t, LANES - 2 * MLA_ROPE), jnp.float32)], axis=1)
    tab = jnp.concatenate([
        jnp.tile(cq, (1, MLA_HEADS)), jnp.tile(sq, (1, MLA_HEADS)), kr,
        jnp.tile(cs2, (1, SWA_HEADS)) * (SWA_SCALE * LOG2E), jnp.tile(ss2, (1, SWA_HEADS)) * (SWA_SCALE * LOG2E),
        jnp.tile(cs2, (1, SWA_KV_HEADS)), jnp.tile(ss2, (1, SWA_KV_HEADS))], axis=1)
    return tab


def _strict_upper(n):
    r = lax.broadcasted_iota(jnp.int32, (n, n), 0)
    c = lax.broadcasted_iota(jnp.int32, (n, n), 1)
    return (r < c).astype(jnp.bfloat16)


def kernel(x, c, ctx, c_ctx, ada_w, ada_b, norm1_g, w_in, mla_q_norm_g, mla_w_uq, mla_kv_norm_g, mla_w_ukv,
           swa_sink, conv_w, w_o, norm2_g, router_w, exp_w_gate, exp_w_up, exp_w_down, final_norm_g):
    b, s, d = x.shape
    n_ctx = ctx.shape[1]
    depth = ada_w.shape[0]
    assert d == D_MODEL and s % MLA_TQ == 0 and n_ctx == TILE and b + 1 <= MOD_ROWS
    cap_lat = EC_CAPACITY_FACTOR * s // N_EXPERTS
    cap_ctx = EC_CAPACITY_FACTOR * n_ctx // N_EXPERTS

    cc = jnp.concatenate([c, c_ctx[None, :], jnp.zeros((MOD_ROWS - b - 1, d), c.dtype)], axis=0)
    mod = _modulation(cc, ada_w, ada_b).reshape(depth, MOD_ROWS, 1, N_MOD * d)
    tab = _row_table(n_ctx, s)
    tri_lat = _strict_upper(s)
    tri_ctx = _strict_upper(n_ctx)
    swa_bias = _swa_bias(n_ctx, s)
    sink_slots = jnp.zeros((8,), jnp.float32)
    fg = final_norm_g.reshape(1, d)
    wg_all = exp_w_gate.astype(jnp.bfloat16).reshape((depth * N_EXPERTS,) + exp_w_gate.shape[2:])
    wu_all = exp_w_up.astype(jnp.bfloat16).reshape((depth * N_EXPERTS,) + exp_w_up.shape[2:])
    wd_all = exp_w_down.astype(jnp.bfloat16).reshape((depth * N_EXPERTS,) + exp_w_down.shape[2:])

    xl, xc = x, ctx
    for li in range(depth):
        last = li == depth - 1
        lw = _layer_weights(w_in[li], mla_q_norm_g[li], mla_w_uq[li], mla_kv_norm_g[li], mla_w_ukv[li],
                            conv_w[li], w_o[li], norm1_g[li], norm2_g[li], router_w[li])
        lw.update(wg=wg_all, wu=wu_all, wd=wd_all, expert0=li * N_EXPERTS)
        mod_l = mod[li]
        sink = sink_slots.at[:SWA_HEADS].set(swa_sink[li] * LOG2E)
        qm, km, vmt, qs, ks, vs, cz = _projections(xl, xc, mod_l, lw, tab)
        a = _mla_attention(qm, km, vmt, s)
        bsw = _swa_attention(sink, qs, ks, vs, swa_bias, s)
        x1, aff = _mixer_out(a, bsw, cz, xl, mod_l, None, lw, 0)
        idx, taff = _routing(aff, tri_lat, cap_lat)
        if not last:
            ac = _mla_attention_ctx(qm, km, vmt, s)
            bc = _swa_attention_ctx(sink, qs, ks, vs, s)
            xc1, affc = _mixer_out(ac, bc, cz, xc, mod_l, b, lw, s)
            idxc, taffc = _routing(affc, tri_ctx, cap_ctx)
            xc = _experts(*_group_samples(xc1, idxc, taffc), mod_l, b, lw, fg, final=False).reshape(xc1.shape)
        xl = _experts(x1, idx, taff, mod_l, None, lw, fg, final=last)
    return xl
```
